```python
import math
import jax, jax.numpy as jnp
from jax import lax
import numpy as np

D_MODEL = 1024
BATCH = 8
SEQ = 2048
DEPTH = 4
DEC_BATCH = 128
DEC_SEQ = 1
PAST_LEN = 8192
PAGE_SIZE = 128

N_AB = (DEPTH + 1) // 2
N_C = DEPTH // 2
A_CH = D_MODEL // 2
CONV_W = 31
N_HEADS = 8
N_KV_HEADS = 2
HEAD_DIM = 64
Q_GROUP = N_HEADS // N_KV_HEADS
WINDOW = 128
B_WIDTH = N_HEADS * HEAD_DIM
KV_WIDTH = N_KV_HEADS * HEAD_DIM
AB_IN = 2 * A_CH + B_WIDTH + 2 * KV_WIDTH
AB_OUT = A_CH + B_WIDTH
ATTN_SCALE = HEAD_DIM ** -0.5
C_WIDTH = D_MODEL
C_GROUPS = 8
C_GDIM = C_WIDTH // C_GROUPS
C_CHUNK = 128
D_FF = 2816
N_EXPERTS = 8
TOP_K = 2
D_FF_EXPERT = 3584
EPS = 1e-6

kernel_name = "hybrid_conv_swa_gmlp_moe_decode_step"


def rms_norm(x, g):
    xf = x.astype(jnp.float32)
    y = xf * lax.rsqrt(jnp.mean(xf * xf, axis=-1, keepdims=True) + EPS)
    return (y * g.astype(jnp.float32)).astype(x.dtype)


def layer_norm(x, g, b):
    xf = x.astype(jnp.float32)
    mu = jnp.mean(xf, axis=-1, keepdims=True)
    xc = xf - mu
    var = jnp.mean(xc * xc, axis=-1, keepdims=True)
    y = xc * lax.rsqrt(var + EPS) * g.astype(jnp.float32) + b.astype(jnp.float32)
    return y.astype(x.dtype)


def alibi_slopes():
    m = jnp.exp2(-8.0 * jnp.arange(1, N_HEADS + 1, dtype=jnp.float32) / N_HEADS)
    return m.reshape(N_KV_HEADS, Q_GROUP, 1, 1)


def ab_project(h, w_in, q_g, k_g):
    z = h @ w_in
    lead = z.shape[:-1]
    a = z[..., :A_CH] * jax.nn.sigmoid(z[..., A_CH:2 * A_CH])
    o = 2 * A_CH
    q = z[..., o:o + B_WIDTH].reshape(*lead, N_HEADS, HEAD_DIM)
    o += B_WIDTH
    k = z[..., o:o + KV_WIDTH].reshape(*lead, N_KV_HEADS, HEAD_DIM)
    o += KV_WIDTH
    v = z[..., o:o + KV_WIDTH].reshape(*lead, N_KV_HEADS, HEAD_DIM)
    return a, rms_norm(q, q_g), rms_norm(k, k_g), v


def dwconv_valid(x_full, w, b):
    y = lax.conv_general_dilated(
        x_full, w[:, None, :].astype(x_full.dtype), window_strides=(1,), padding='VALID',
        dimension_numbers=('NWC', 'WIO', 'NWC'), feature_group_count=A_CH)
    return y + b


def conv_tail(c, ln_g, ln_b):
    return jax.nn.silu(layer_norm(c, ln_g, ln_b))


def sink_probs(s, sink):
    m = jnp.maximum(jnp.max(s, axis=-1, keepdims=True), sink)
    p = jnp.exp(s - m)
    return p / (jnp.sum(p, axis=-1, keepdims=True) + jnp.exp(sink - m))


def swa_prompt(q, k, v, sinks):
    n, t = q.shape[:2]
    nb = t // WINDOW
    qb = q.reshape(n, nb, WINDOW, N_KV_HEADS, Q_GROUP, HEAD_DIM)
    kp = jnp.concatenate([jnp.zeros_like(k[:, :WINDOW]), k], axis=1).reshape(n, nb + 1, WINDOW, N_KV_HEADS, HEAD_DIM)
    vp = jnp.concatenate([jnp.zeros_like(v[:, :WINDOW]), v], axis=1).reshape(n, nb + 1, WINDOW, N_KV_HEADS, HEAD_DIM)
    kb = jnp.concatenate([kp[:, :-1], kp[:, 1:]], axis=2)
    vb = jnp.concatenate([vp[:, :-1], vp[:, 1:]], axis=2)
    s = jnp.einsum('bnqkgd,bnskd->bnkgqs', qb, kb, preferred_element_type=jnp.float32) * ATTN_SCALE
    i = jnp.arange(WINDOW)[:, None]
    j = jnp.arange(2 * WINDOW)[None, :]
    dist = (WINDOW + i - j).astype(jnp.float32)
    in_window = (dist >= 0) & (dist < WINDOW)
    key_exists = (jnp.arange(nb)[:, None, None] > 0) | (j >= WINDOW)[None]
    valid = in_window[None] & key_exists
    s = s - alibi_slopes() * dist
    s = jnp.where(valid[None, :, None, None], s, -jnp.inf)
    p = sink_probs(s, sinks.astype(jnp.float32).reshape(N_KV_HEADS, Q_GROUP, 1, 1))
    o = jnp.einsum('bnkgqs,bnskd->bnqkgd', p.astype(v.dtype), vb)
    return o.reshape(n, t, B_WIDTH)


def swa_sample(q, k_new, v_new, k_buf, v_buf, sinks):
    n, t = q.shape[:2]
    kk = jnp.concatenate([k_buf, k_new], axis=1)
    vv = jnp.concatenate([v_buf, v_new], axis=1)
    qg = q.reshape(n, t, N_KV_HEADS, Q_GROUP, HEAD_DIM)
    s = jnp.einsum('bqkgd,bskd->bkgqs', qg, kk, preferred_element_type=jnp.float32) * ATTN_SCALE
    i = jnp.arange(t)[:, None]
    j = jnp.arange(WINDOW + t)[None, :]
    dist = (WINDOW + i - j).astype(jnp.float32)
    valid = (dist >= 0) & (dist < WINDOW)
    s = s - alibi_slopes() * dist
    s = jnp.where(valid, s, -jnp.inf)
    p = sink_probs(s, sinks.astype(jnp.float32).reshape(N_KV_HEADS, Q_GROUP, 1, 1))
    o = jnp.einsum('bkgqs,bskd->bqkgd', p.astype(vv.dtype), vv)
    return o.reshape(n, t, B_WIDTH), kk[:, t:], vv[:, t:]


def chunk_spatial_gate(h, ln_g, ln_b, w_s, b_s):
    n, t = h.shape[:2]
    u = h[..., :C_WIDTH]
    v = layer_norm(h[..., C_WIDTH:], ln_g, ln_b)
    vb = v.reshape(n, t // C_CHUNK, C_CHUNK, C_GROUPS, C_GDIM)
    w = w_s * jnp.tril(jnp.ones((C_CHUNK, C_CHUNK), w_s.dtype))
    mixed = jnp.einsum('gij,ncjgd->ncigd', w, vb) + b_s.T[None, None, :, :, None]
    return u * mixed.reshape(n, t, C_WIDTH), v


def dense_swiglu(x, wg, wu, wd):
    return (jax.nn.silu(x @ wg) * (x @ wu)) @ wd


def moe_swiglu(x, w_router, wg, wu, wd):
    shp = x.shape
    xf = x.reshape(-1, shp[-1])
    logits = jnp.dot(xf, w_router, preferred_element_type=jnp.float32)
    top_val, top_idx = lax.top_k(logits, TOP_K)
    gates = jax.nn.softmax(top_val, axis=-1)
    combine = jnp.einsum('nk,nke->ne', gates, jax.nn.one_hot(top_idx, N_EXPERTS, dtype=jnp.float32)).astype(x.dtype)
    y = jnp.zeros_like(xf)
    for e in range(N_EXPERTS):
        y = y + combine[:, e:e + 1] * dense_swiglu(xf, wg[e], wu[e], wd[e])
    return y.reshape(shp)


def setup_inputs(seed: int = 0) -> dict:
    key = jax.random.key(seed)
    ks = jax.random.split(key, 32)

    def nrm(k, shape, scale):
        return scale * jax.random.normal(k, shape, jnp.float32)

    def gain(k, shape):
        return 1.0 + 0.02 * jax.random.normal(k, shape, jnp.float32)

    return {
        "x_prompt": nrm(ks[0], (BATCH, SEQ, D_MODEL), 1.0),
        "x_sample": nrm(ks[1], (DEC_BATCH, DEC_SEQ, D_MODEL), 1.0),
        "state_conv": nrm(ks[2], (N_AB, DEC_BATCH, CONV_W - 1, A_CH), 1.0),
        "cache_k": nrm(ks[3], (N_AB, DEC_BATCH, WINDOW, N_KV_HEADS, HEAD_DIM), 1.0),
        "cache_v": nrm(ks[4], (N_AB, DEC_BATCH, WINDOW, N_KV_HEADS, HEAD_DIM), 1.0),
        "norm_mix_g": gain(ks[5], (DEPTH, D_MODEL)),
        "norm_ffn_g": gain(ks[6], (DEPTH, D_MODEL)),
        "w_in_ab": nrm(ks[7], (N_AB, D_MODEL, AB_IN), D_MODEL ** -0.5),
        "conv_w": nrm(ks[8], (N_AB, CONV_W, A_CH), CONV_W ** -0.5),
        "conv_b": nrm(ks[9], (N_AB, A_CH), 0.02),
        "conv_ln_g": gain(ks[10], (N_AB, A_CH)),
        "conv_ln_b": nrm(ks[11], (N_AB, A_CH), 0.02),
        "q_norm_g": gain(ks[12], (N_AB, HEAD_DIM)),
        "k_norm_g": gain(ks[13], (N_AB, HEAD_DIM)),
        "attn_sinks": nrm(ks[14], (N_AB, N_HEADS), 0.5),
        "w_out_ab": nrm(ks[15], (N_AB, AB_OUT, D_MODEL), AB_OUT ** -0.5),
        "w_gate_dense": nrm(ks[16], (N_AB, D_MODEL, D_FF), D_MODEL ** -0.5),
        "w_up_dense": nrm(ks[17], (N_AB, D_MODEL, D_FF), D_MODEL ** -0.5),
        "w_down_dense": nrm(ks[18], (N_AB, D_FF, D_MODEL), D_FF ** -0.5),
        "w_in_c": nrm(ks[19], (N_C, D_MODEL, 2 * C_WIDTH), D_MODEL ** -0.5),
        "c_ln_g": gain(ks[20], (N_C, C_WIDTH)),
        "c_ln_b": nrm(ks[21], (N_C, C_WIDTH), 0.02),
        "w_spatial": nrm(ks[22], (N_C, C_GROUPS, C_CHUNK, C_CHUNK), C_CHUNK ** -0.5),
        "b_spatial": gain(ks[23], (N_C, C_GROUPS, C_CHUNK)),
        "w_out_c": nrm(ks[24], (N_C, C_WIDTH, D_MODEL), C_WIDTH ** -0.5),
        "w_router": nrm(ks[25], (N_C, D_MODEL, N_EXPERTS), D_MODEL ** -0.5),
        "w_gate_exp": nrm(ks[26], (N_C, N_EXPERTS, D_MODEL, D_FF_EXPERT), D_MODEL ** -0.5),
        "w_up_exp": nrm(ks[27], (N_C, N_EXPERTS, D_MODEL, D_FF_EXPERT), D_MODEL ** -0.5),
        "w_down_exp": nrm(ks[28], (N_C, N_EXPERTS, D_FF_EXPERT, D_MODEL), D_FF_EXPERT ** -0.5),
    }


def reference(x_prompt, x_sample, state_conv, cache_k, cache_v, norm_mix_g, norm_ffn_g,
              w_in_ab, conv_w, conv_b, conv_ln_g, conv_ln_b, q_norm_g, k_norm_g, attn_sinks,
              w_out_ab, w_gate_dense, w_up_dense, w_down_dense, w_in_c, c_ln_g, c_ln_b,
              w_spatial, b_spatial, w_out_c, w_router, w_gate_exp, w_up_exp, w_down_exp):
    xp, xs = x_prompt, x_sample
    dec_seq = x_sample.shape[1]
    pad_len = -(-dec_seq // C_CHUNK) * C_CHUNK - dec_seq
    conv_p, conv_s, kp_out, vp_out, ks_out, vs_out, vc_out = [], [], [], [], [], [], []
    for layer in range(DEPTH):
        i = layer // 2
        hp = rms_norm(xp, norm_mix_g[layer])
        hs = rms_norm(xs, norm_mix_g[layer])
        if layer % 2 == 0:
            ap, qp, kp, vp = ab_project(hp, w_in_ab[i], q_norm_g[i], k_norm_g[i])
            as_, qs, ks_, vs_ = ab_project(hs, w_in_ab[i], q_norm_g[i], k_norm_g[i])
            ap_full = jnp.concatenate([jnp.zeros_like(ap[:, :CONV_W - 1]), ap], axis=1)
            ca_p = conv_tail(dwconv_valid(ap_full, conv_w[i], conv_b[i]), conv_ln_g[i], conv_ln_b[i])
            as_full = jnp.concatenate([state_conv[i].astype(as_.dtype), as_], axis=1)
            ca_s = conv_tail(dwconv_valid(as_full, conv_w[i], conv_b[i]), conv_ln_g[i], conv_ln_b[i])
            conv_p.append(ap[:, ap.shape[1] - (CONV_W - 1):])
            conv_s.append(as_full[:, dec_seq:])
            ob_p = swa_prompt(qp, kp, vp, attn_sinks[i])
            ob_s, kbuf, vbuf = swa_sample(qs, ks_, vs_, cache_k[i].astype(ks_.dtype),
                                          cache_v[i].astype(vs_.dtype), attn_sinks[i])
            kp_out.append(kp[:, kp.shape[1] - WINDOW:])
            vp_out.append(vp[:, vp.shape[1] - WINDOW:])
            ks_out.append(kbuf)
            vs_out.append(vbuf)
            xp = xp + jnp.concatenate([ca_p, ob_p], axis=-1) @ w_out_ab[i]
            xs = xs + jnp.concatenate([ca_s, ob_s], axis=-1) @ w_out_ab[i]
            hp = rms_norm(xp, norm_ffn_g[layer])
            hs = rms_norm(xs, norm_ffn_g[layer])
            xp = xp + dense_swiglu(hp, w_gate_dense[i], w_up_dense[i], w_down_dense[i])
            xs = xs + dense_swiglu(hs, w_gate_dense[i], w_up_dense[i], w_down_dense[i])
        else:
            zp = jax.nn.gelu(hp @ w_in_c[i])
            zs = jax.nn.gelu(hs @ w_in_c[i])
            gp, _ = chunk_spatial_gate(zp, c_ln_g[i], c_ln_b[i], w_spatial[i], b_spatial[i])
            zs_pad = jnp.pad(zs, ((0, 0), (0, pad_len), (0, 0)))
            gs, vs_norm = chunk_spatial_gate(zs_pad, c_ln_g[i], c_ln_b[i], w_spatial[i], b_spatial[i])
            vc_out.append(vs_norm[:, :dec_seq])
            xp = xp + gp @ w_out_c[i]
            xs = xs + gs[:, :dec_seq] @ w_out_c[i]
            hp = rms_norm(xp, norm_ffn_g[layer])
            hs = rms_norm(xs, norm_ffn_g[layer])
            xp = xp + moe_swiglu(hp, w_router[i], w_gate_exp[i], w_up_exp[i], w_down_exp[i])
            xs = xs + moe_swiglu(hs, w_router[i], w_gate_exp[i], w_up_exp[i], w_down_exp[i])
    return (xp, xs, jnp.stack(conv_p), jnp.stack(conv_s), jnp.stack(kp_out), jnp.stack(vp_out),
            jnp.stack(ks_out), jnp.stack(vs_out), jnp.stack(vc_out))
```

```python
import functools

import jax
import jax.numpy as jnp
from jax import lax
from jax.experimental import pallas as pl
from jax.experimental.pallas import tpu as pltpu

F32 = jnp.float32
BF16 = jnp.bfloat16

D_MODEL = 1024
BATCH = 8
SEQ = 2048
DEPTH = 4
DEC_BATCH = 128
A_CH = 512
CONV_W = 31
N_HEADS = 8
N_KV_HEADS = 2
HEAD_DIM = 64
Q_GROUP = 4
WINDOW = 128
B_WIDTH = 512
KV_WIDTH = 128
AB_IN = 2 * A_CH + B_WIDTH + 2 * KV_WIDTH
C_WIDTH = 1024
C_GROUPS = 8
C_CHUNK = 128
D_FF = 2816
N_EXPERTS = 8
D_FF_EXPERT = 3584
EPS = 1e-6
ATTN_SCALE = HEAD_DIM ** -0.5

N_PROMPT = BATCH * SEQ
N_TOK = N_PROMPT + DEC_BATCH
ROW_TILE = 1024
FF_SPLIT = 2
MOE_TILE = 1024
MOE_FF_TILE = 512
MOE_NF = D_FF_EXPERT // MOE_FF_TILE
MOE_MAX_TILES = (2 * N_TOK + N_EXPERTS * (MOE_TILE - 1)) // MOE_TILE
COMBINE_TILE = 512
VMEM_LIMIT = 56 * 1024 * 1024


def _params(n_axes, vmem=VMEM_LIMIT):
    return pltpu.CompilerParams(dimension_semantics=("arbitrary",) * n_axes, vmem_limit_bytes=vmem)


def _rms(x, g):
    return x * lax.rsqrt(jnp.mean(x * x, axis=-1, keepdims=True) + EPS) * g


def _layer_norm(x, g, b):
    mu = jnp.mean(x, axis=-1, keepdims=True)
    xc = x - mu
    var = jnp.mean(xc * xc, axis=-1, keepdims=True)
    return xc * lax.rsqrt(var + EPS) * g + b


def _const_spec(shape):
    return pl.BlockSpec(shape, lambda *_: (0,) * len(shape))


def _inproj_kernel(x_ref, g_ref, w_ref, qg_ref, kg_ref, seg_ref, a_ref, q_ref, k_ref, v_ref):
    h = _rms(x_ref[...], g_ref[...]).astype(BF16)
    z = jnp.dot(h, w_ref[...], preferred_element_type=F32)
    a_ref[...] = z[:, :A_CH] * jax.nn.sigmoid(z[:, A_CH:2 * A_CH])
    seg = seg_ref[...]

    def head_norm(t, gain, seg_t):
        sq = t * t
        hi = sq.astype(BF16)
        lo = (sq - hi.astype(F32)).astype(BF16)
        ms = (jnp.dot(hi, seg_t, preferred_element_type=F32)
              + jnp.dot(lo, seg_t, preferred_element_type=F32)) * (1.0 / HEAD_DIM)
        return t * lax.rsqrt(ms + EPS) * gain

    o = 2 * A_CH
    q = head_norm(z[:, o:o + B_WIDTH], qg_ref[...], seg)
    q_ref[...] = (q * ATTN_SCALE).astype(BF16)
    o += B_WIDTH
    k_ref[...] = head_norm(z[:, o:o + KV_WIDTH], kg_ref[...], seg[:KV_WIDTH, :KV_WIDTH])
    o += KV_WIDTH
    v_ref[...] = z[:, o:o + KV_WIDTH]


def _inproj(x, g, w_bf, qg, kg, seg, tm):
    n = x.shape[0]
    row = lambda w: pl.BlockSpec((tm, w), lambda i: (i, 0))
    return pl.pallas_call(
        _inproj_kernel,
        grid=(n // tm,),
        in_specs=[row(D_MODEL), _const_spec((1, D_MODEL)), _const_spec((D_MODEL, AB_IN)),
                  _const_spec((1, B_WIDTH)), _const_spec((1, KV_WIDTH)), _const_spec((B_WIDTH, B_WIDTH))],
        out_specs=[row(A_CH), row(B_WIDTH), row(KV_WIDTH), row(KV_WIDTH)],
        out_shape=[jax.ShapeDtypeStruct((n, A_CH), F32), jax.ShapeDtypeStruct((n, B_WIDTH), BF16),
                   jax.ShapeDtypeStruct((n, KV_WIDTH), F32), jax.ShapeDtypeStruct((n, KV_WIDTH), F32)],
        compiler_params=_params(1),
        name="inproj_ab",
    )(x, g, w_bf, qg, kg, seg)


CONV_PAD = 32
CONV_ROWS = 128


def _conv_tail(c, lg, lb):
    y = _layer_norm(c, lg, lb)
    return y * jax.nn.sigmoid(y)


def _conv_prompt_kernel(a_ref, w_ref, b_ref, lg_ref, lb_ref, o_ref, pad_ref, win_ref, c_ref):
    pad_ref[0:CONV_PAD, :] = jnp.zeros((CONV_PAD, A_CH), F32)
    pad_ref[CONV_PAD:, :] = a_ref[...]

    def body(r, carry):
        t0 = pl.multiple_of(r * CONV_ROWS, CONV_ROWS)
        win_ref[...] = pad_ref[pl.ds(t0, CONV_ROWS + CONV_PAD), :]
        for c in range(A_CH // 128):
            lanes = slice(c * 128, (c + 1) * 128)
            acc = jnp.zeros((CONV_ROWS, 128), F32)
            for k in range(CONV_W):
                start = CONV_PAD - (CONV_W - 1) + k
                acc = acc + win_ref[start:start + CONV_ROWS, lanes] * w_ref[k:k + 1, lanes]
            c_ref[:, lanes] = acc
        y = _conv_tail(c_ref[...] + b_ref[...], lg_ref[...], lb_ref[...])
        o_ref[pl.ds(t0, CONV_ROWS), :] = y.astype(BF16)
        return carry

    lax.fori_loop(0, SEQ // CONV_ROWS, body, 0)


def _conv_prompt(a, w, b, lg, lb):
    return pl.pallas_call(
        _conv_prompt_kernel,
        grid=(BATCH,),
        in_specs=[pl.BlockSpec((SEQ, A_CH), lambda i: (i, 0)), _const_spec((CONV_W, A_CH)),
                  _const_spec((1, A_CH)), _const_spec((1, A_CH)), _const_spec((1, A_CH))],
        out_specs=pl.BlockSpec((SEQ, A_CH), lambda i: (i, 0)),
        out_shape=jax.ShapeDtypeStruct((N_PROMPT, A_CH), BF16),
        scratch_shapes=[pltpu.VMEM((CONV_PAD + SEQ, A_CH), F32), pltpu.VMEM((CONV_ROWS + CONV_PAD, A_CH), F32),
                        pltpu.VMEM((CONV_ROWS, A_CH), F32)],
        compiler_params=_params(1),
        name="conv_prompt",
    )(a, w, b, lg, lb)


def _conv_sample_kernel(a_ref, st_ref, w_ref, b_ref, lg_ref, lb_ref, o_ref):
    acc = a_ref[...] * w_ref[CONV_W - 1:CONV_W, :]
    for k in range(CONV_W - 1):
        acc = acc + st_ref[k] * w_ref[k:k + 1, :]
    o_ref[...] = _conv_tail(acc + b_ref[...], lg_ref[...], lb_ref[...]).astype(BF16)


def _conv_sample(a, st_t, w, b, lg, lb):
    return pl.pallas_call(
        _conv_sample_kernel,
        grid=(1,),
        in_specs=[_const_spec((DEC_BATCH, A_CH)), _const_spec((CONV_W - 1, DEC_BATCH, A_CH)),
                  _const_spec((CONV_W, A_CH)), _const_spec((1, A_CH)), _const_spec((1, A_CH)),
                  _const_spec((1, A_CH))],
        out_specs=_const_spec((DEC_BATCH, A_CH)),
        out_shape=jax.ShapeDtypeStruct((DEC_BATCH, A_CH), BF16),
        compiler_params=_params(1),
        name="conv_sample",
    )(a, st_t, w, b, lg, lb)


def _alibi_slope(h):
    return 2.0 ** (-8.0 * (h + 1) / N_HEADS)


def _attn_prompt_kernel(sink_ref, q_ref, kp_ref, kc_ref, vp_ref, vc_ref, o_ref):
    blk = pl.program_id(1)
    kk = jnp.concatenate([kp_ref[...], kc_ref[...]], axis=0).astype(BF16)
    vv = jnp.concatenate([vp_ref[...], vc_ref[...]], axis=0).astype(BF16)
    qi = lax.broadcasted_iota(jnp.int32, (WINDOW, 2 * WINDOW), 0)
    kj = lax.broadcasted_iota(jnp.int32, (WINDOW, 2 * WINDOW), 1)
    dist_i = WINDOW + qi - kj
    lo = jnp.where(blk > 0, 0, WINDOW)
    valid = (dist_i >= 0) & (dist_i < WINDOW) & (kj >= lo)
    dist = dist_i.astype(F32)
    outs = []
    for h in range(N_HEADS):
        kv = h // Q_GROUP
        qh = q_ref[:, h * HEAD_DIM:(h + 1) * HEAD_DIM]
        kh = kk[:, kv * HEAD_DIM:(kv + 1) * HEAD_DIM]
        s = lax.dot_general(qh, kh, (((1,), (1,)), ((), ())), preferred_element_type=F32)
        s = jnp.where(valid, s - _alibi_slope(h) * dist, -jnp.inf)
        sink = sink_ref[h]
        m = jnp.maximum(jnp.max(s, axis=-1, keepdims=True), sink)
        p = jnp.exp(s - m)
        denom = jnp.sum(p, axis=-1, keepdims=True) + jnp.exp(sink - m)
        p = (p / denom).astype(BF16)
        outs.append(jnp.dot(p, vv[:, kv * HEAD_DIM:(kv + 1) * HEAD_DIM], preferred_element_type=F32))
    o_ref[...] = jnp.concatenate(outs, axis=-1).astype(BF16)


def _attn_prompt(q, k, v, sinks):
    nb = SEQ // WINDOW
    own = lambda b, j: (b * nb + j, 0)
    prev = lambda b, j: (b * nb + jnp.maximum(j - 1, 0), 0)
    return pl.pallas_call(
        _attn_prompt_kernel,
        grid=(BATCH, nb),
        in_specs=[pl.BlockSpec(memory_space=pltpu.SMEM),
                  pl.BlockSpec((WINDOW, B_WIDTH), own),
                  pl.BlockSpec((WINDOW, KV_WIDTH), prev), pl.BlockSpec((WINDOW, KV_WIDTH), own),
                  pl.BlockSpec((WINDOW, KV_WIDTH), prev), pl.BlockSpec((WINDOW, KV_WIDTH), own)],
        out_specs=pl.BlockSpec((WINDOW, B_WIDTH), own),
        out_shape=jax.ShapeDtypeStruct((N_PROMPT, B_WIDTH), BF16),
        compiler_params=_params(2),
        name="attn_prompt",
    )(sinks, q, k, k, v, v)


ATTN_S_BLK = 32


def _attn_sample_kernel(sink_ref, q_ref, kn_ref, vn_ref, kc_ref, vc_ref, o_ref):
    kc = kc_ref[...].astype(BF16)
    vc = vc_ref[...].astype(BF16)
    j = lax.broadcasted_iota(jnp.int32, (1, 1, WINDOW), 2)
    dist = (WINDOW - j).astype(F32)
    g_idx = lax.broadcasted_iota(jnp.int32, (1, Q_GROUP, 1), 1)
    for kv in range(N_KV_HEADS):
        lanes = slice(kv * HEAD_DIM, (kv + 1) * HEAD_DIM)
        qg = q_ref[:, kv]
        slope = jnp.exp2(-(g_idx + (kv * Q_GROUP + 1)).astype(F32) * (8.0 / N_HEADS))
        s = jnp.einsum('ngd,njd->ngj', qg, kc[:, :, lanes], preferred_element_type=F32)
        s = jnp.where(j >= 1, s - slope * dist, -jnp.inf)
        kn = kn_ref[:, kv].astype(BF16).astype(F32)
        s_new = jnp.sum(qg.astype(F32) * kn[:, None, :], axis=-1, keepdims=True)
        sink = sink_ref[kv][:, :1][None]
        m = jnp.maximum(jnp.maximum(jnp.max(s, axis=-1, keepdims=True), s_new), sink)
        p = jnp.exp(s - m)
        p_new = jnp.exp(s_new - m)
        denom = jnp.sum(p, axis=-1, keepdims=True) + p_new + jnp.exp(sink - m)
        o = jnp.einsum('ngj,njd->ngd', (p / denom).astype(BF16), vc[:, :, lanes], preferred_element_type=F32)
        vn = vn_ref[:, kv].astype(BF16).astype(F32)
        o = o + (p_new / denom).astype(BF16).astype(F32) * vn[:, None, :]
        o_ref[:, kv] = o.astype(BF16)


def _attn_sample(q4, kn3, vn3, kc, vc, sinks_b):
    nb = ATTN_S_BLK
    return pl.pallas_call(
        _attn_sample_kernel,
        grid=(DEC_BATCH // nb,),
        in_specs=[_const_spec((N_KV_HEADS, Q_GROUP, 128)),
                  pl.BlockSpec((nb, N_KV_HEADS, Q_GROUP, HEAD_DIM), lambda i: (i, 0, 0, 0)),
                  pl.BlockSpec((nb, N_KV_HEADS, HEAD_DIM), lambda i: (i, 0, 0)),
                  pl.BlockSpec((nb, N_KV_HEADS, HEAD_DIM), lambda i: (i, 0, 0)),
                  pl.BlockSpec((nb, WINDOW, KV_WIDTH), lambda i: (i, 0, 0)),
                  pl.BlockSpec((nb, WINDOW, KV_WIDTH), lambda i: (i, 0, 0))],
        out_specs=pl.BlockSpec((nb, N_KV_HEADS, Q_GROUP, HEAD_DIM), lambda i: (i, 0, 0, 0)),
        out_shape=jax.ShapeDtypeStruct((DEC_BATCH, N_KV_HEADS, Q_GROUP, HEAD_DIM), BF16),
        compiler_params=_params(1),
        name="attn_sample",
    )(sinks_b, q4, kn3, vn3, kc, vc)


def _outproj_ffn_kernel(x_ref, ca_ref, ob_ref, g_ref, wo_hbm, wg_hbm, wu_hbm, wd_hbm, o_ref,
                        wo_ref, wg_ref, wu_ref, wd_ref):
    @pl.when(pl.program_id(0) == 0)
    def _():
        pltpu.sync_copy(wo_hbm, wo_ref)
        pltpu.sync_copy(wg_hbm, wg_ref)
        pltpu.sync_copy(wu_hbm, wu_ref)
        pltpu.sync_copy(wd_hbm, wd_ref)

    o_ref[...] = (x_ref[...]
                  + jnp.dot(ca_ref[...], wo_ref[:A_CH, :], preferred_element_type=F32)
                  + jnp.dot(ob_ref[...], wo_ref[A_CH:, :], preferred_element_type=F32))
    h = _rms(o_ref[...], g_ref[...]).astype(BF16)
    ff = D_FF // FF_SPLIT
    for c in range(FF_SPLIT):
        cols = slice(c * ff, (c + 1) * ff)
        gate = jnp.dot(h, wg_ref[:, cols], preferred_element_type=F32)
        up = jnp.dot(h, wu_ref[:, cols], preferred_element_type=F32)
        act = (gate * jax.nn.sigmoid(gate) * up).astype(BF16)
        o_ref[...] += jnp.dot(act, wd_ref[cols, :], preferred_element_type=F32)


def _outproj_ffn(x, ca, ob, wo, g, wg, wu, wd, tm):
    n = x.shape[0]
    row = lambda w: pl.BlockSpec((tm, w), lambda i: (i, 0))
    hbm = pl.BlockSpec(memory_space=pl.ANY)
    return pl.pallas_call(
        _outproj_ffn_kernel,
        grid=(n // tm,),
        in_specs=[row(D_MODEL), row(A_CH), row(B_WIDTH), _const_spec((1, D_MODEL)), hbm, hbm, hbm, hbm],
        out_specs=row(D_MODEL),
        out_shape=jax.ShapeDtypeStruct((n, D_MODEL), F32),
        scratch_shapes=[pltpu.VMEM((D_MODEL, D_MODEL), BF16), pltpu.VMEM((D_MODEL, D_FF), BF16),
                        pltpu.VMEM((D_MODEL, D_FF), BF16), pltpu.VMEM((D_FF, D_MODEL), BF16)],
        compiler_params=_params(1),
        name="outproj_ffn",
    )(x, ca, ob, g, wo, wg, wu, wd)


def _gmlp_prompt_kernel(x_ref, g_ref, win_ref, lg_ref, lb_ref, ws_ref, bs_ref, wout_ref, o_ref, gate_ref):
    x = x_ref[...]
    h = _rms(x, g_ref[...]).astype(BF16)
    z = jax.nn.gelu(jnp.dot(h, win_ref[...], preferred_element_type=F32))
    u = z[:, :C_WIDTH]
    v = _layer_norm(z[:, C_WIDTH:], lg_ref[...], lb_ref[...]).astype(BF16)
    ri = lax.broadcasted_iota(jnp.int32, (C_CHUNK, C_CHUNK), 0)
    ci = lax.broadcasted_iota(jnp.int32, (C_CHUNK, C_CHUNK), 1)
    tm = x.shape[0]
    gd = C_WIDTH // C_GROUPS
    for grp in range(C_GROUPS):
        w = jnp.where(ri >= ci, ws_ref[grp], 0.0).astype(BF16)
        b = bs_ref[:, grp:grp + 1]
        lanes = slice(grp * gd, (grp + 1) * gd)
        for c in range(tm // C_CHUNK):
            rows = slice(c * C_CHUNK, (c + 1) * C_CHUNK)
            mixed = jnp.dot(w, v[rows, lanes], preferred_element_type=F32) + b
            gate_ref[rows, lanes] = (u[rows, lanes] * mixed).astype(BF16)
    o_ref[...] = x + jnp.dot(gate_ref[...], wout_ref[...], preferred_element_type=F32)


def _gmlp_prompt(x, g, win, lg, lb, ws, bs_t, wout, tm):
    n = x.shape[0]
    row = lambda w: pl.BlockSpec((tm, w), lambda i: (i, 0))
    return pl.pallas_call(
        _gmlp_prompt_kernel,
        grid=(n // tm,),
        in_specs=[row(D_MODEL), _const_spec((1, D_MODEL)), _const_spec((D_MODEL, 2 * C_WIDTH)),
                  _const_spec((1, C_WIDTH)), _const_spec((1, C_WIDTH)),
                  _const_spec((C_GROUPS, C_CHUNK, C_CHUNK)), _const_spec((C_CHUNK, C_GROUPS)),
                  _const_spec((C_WIDTH, D_MODEL))],
        out_specs=row(D_MODEL),
        out_shape=jax.ShapeDtypeStruct((n, D_MODEL), F32),
        scratch_shapes=[pltpu.VMEM((tm, C_WIDTH), BF16)],
        compiler_params=_params(1),
        name="gmlp_prompt",
    )(x, g, win, lg, lb, ws, bs_t, wout)


def _gmlp_sample_kernel(x_ref, g_ref, win_ref, lg_ref, lb_ref, w0_ref, b0_ref, wout_ref, o_ref, v_ref):
    x = x_ref[...]
    h = _rms(x, g_ref[...]).astype(BF16)
    z = jax.nn.gelu(jnp.dot(h, win_ref[...], preferred_element_type=F32))
    v = _layer_norm(z[:, C_WIDTH:], lg_ref[...], lb_ref[...])
    v_ref[...] = v
    gated = z[:, :C_WIDTH] * (w0_ref[...] * v + b0_ref[...])
    o_ref[...] = x + jnp.dot(gated.astype(BF16), wout_ref[...], preferred_element_type=F32)


def _gmlp_sample(x, g, win, lg, lb, w0, b0, wout):
    n = x.shape[0]
    return pl.pallas_call(
        _gmlp_sample_kernel,
        grid=(1,),
        in_specs=[_const_spec((n, D_MODEL)), _const_spec((1, D_MODEL)), _const_spec((D_MODEL, 2 * C_WIDTH)),
                  _const_spec((1, C_WIDTH)), _const_spec((1, C_WIDTH)), _const_spec((1, C_WIDTH)),
                  _const_spec((1, C_WIDTH)), _const_spec((C_WIDTH, D_MODEL))],
        out_specs=[_const_spec((n, D_MODEL)), _const_spec((n, C_WIDTH))],
        out_shape=[jax.ShapeDtypeStruct((n, D_MODEL), F32), jax.ShapeDtypeStruct((n, C_WIDTH), F32)],
        compiler_params=_params(1),
        name="gmlp_sample",
    )(x, g, win, lg, lb, w0, b0, wout)


META_W = 8


def _router_kernel(x_ref, g_ref, wr_ref, base_ref, *rest):
    hn_ref, meta_ref, cnt_ref, run_ref = rest[-4:]
    step = pl.program_id(0)

    @pl.when(step == 0)
    def _():
        run_ref[...] = base_ref[...]

    h = _rms(x_ref[...], g_ref[...])
    hn_ref[...] = h
    tm = h.shape[0]
    logits = jnp.dot(h, wr_ref[...], preferred_element_type=F32, precision=lax.Precision.HIGHEST)
    lane = lax.broadcasted_iota(jnp.int32, (tm, 128), 1).astype(F32)
    neg = -jnp.inf
    lg = jnp.where(lane < N_EXPERTS, logits, neg)
    m1 = jnp.max(lg, axis=-1, keepdims=True)
    i1 = jnp.min(jnp.where(lg == m1, lane, 128.0), axis=-1, keepdims=True)
    lg2 = jnp.where(lane == i1, neg, lg)
    m2 = jnp.max(lg2, axis=-1, keepdims=True)
    i2 = jnp.min(jnp.where(lg2 == m2, lane, 128.0), axis=-1, keepdims=True)
    e = jnp.exp(m2 - m1)
    g1 = 1.0 / (1.0 + e)
    g2 = e / (1.0 + e)
    sel1 = lane == i1
    sel2 = lane == i2
    onehot = jnp.where(sel1, 1.0, jnp.where(sel2, 1.0, 0.0))
    ri = lax.broadcasted_iota(jnp.int32, (tm, tm), 0)
    ci = lax.broadcasted_iota(jnp.int32, (tm, tm), 1)
    tri = jnp.where(ri > ci, 1.0, 0.0).astype(BF16)
    before = jnp.dot(tri, onehot.astype(BF16), preferred_element_type=F32) + run_ref[...]
    r1 = jnp.sum(jnp.where(sel1, before, 0.0), axis=-1, keepdims=True)
    r2 = jnp.sum(jnp.where(sel2, before, 0.0), axis=-1, keepdims=True)
    run_ref[...] = run_ref[...] + jnp.sum(onehot, axis=0, keepdims=True)
    cnt_ref[...] = run_ref[...]
    meta = jnp.where(lane == 0, i1,
           jnp.where(lane == 1, i2,
           jnp.where(lane == 2, g1,
           jnp.where(lane == 3, g2,
           jnp.where(lane == 4, r1,
           jnp.where(lane == 5, r2, 0.0))))))
    meta_ref[...] = meta[:, :META_W]


def _router(x, g, wr_pad, base, tm, row_blk0, prev=None):
    n = x.shape[0]
    in_specs = [pl.BlockSpec((tm, D_MODEL), lambda i: (i, 0)), _const_spec((1, D_MODEL)),
                _const_spec((D_MODEL, 128)), _const_spec((1, 128))]
    args = [x, g, wr_pad, base]
    aliases = {}
    if prev is not None:
        in_specs += [pl.BlockSpec(memory_space=pl.ANY), pl.BlockSpec(memory_space=pl.ANY)]
        args += list(prev)
        aliases = {4: 0, 5: 1}
    return pl.pallas_call(
        _router_kernel,
        grid=(n // tm,),
        in_specs=in_specs,
        out_specs=[pl.BlockSpec((tm, D_MODEL), lambda i: (row_blk0 + i, 0)),
                   pl.BlockSpec((tm, META_W), lambda i: (row_blk0 + i, 0)),
                   _const_spec((1, 128))],
        out_shape=[jax.ShapeDtypeStruct((N_TOK, D_MODEL), F32), jax.ShapeDtypeStruct((N_TOK, META_W), F32),
                   jax.ShapeDtypeStruct((1, 128), F32)],
        scratch_shapes=[pltpu.VMEM((1, 128), F32)],
        input_output_aliases=aliases,
        compiler_params=_params(1),
        name="moe_router",
    )(*args)


DISPATCH_CHUNK = N_TOK // 16


def _dispatch_kernel(pos_ref, hn_ref, xs_ref, sem):
    base = pl.program_id(0) * DISPATCH_CHUNK

    def row_copy(t, k):
        return pltpu.make_async_copy(hn_ref.at[pl.ds(t, 1)], xs_ref.at[pl.ds(pos_ref[2 * t + k], 1)], sem)

    def start(j, c):
        row_copy(base + j, 0).start()
        row_copy(base + j, 1).start()
        return c

    def wait(j, c):
        row_copy(base + j, 0).wait()
        row_copy(base + j, 1).wait()
        return c

    lax.fori_loop(0, DISPATCH_CHUNK, start, 0)
    lax.fori_loop(0, DISPATCH_CHUNK, wait, 0)


def _dispatch(pos_flat, hn):
    return pl.pallas_call(
        _dispatch_kernel,
        grid_spec=pltpu.PrefetchScalarGridSpec(
            num_scalar_prefetch=1,
            grid=(N_TOK // DISPATCH_CHUNK,),
            in_specs=[pl.BlockSpec(memory_space=pl.ANY)],
            out_specs=pl.BlockSpec(memory_space=pl.ANY),
            scratch_shapes=[pltpu.SemaphoreType.DMA(())],
        ),
        out_shape=jax.ShapeDtypeStruct((MOE_MAX_TILES * MOE_TILE, D_MODEL), F32),
        compiler_params=_params(1),
        name="moe_dispatch",
    )(pos_flat, hn)


def _experts_kernel(te_ref, tb_ref, nv_ref, x_ref, wg_ref, wu_ref, wd_ref, o_ref, xb_ref, acc_ref):
    t = pl.program_id(0)
    f = pl.program_id(1)
    nv = nv_ref[t]

    @pl.when(jnp.logical_and(nv > 0, f == 0))
    def _():
        rows = lax.broadcasted_iota(jnp.int32, (MOE_TILE, 1), 0)
        xb_ref[...] = jnp.where(rows < nv, x_ref[...], 0.0).astype(BF16)

    @pl.when(nv > 0)
    def _():
        xb = xb_ref[...]
        gate = jnp.dot(xb, wg_ref[...].astype(BF16), preferred_element_type=F32)
        up = jnp.dot(xb, wu_ref[...].astype(BF16), preferred_element_type=F32)
        act = (gate * jax.nn.sigmoid(gate) * up).astype(BF16)
        part = jnp.dot(act, wd_ref[...].astype(BF16), preferred_element_type=F32)

        @pl.when(f == 0)
        def _():
            acc_ref[...] = part

        @pl.when(f > 0)
        def _():
            acc_ref[...] = acc_ref[...] + part

    @pl.when(jnp.logical_and(nv > 0, f == MOE_NF - 1))
    def _():
        o_ref[...] = acc_ref[...]


def _experts(tile_expert, tile_blk, tile_rows, xs, wg, wu, wd):
    def chunk(t, f, nv):
        return jnp.where(nv[t] > 0, f, MOE_NF - 1)
    return pl.pallas_call(
        _experts_kernel,
        grid_spec=pltpu.PrefetchScalarGridSpec(
            num_scalar_prefetch=3,
            grid=(MOE_MAX_TILES, MOE_NF),
            in_specs=[pl.BlockSpec((MOE_TILE, D_MODEL), lambda t, f, te, tb, nv: (tb[t], 0)),
                      pl.BlockSpec((None, D_MODEL, MOE_FF_TILE), lambda t, f, te, tb, nv: (te[t], 0, chunk(t, f, nv))),
                      pl.BlockSpec((None, D_MODEL, MOE_FF_TILE), lambda t, f, te, tb, nv: (te[t], 0, chunk(t, f, nv))),
                      pl.BlockSpec((None, MOE_FF_TILE, D_MODEL), lambda t, f, te, tb, nv: (te[t], chunk(t, f, nv), 0))],
            out_specs=pl.BlockSpec((MOE_TILE, D_MODEL), lambda t, f, te, tb, nv: (tb[t], 0)),
            scratch_shapes=[pltpu.VMEM((MOE_TILE, D_MODEL), BF16), pltpu.VMEM((MOE_TILE, D_MODEL), F32)],
        ),
        out_shape=jax.ShapeDtypeStruct((MOE_MAX_TILES * MOE_TILE, D_MODEL), F32),
        compiler_params=_params(2),
        name="moe_experts",
    )(tile_expert, tile_blk, tile_rows, xs, wg, wu, wd)


def _combine_kernel(pos_ref, x_ref, meta_ref, eo_ref, y_ref, buf_ref, sem, *, tok0, tm):
    base = tok0 + pl.program_id(0) * tm

    def row_copy(j, k):
        return pltpu.make_async_copy(eo_ref.at[pl.ds(pos_ref[2 * (base + j) + k], 1)],
                                     buf_ref.at[k, pl.ds(j, 1)], sem)

    def start(j, c):
        row_copy(j, 0).start()
        row_copy(j, 1).start()
        return c

    def wait(j, c):
        row_copy(j, 0).wait()
        row_copy(j, 1).wait()
        return c

    lax.fori_loop(0, tm, start, 0)
    lax.fori_loop(0, tm, wait, 0)
    meta = meta_ref[...]
    y_ref[...] = x_ref[...] + (meta[:, 2:3] * buf_ref[0] + meta[:, 3:4] * buf_ref[1])


def _combine(pos_flat, x, meta, eo, tok0, tm):
    n = x.shape[0]
    blk0 = tok0 // tm
    return pl.pallas_call(
        functools.partial(_combine_kernel, tok0=tok0, tm=tm),
        grid_spec=pltpu.PrefetchScalarGridSpec(
            num_scalar_prefetch=1,
            grid=(n // tm,),
            in_specs=[pl.BlockSpec((tm, D_MODEL), lambda i, pos: (i, 0)),
                      pl.BlockSpec((tm, META_W), lambda i, pos: (blk0 + i, 0)),
                      pl.BlockSpec(memory_space=pl.ANY)],
            out_specs=pl.BlockSpec((tm, D_MODEL), lambda i, pos: (i, 0)),
            scratch_shapes=[pltpu.VMEM((2, tm, D_MODEL), F32), pltpu.SemaphoreType.DMA(())],
        ),
        out_shape=jax.ShapeDtypeStruct((n, D_MODEL), F32),
        compiler_params=_params(1),
        name="moe_combine",
    )(pos_flat, x, meta, eo)


def _moe(xp, xs, g, w_router, wg, wu, wd):
    g2 = g.reshape(1, D_MODEL)
    wr_pad = jnp.pad(w_router, ((0, 0), (0, 128 - N_EXPERTS)))
    hn, meta, cnt = _router(xp, g2, wr_pad, jnp.zeros((1, 128), F32), ROW_TILE, 0)
    hn, meta, cnt = _router(xs, g2, wr_pad, cnt, DEC_BATCH, N_PROMPT // DEC_BATCH, prev=(hn, meta))

    counts = cnt[0, :N_EXPERTS].astype(jnp.int32)
    tiles_e = (counts + (MOE_TILE - 1)) // MOE_TILE
    tile_end = jnp.cumsum(tiles_e)
    tile_start = tile_end - tiles_e
    n_tiles = tile_end[-1]
    idx = meta[:, 0:2].astype(jnp.int32)
    rank = meta[:, 4:6].astype(jnp.int32)
    pos_flat = ((tile_start * MOE_TILE)[idx] + rank).reshape(-1)
    pos_flat = jnp.clip(pos_flat, 0, MOE_MAX_TILES * MOE_TILE - 1)
    t = jnp.arange(MOE_MAX_TILES, dtype=jnp.int32)
    tc = jnp.minimum(t, n_tiles - 1)
    tile_expert = jnp.sum((tc[:, None] >= tile_end[None, :]).astype(jnp.int32), axis=1)
    rows_left = counts[tile_expert] - (tc - tile_start[tile_expert]) * MOE_TILE
    tile_rows = jnp.where(t < n_tiles, jnp.minimum(rows_left, MOE_TILE), 0).astype(jnp.int32)

    slots = _dispatch(pos_flat, hn)
    eo = _experts(tile_expert.astype(jnp.int32), tc, tile_rows, slots, wg, wu, wd)
    yp = _combine(pos_flat, xp, meta, eo, 0, COMBINE_TILE)
    ys = _combine(pos_flat, xs, meta, eo, N_PROMPT, DEC_BATCH)
    return yp, ys


def kernel(x_prompt, x_sample, state_conv, cache_k, cache_v, norm_mix_g, norm_ffn_g, w_in_ab, conv_w, conv_b,
           conv_ln_g, conv_ln_b, q_norm_g, k_norm_g, attn_sinks, w_out_ab, w_gate_dense, w_up_dense, w_down_dense,
           w_in_c, c_ln_g, c_ln_b, w_spatial, b_spatial, w_out_c, w_router, w_gate_exp, w_up_exp, w_down_exp):
    xp = x_prompt.reshape(N_PROMPT, D_MODEL)
    xs = x_sample.reshape(DEC_BATCH, D_MODEL)
    head = jnp.arange(B_WIDTH) // HEAD_DIM
    seg = (head[:, None] == head[None, :]).astype(BF16)
    conv_p, conv_s, kp_out, vp_out, ks_out, vs_out, vc_out = [], [], [], [], [], [], []
    row = lambda a: a.reshape(1, -1)
    for layer in range(DEPTH):
        i = layer // 2
        g_mix = row(norm_mix_g[layer])
        g_ffn = row(norm_ffn_g[layer])
        if layer % 2 == 0:
            w_in = w_in_ab[i].astype(BF16)
            qg = row(jnp.tile(q_norm_g[i], N_HEADS))
            kg = row(jnp.tile(k_norm_g[i], N_KV_HEADS))
            a_p, q_p, k_p, v_p = _inproj(xp, g_mix, w_in, qg, kg, seg, ROW_TILE)
            a_s, q_s, k_s, v_s = _inproj(xs, g_mix, w_in, qg, kg, seg, DEC_BATCH)
            cb, lg, lb = row(conv_b[i]), row(conv_ln_g[i]), row(conv_ln_b[i])
            ca_p = _conv_prompt(a_p, conv_w[i], cb, lg, lb)
            ca_s = _conv_sample(a_s, jnp.transpose(state_conv[i], (1, 0, 2)), conv_w[i], cb, lg, lb)
            ob_p = _attn_prompt(q_p, k_p, v_p, attn_sinks[i])
            sinks_b = jnp.broadcast_to(attn_sinks[i].reshape(N_KV_HEADS, Q_GROUP, 1), (N_KV_HEADS, Q_GROUP, 128))
            ob_s = _attn_sample(q_s.reshape(DEC_BATCH, N_KV_HEADS, Q_GROUP, HEAD_DIM),
                                k_s.reshape(DEC_BATCH, N_KV_HEADS, HEAD_DIM),
                                v_s.reshape(DEC_BATCH, N_KV_HEADS, HEAD_DIM),
                                cache_k[i].reshape(DEC_BATCH, WINDOW, KV_WIDTH),
                                cache_v[i].reshape(DEC_BATCH, WINDOW, KV_WIDTH), sinks_b)
            ob_s = ob_s.reshape(DEC_BATCH, B_WIDTH)
            conv_p.append(a_p.reshape(BATCH, SEQ, A_CH)[:, SEQ - (CONV_W - 1):])
            conv_s.append(jnp.concatenate([state_conv[i][:, 1:], a_s[:, None, :]], axis=1))
            kp_out.append(k_p.reshape(BATCH, SEQ, N_KV_HEADS, HEAD_DIM)[:, SEQ - WINDOW:])
            vp_out.append(v_p.reshape(BATCH, SEQ, N_KV_HEADS, HEAD_DIM)[:, SEQ - WINDOW:])
            ks_out.append(jnp.concatenate(
                [cache_k[i][:, 1:], k_s.reshape(DEC_BATCH, 1, N_KV_HEADS, HEAD_DIM)], axis=1))
            vs_out.append(jnp.concatenate(
                [cache_v[i][:, 1:], v_s.reshape(DEC_BATCH, 1, N_KV_HEADS, HEAD_DIM)], axis=1))
            wo = w_out_ab[i].astype(BF16)
            wg = w_gate_dense[i].astype(BF16)
            wu = w_up_dense[i].astype(BF16)
            wd = w_down_dense[i].astype(BF16)
            xp = _outproj_ffn(xp, ca_p, ob_p, wo, g_ffn, wg, wu, wd, ROW_TILE)
            xs = _outproj_ffn(xs, ca_s, ob_s, wo, g_ffn, wg, wu, wd, DEC_BATCH)
        else:
            win = w_in_c[i].astype(BF16)
            wout = w_out_c[i].astype(BF16)
            lg, lb = row(c_ln_g[i]), row(c_ln_b[i])
            xp = _gmlp_prompt(xp, g_mix, win, lg, lb, w_spatial[i], b_spatial[i].T, wout, ROW_TILE)
            gd = C_WIDTH // C_GROUPS
            w0 = row(jnp.repeat(w_spatial[i][:, 0, 0], gd))
            b0 = row(jnp.repeat(b_spatial[i][:, 0], gd))
            xs, v_s = _gmlp_sample(xs, g_mix, win, lg, lb, w0, b0, wout)
            vc_out.append(v_s.reshape(DEC_BATCH, 1, C_WIDTH))
            xp, xs = _moe(xp, xs, norm_ffn_g[layer], w_router[i], w_gate_exp[i], w_up_exp[i], w_down_exp[i])
    return (xp.reshape(BATCH, SEQ, D_MODEL), xs.reshape(DEC_BATCH, 1, D_MODEL),
            jnp.stack(conv_p), jnp.stack(conv_s), jnp.stack(kp_out), jnp.stack(vp_out),
            jnp.stack(ks_out), jnp.stack(vs_out), jnp.stack(vc_out))
```

```python
import functools

import jax
import jax.numpy as jnp
from jax import lax
from jax.experimental import pallas as pl
from jax.experimental.pallas import tpu as pltpu

F32 = jnp.float32
BF16 = jnp.bfloat16

D_MODEL = 1024
BATCH = 8
SEQ = 2048
DEPTH = 4
DEC_BATCH = 128
A_CH = 512
CONV_W = 31
N_HEADS = 8
N_KV_HEADS = 2
HEAD_DIM = 64
Q_GROUP = 4
WINDOW = 128
B_WIDTH = 512
KV_WIDTH = 128
AB_IN = 2 * A_CH + B_WIDTH + 2 * KV_WIDTH
C_WIDTH = 1024
C_GROUPS = 8
C_CHUNK = 128
D_FF = 2816
N_EXPERTS = 8
D_FF_EXPERT = 3584
EPS = 1e-6
ATTN_SCALE = HEAD_DIM ** -0.5

N_PROMPT = BATCH * SEQ
N_TOK = N_PROMPT + DEC_BATCH
ROW_TILE = 1024
FF_SPLIT = 2
MOE_TILE = 1024
MOE_FF_TILE = 512
MOE_NF = D_FF_EXPERT // MOE_FF_TILE
MOE_BLOCK = 256
MOE_N_BLOCKS = N_PROMPT // MOE_BLOCK + 1
SEG_ALIGN = 8
MOE_MAX_TILES = (2 * N_TOK + MOE_N_BLOCKS * N_EXPERTS * (SEG_ALIGN - 1) + N_EXPERTS * (MOE_TILE - 1)) // MOE_TILE
N_SLOTS = MOE_MAX_TILES * MOE_TILE
VMEM_LIMIT = 56 * 1024 * 1024


def _params(n_axes, vmem=VMEM_LIMIT):
    return pltpu.CompilerParams(dimension_semantics=("arbitrary",) * n_axes, vmem_limit_bytes=vmem)


def _rms(x, g):
    return x * lax.rsqrt(jnp.mean(x * x, axis=-1, keepdims=True) + EPS) * g


def _layer_norm(x, g, b):
    mu = jnp.mean(x, axis=-1, keepdims=True)
    xc = x - mu
    var = jnp.mean(xc * xc, axis=-1, keepdims=True)
    return xc * lax.rsqrt(var + EPS) * g + b


def _const_spec(shape):
    return pl.BlockSpec(shape, lambda *_: (0,) * len(shape))


def _inproj_kernel(x_ref, g_ref, w_ref, qg_ref, kg_ref, seg_ref, a_ref, q_ref, k_ref, v_ref):
    h = _rms(x_ref[...], g_ref[...]).astype(BF16)
    z = jnp.dot(h, w_ref[...], preferred_element_type=F32)
    a_ref[...] = z[:, :A_CH] * jax.nn.sigmoid(z[:, A_CH:2 * A_CH])
    seg = seg_ref[...]

    def head_norm(t, gain, seg_t):
        sq = t * t
        hi = sq.astype(BF16)
        lo = (sq - hi.astype(F32)).astype(BF16)
        ms = (jnp.dot(hi, seg_t, preferred_element_type=F32)
              + jnp.dot(lo, seg_t, preferred_element_type=F32)) * (1.0 / HEAD_DIM)
        return t * lax.rsqrt(ms + EPS) * gain

    o = 2 * A_CH
    q = head_norm(z[:, o:o + B_WIDTH], qg_ref[...], seg)
    q_ref[...] = (q * ATTN_SCALE).astype(BF16)
    o += B_WIDTH
    k_ref[...] = head_norm(z[:, o:o + KV_WIDTH], kg_ref[...], seg[:KV_WIDTH, :KV_WIDTH])
    o += KV_WIDTH
    v_ref[...] = z[:, o:o + KV_WIDTH]


def _inproj(x, g, w_bf, qg, kg, seg, tm):
    n = x.shape[0]
    row = lambda w: pl.BlockSpec((tm, w), lambda i: (i, 0))
    return pl.pallas_call(
        _inproj_kernel,
        grid=(n // tm,),
        in_specs=[row(D_MODEL), _const_spec((1, D_MODEL)), _const_spec((D_MODEL, AB_IN)),
                  _const_spec((1, B_WIDTH)), _const_spec((1, KV_WIDTH)), _const_spec((B_WIDTH, B_WIDTH))],
        out_specs=[row(A_CH), row(B_WIDTH), row(KV_WIDTH), row(KV_WIDTH)],
        out_shape=[jax.ShapeDtypeStruct((n, A_CH), F32), jax.ShapeDtypeStruct((n, B_WIDTH), BF16),
                   jax.ShapeDtypeStruct((n, KV_WIDTH), F32), jax.ShapeDtypeStruct((n, KV_WIDTH), F32)],
        compiler_params=_params(1),
        name="inproj_ab",
    )(x, g, w_bf, qg, kg, seg)


CONV_PAD = 32
CONV_ROWS = 128


def _conv_tail(c, lg, lb):
    y = _layer_norm(c, lg, lb)
    return y * jax.nn.sigmoid(y)


def _conv_prompt_kernel(a_ref, w_ref, b_ref, lg_ref, lb_ref, o_ref, pad_ref, win_ref, c_ref):
    pad_ref[0:CONV_PAD, :] = jnp.zeros((CONV_PAD, A_CH), F32)
    pad_ref[CONV_PAD:, :] = a_ref[...]

    def body(r, carry):
        t0 = pl.multiple_of(r * CONV_ROWS, CONV_ROWS)
        win_ref[...] = pad_ref[pl.ds(t0, CONV_ROWS + CONV_PAD), :]
        for c in range(A_CH // 128):
            lanes = slice(c * 128, (c + 1) * 128)
            acc = jnp.zeros((CONV_ROWS, 128), F32)
            for k in range(CONV_W):
                start = CONV_PAD - (CONV_W - 1) + k
                acc = acc + win_ref[start:start + CONV_ROWS, lanes] * w_ref[k:k + 1, lanes]
            c_ref[:, lanes] = acc
        y = _conv_tail(c_ref[...] + b_ref[...], lg_ref[...], lb_ref[...])
        o_ref[pl.ds(t0, CONV_ROWS), :] = y.astype(BF16)
        return carry

    lax.fori_loop(0, SEQ // CONV_ROWS, body, 0)


def _conv_prompt(a, w, b, lg, lb):
    return pl.pallas_call(
        _conv_prompt_kernel,
        grid=(BATCH,),
        in_specs=[pl.BlockSpec((SEQ, A_CH), lambda i: (i, 0)), _const_spec((CONV_W, A_CH)),
                  _const_spec((1, A_CH)), _const_spec((1, A_CH)), _const_spec((1, A_CH))],
        out_specs=pl.BlockSpec((SEQ, A_CH), lambda i: (i, 0)),
        out_shape=jax.ShapeDtypeStruct((N_PROMPT, A_CH), BF16),
        scratch_shapes=[pltpu.VMEM((CONV_PAD + SEQ, A_CH), F32), pltpu.VMEM((CONV_ROWS + CONV_PAD, A_CH), F32),
                        pltpu.VMEM((CONV_ROWS, A_CH), F32)],
        compiler_params=_params(1),
        name="conv_prompt",
    )(a, w, b, lg, lb)


def _conv_sample_kernel(a_ref, st_ref, w_ref, b_ref, lg_ref, lb_ref, o_ref):
    acc = a_ref[...] * w_ref[CONV_W - 1:CONV_W, :]
    for k in range(CONV_W - 1):
        acc = acc + st_ref[k] * w_ref[k:k + 1, :]
    o_ref[...] = _conv_tail(acc + b_ref[...], lg_ref[...], lb_ref[...]).astype(BF16)


def _conv_sample(a, st_t, w, b, lg, lb):
    return pl.pallas_call(
        _conv_sample_kernel,
        grid=(1,),
        in_specs=[_const_spec((DEC_BATCH, A_CH)), _const_spec((CONV_W - 1, DEC_BATCH, A_CH)),
                  _const_spec((CONV_W, A_CH)), _const_spec((1, A_CH)), _const_spec((1, A_CH)),
                  _const_spec((1, A_CH))],
        out_specs=_const_spec((DEC_BATCH, A_CH)),
        out_shape=jax.ShapeDtypeStruct((DEC_BATCH, A_CH), BF16),
        compiler_params=_params(1),
        name="conv_sample",
    )(a, st_t, w, b, lg, lb)


def _alibi_slope(h):
    return 2.0 ** (-8.0 * (h + 1) / N_HEADS)


def _attn_prompt_kernel(sink_ref, q_ref, kp_ref, kc_ref, vp_ref, vc_ref, o_ref):
    blk = pl.program_id(1)
    kk = jnp.concatenate([kp_ref[...], kc_ref[...]], axis=0).astype(BF16)
    vv = jnp.concatenate([vp_ref[...], vc_ref[...]], axis=0).astype(BF16)
    qi = lax.broadcasted_iota(jnp.int32, (WINDOW, 2 * WINDOW), 0)
    kj = lax.broadcasted_iota(jnp.int32, (WINDOW, 2 * WINDOW), 1)
    dist_i = WINDOW + qi - kj
    lo = jnp.where(blk > 0, 0, WINDOW)
    valid = (dist_i >= 0) & (dist_i < WINDOW) & (kj >= lo)
    dist = dist_i.astype(F32)
    outs = []
    for h in range(N_HEADS):
        kv = h // Q_GROUP
        qh = q_ref[:, h * HEAD_DIM:(h + 1) * HEAD_DIM]
        kh = kk[:, kv * HEAD_DIM:(kv + 1) * HEAD_DIM]
        s = lax.dot_general(qh, kh, (((1,), (1,)), ((), ())), preferred_element_type=F32)
        s = jnp.where(valid, s - _alibi_slope(h) * dist, -jnp.inf)
        sink = sink_ref[h]
        m = jnp.maximum(jnp.max(s, axis=-1, keepdims=True), sink)
        p = jnp.exp(s - m)
        denom = jnp.sum(p, axis=-1, keepdims=True) + jnp.exp(sink - m)
        p = (p / denom).astype(BF16)
        outs.append(jnp.dot(p, vv[:, kv * HEAD_DIM:(kv + 1) * HEAD_DIM], preferred_element_type=F32))
    o_ref[...] = jnp.concatenate(outs, axis=-1).astype(BF16)


def _attn_prompt(q, k, v, sinks):
    nb = SEQ // WINDOW
    own = lambda b, j: (b * nb + j, 0)
    prev = lambda b, j: (b * nb + jnp.maximum(j - 1, 0), 0)
    return pl.pallas_call(
        _attn_prompt_kernel,
        grid=(BATCH, nb),
        in_specs=[pl.BlockSpec(memory_space=pltpu.SMEM),
                  pl.BlockSpec((WINDOW, B_WIDTH), own),
                  pl.BlockSpec((WINDOW, KV_WIDTH), prev), pl.BlockSpec((WINDOW, KV_WIDTH), own),
                  pl.BlockSpec((WINDOW, KV_WIDTH), prev), pl.BlockSpec((WINDOW, KV_WIDTH), own)],
        out_specs=pl.BlockSpec((WINDOW, B_WIDTH), own),
        out_shape=jax.ShapeDtypeStruct((N_PROMPT, B_WIDTH), BF16),
        compiler_params=_params(2),
        name="attn_prompt",
    )(sinks, q, k, k, v, v)


ATTN_S_BLK = 32


def _attn_sample_kernel(sink_ref, q_ref, kn_ref, vn_ref, kc_ref, vc_ref, o_ref):
    kc = kc_ref[...].astype(BF16)
    vc = vc_ref[...].astype(BF16)
    j = lax.broadcasted_iota(jnp.int32, (1, 1, WINDOW), 2)
    dist = (WINDOW - j).astype(F32)
    g_idx = lax.broadcasted_iota(jnp.int32, (1, Q_GROUP, 1), 1)
    for kv in range(N_KV_HEADS):
        lanes = slice(kv * HEAD_DIM, (kv + 1) * HEAD_DIM)
        qg = q_ref[:, kv]
        slope = jnp.exp2(-(g_idx + (kv * Q_GROUP + 1)).astype(F32) * (8.0 / N_HEADS))
        s = jnp.einsum('ngd,njd->ngj', qg, kc[:, :, lanes], preferred_element_type=F32)
        s = jnp.where(j >= 1, s - slope * dist, -jnp.inf)
        kn = kn_ref[:, kv].astype(BF16).astype(F32)
        s_new = jnp.sum(qg.astype(F32) * kn[:, None, :], axis=-1, keepdims=True)
        sink = sink_ref[kv][:, :1][None]
        m = jnp.maximum(jnp.maximum(jnp.max(s, axis=-1, keepdims=True), s_new), sink)
        p = jnp.exp(s - m)
        p_new = jnp.exp(s_new - m)
        denom = jnp.sum(p, axis=-1, keepdims=True) + p_new + jnp.exp(sink - m)
        o = jnp.einsum('ngj,njd->ngd', (p / denom).astype(BF16), vc[:, :, lanes], preferred_element_type=F32)
        vn = vn_ref[:, kv].astype(BF16).astype(F32)
        o = o + (p_new / denom).astype(BF16).astype(F32) * vn[:, None, :]
        o_ref[:, kv] = o.astype(BF16)


def _attn_sample(q4, kn3, vn3, kc, vc, sinks_b):
    nb = ATTN_S_BLK
    return pl.pallas_call(
        _attn_sample_kernel,
        grid=(DEC_BATCH // nb,),
        in_specs=[_const_spec((N_KV_HEADS, Q_GROUP, 128)),
                  pl.BlockSpec((nb, N_KV_HEADS, Q_GROUP, HEAD_DIM), lambda i: (i, 0, 0, 0)),
                  pl.BlockSpec((nb, N_KV_HEADS, HEAD_DIM), lambda i: (i, 0, 0)),
                  pl.BlockSpec((nb, N_KV_HEADS, HEAD_DIM), lambda i: (i, 0, 0)),
                  pl.BlockSpec((nb, WINDOW, KV_WIDTH), lambda i: (i, 0, 0)),
                  pl.BlockSpec((nb, WINDOW, KV_WIDTH), lambda i: (i, 0, 0))],
        out_specs=pl.BlockSpec((nb, N_KV_HEADS, Q_GROUP, HEAD_DIM), lambda i: (i, 0, 0, 0)),
        out_shape=jax.ShapeDtypeStruct((DEC_BATCH, N_KV_HEADS, Q_GROUP, HEAD_DIM), BF16),
        compiler_params=_params(1),
        name="attn_sample",
    )(sinks_b, q4, kn3, vn3, kc, vc)


def _outproj_ffn_kernel(x_ref, ca_ref, ob_ref, g_ref, wo_hbm, wg_hbm, wu_hbm, wd_hbm, o_ref,
                        wo_ref, wg_ref, wu_ref, wd_ref):
    @pl.when(pl.program_id(0) == 0)
    def _():
        pltpu.sync_copy(wo_hbm, wo_ref)
        pltpu.sync_copy(wg_hbm, wg_ref)
        pltpu.sync_copy(wu_hbm, wu_ref)
        pltpu.sync_copy(wd_hbm, wd_ref)

    o_ref[...] = (x_ref[...]
                  + jnp.dot(ca_ref[...], wo_ref[:A_CH, :], preferred_element_type=F32)
                  + jnp.dot(ob_ref[...], wo_ref[A_CH:, :], preferred_element_type=F32))
    h = _rms(o_ref[...], g_ref[...]).astype(BF16)
    ff = D_FF // FF_SPLIT
    for c in range(FF_SPLIT):
        cols = slice(c * ff, (c + 1) * ff)
        gate = jnp.dot(h, wg_ref[:, cols], preferred_element_type=F32)
        up = jnp.dot(h, wu_ref[:, cols], preferred_element_type=F32)
        act = (gate * jax.nn.sigmoid(gate) * up).astype(BF16)
        o_ref[...] += jnp.dot(act, wd_ref[cols, :], preferred_element_type=F32)


def _outproj_ffn(x, ca, ob, wo, g, wg, wu, wd, tm):
    n = x.shape[0]
    row = lambda w: pl.BlockSpec((tm, w), lambda i: (i, 0))
    hbm = pl.BlockSpec(memory_space=pl.ANY)
    return pl.pallas_call(
        _outproj_ffn_kernel,
        grid=(n // tm,),
        in_specs=[row(D_MODEL), row(A_CH), row(B_WIDTH), _const_spec((1, D_MODEL)), hbm, hbm, hbm, hbm],
        out_specs=row(D_MODEL),
        out_shape=jax.ShapeDtypeStruct((n, D_MODEL), F32),
        scratch_shapes=[pltpu.VMEM((D_MODEL, D_MODEL), BF16), pltpu.VMEM((D_MODEL, D_FF), BF16),
                        pltpu.VMEM((D_MODEL, D_FF), BF16), pltpu.VMEM((D_FF, D_MODEL), BF16)],
        compiler_params=_params(1),
        name="outproj_ffn",
    )(x, ca, ob, g, wo, wg, wu, wd)


def _gmlp_prompt_kernel(x_ref, g_ref, win_ref, lg_ref, lb_ref, ws_ref, bs_ref, wout_ref, o_ref, gate_ref):
    x = x_ref[...]
    h = _rms(x, g_ref[...]).astype(BF16)
    z = jax.nn.gelu(jnp.dot(h, win_ref[...], preferred_element_type=F32))
    u = z[:, :C_WIDTH]
    v = _layer_norm(z[:, C_WIDTH:], lg_ref[...], lb_ref[...]).astype(BF16)
    ri = lax.broadcasted_iota(jnp.int32, (C_CHUNK, C_CHUNK), 0)
    ci = lax.broadcasted_iota(jnp.int32, (C_CHUNK, C_CHUNK), 1)
    tm = x.shape[0]
    gd = C_WIDTH // C_GROUPS
    for grp in range(C_GROUPS):
        w = jnp.where(ri >= ci, ws_ref[grp], 0.0).astype(BF16)
        b = bs_ref[:, grp:grp + 1]
        lanes = slice(grp * gd, (grp + 1) * gd)
        for c in range(tm // C_CHUNK):
            rows = slice(c * C_CHUNK, (c + 1) * C_CHUNK)
            mixed = jnp.dot(w, v[rows, lanes], preferred_element_type=F32) + b
            gate_ref[rows, lanes] = (u[rows, lanes] * mixed).astype(BF16)
    o_ref[...] = x + jnp.dot(gate_ref[...], wout_ref[...], preferred_element_type=F32)


def _gmlp_prompt(x, g, win, lg, lb, ws, bs_t, wout, tm):
    n = x.shape[0]
    row = lambda w: pl.BlockSpec((tm, w), lambda i: (i, 0))
    return pl.pallas_call(
        _gmlp_prompt_kernel,
        grid=(n // tm,),
        in_specs=[row(D_MODEL), _const_spec((1, D_MODEL)), _const_spec((D_MODEL, 2 * C_WIDTH)),
                  _const_spec((1, C_WIDTH)), _const_spec((1, C_WIDTH)),
                  _const_spec((C_GROUPS, C_CHUNK, C_CHUNK)), _const_spec((C_CHUNK, C_GROUPS)),
                  _const_spec((C_WIDTH, D_MODEL))],
        out_specs=row(D_MODEL),
        out_shape=jax.ShapeDtypeStruct((n, D_MODEL), F32),
        scratch_shapes=[pltpu.VMEM((tm, C_WIDTH), BF16)],
        compiler_params=_params(1),
        name="gmlp_prompt",
    )(x, g, win, lg, lb, ws, bs_t, wout)


def _gmlp_sample_kernel(x_ref, g_ref, win_ref, lg_ref, lb_ref, w0_ref, b0_ref, wout_ref, o_ref, v_ref):
    x = x_ref[...]
    h = _rms(x, g_ref[...]).astype(BF16)
    z = jax.nn.gelu(jnp.dot(h, win_ref[...], preferred_element_type=F32))
    v = _layer_norm(z[:, C_WIDTH:], lg_ref[...], lb_ref[...])
    v_ref[...] = v
    gated = z[:, :C_WIDTH] * (w0_ref[...] * v + b0_ref[...])
    o_ref[...] = x + jnp.dot(gated.astype(BF16), wout_ref[...], preferred_element_type=F32)


def _gmlp_sample(x, g, win, lg, lb, w0, b0, wout):
    n = x.shape[0]
    return pl.pallas_call(
        _gmlp_sample_kernel,
        grid=(1,),
        in_specs=[_const_spec((n, D_MODEL)), _const_spec((1, D_MODEL)), _const_spec((D_MODEL, 2 * C_WIDTH)),
                  _const_spec((1, C_WIDTH)), _const_spec((1, C_WIDTH)), _const_spec((1, C_WIDTH)),
                  _const_spec((1, C_WIDTH)), _const_spec((C_WIDTH, D_MODEL))],
        out_specs=[_const_spec((n, D_MODEL)), _const_spec((n, C_WIDTH))],
        out_shape=[jax.ShapeDtypeStruct((n, D_MODEL), F32), jax.ShapeDtypeStruct((n, C_WIDTH), F32)],
        compiler_params=_params(1),
        name="gmlp_sample",
    )(x, g, win, lg, lb, w0, b0, wout)


META_W = 8


def _router_kernel(x_ref, g_ref, wr_ref, base_ref, *rest):
    hn_ref, meta_ref, cnt_ref, run_ref = rest[-4:]
    step = pl.program_id(0)

    @pl.when(step == 0)
    def _():
        run_ref[...] = base_ref[...]

    h = _rms(x_ref[...], g_ref[...])
    hn_ref[...] = h
    tm = h.shape[0]
    logits = jnp.dot(h, wr_ref[...], preferred_element_type=F32, precision=lax.Precision.HIGHEST)
    lane = lax.broadcasted_iota(jnp.int32, (tm, 128), 1).astype(F32)
    neg = -jnp.inf
    lg = jnp.where(lane < N_EXPERTS, logits, neg)
    m1 = jnp.max(lg, axis=-1, keepdims=True)
    i1 = jnp.min(jnp.where(lg == m1, lane, 128.0), axis=-1, keepdims=True)
    lg2 = jnp.where(lane == i1, neg, lg)
    m2 = jnp.max(lg2, axis=-1, keepdims=True)
    i2 = jnp.min(jnp.where(lg2 == m2, lane, 128.0), axis=-1, keepdims=True)
    e = jnp.exp(m2 - m1)
    g1 = 1.0 / (1.0 + e)
    g2 = e / (1.0 + e)
    sel1 = lane == i1
    sel2 = lane == i2
    onehot = jnp.where(sel1, 1.0, jnp.where(sel2, 1.0, 0.0))
    ri = lax.broadcasted_iota(jnp.int32, (tm, tm), 0)
    ci = lax.broadcasted_iota(jnp.int32, (tm, tm), 1)
    tri = jnp.where(ri > ci, 1.0, 0.0).astype(BF16)
    before = jnp.dot(tri, onehot.astype(BF16), preferred_element_type=F32) + run_ref[...]
    r1 = jnp.sum(jnp.where(sel1, before, 0.0), axis=-1, keepdims=True)
    r2 = jnp.sum(jnp.where(sel2, before, 0.0), axis=-1, keepdims=True)
    run_ref[...] = run_ref[...] + jnp.sum(onehot, axis=0, keepdims=True)
    cnt_ref[...] = run_ref[...]
    meta = jnp.where(lane == 0, i1,
           jnp.where(lane == 1, i2,
           jnp.where(lane == 2, g1,
           jnp.where(lane == 3, g2,
           jnp.where(lane == 4, r1,
           jnp.where(lane == 5, r2, 0.0))))))
    meta_ref[...] = meta[:, :META_W]


def _router(x, g, wr_pad, base, tm, row_blk0, prev=None):
    n = x.shape[0]
    in_specs = [pl.BlockSpec((tm, D_MODEL), lambda i: (i, 0)), _const_spec((1, D_MODEL)),
                _const_spec((D_MODEL, 128)), _const_spec((1, 128))]
    args = [x, g, wr_pad, base]
    aliases = {}
    if prev is not None:
        in_specs += [pl.BlockSpec(memory_space=pl.ANY), pl.BlockSpec(memory_space=pl.ANY)]
        args += list(prev)
        aliases = {4: 0, 5: 1}
    return pl.pallas_call(
        _router_kernel,
        grid=(n // tm,),
        in_specs=in_specs,
        out_specs=[pl.BlockSpec((tm, D_MODEL), lambda i: (row_blk0 + i, 0)),
                   pl.BlockSpec((tm, META_W), lambda i: (row_blk0 + i, 0)),
                   _const_spec((1, 128))],
        out_shape=[jax.ShapeDtypeStruct((N_TOK, D_MODEL), F32), jax.ShapeDtypeStruct((N_TOK, META_W), F32),
                   jax.ShapeDtypeStruct((1, 128), F32)],
        scratch_shapes=[pltpu.VMEM((1, 128), F32)],
        input_output_aliases=aliases,
        compiler_params=_params(1),
        name="moe_router",
    )(*args)


def _local_rows(bt):
    return 2 * bt + N_EXPERTS * SEG_ALIGN


def _segment_copies(seg_n, seg_src, seg_dst, blk, local_ref, slot_ref, sem, bt, to_slots, action):
    def per_expert(e, c):
        s = blk * N_EXPERTS + e
        n, src, dst = seg_n[s], seg_src[s], seg_dst[s]
        off = jnp.int32(0)
        p = bt
        while p >= SEG_ALIGN:
            hit = (n & p) != 0

            @pl.when(hit)
            def _(off=off, p=p):
                a = local_ref.at[pl.ds(pl.multiple_of(src + off, SEG_ALIGN), p)]
                b = slot_ref.at[pl.ds(pl.multiple_of(dst + off, SEG_ALIGN), p)]
                action(pltpu.make_async_copy(a, b, sem) if to_slots else pltpu.make_async_copy(b, a, sem))

            off = off + (n & p)
            p //= 2
        return c

    lax.fori_loop(0, N_EXPERTS, per_expert, 0)


def _dispatch_kernel(seg_n, seg_src, seg_dst, hn_ref, ls_ref, *rest, bt, blk0, nb):
    slots_ref, buf_ref, sem = rest[-3:]
    i = pl.program_id(0)
    par = i % 2
    start = lambda cp: cp.start()
    wait = lambda cp: cp.wait()
    copies = lambda blk, p, action: _segment_copies(seg_n, seg_src, seg_dst, blk, buf_ref.at[p], slots_ref,
                                                     sem.at[p], bt, True, action)

    @pl.when(i >= 2)
    def _():
        copies(blk0 + i - 2, par, wait)

    rows = lax.broadcasted_iota(jnp.int32, (_local_rows(bt), bt), 0)
    ls = ls_ref[0]
    onehot = jnp.where(rows == ls[0:1, :], 1.0, jnp.where(rows == ls[1:2, :], 1.0, 0.0)).astype(BF16)
    buf_ref[par] = jnp.dot(onehot, hn_ref[...].astype(BF16), preferred_element_type=F32)
    copies(blk0 + i, par, start)

    @pl.when(i == nb - 1)
    def _():
        if nb >= 2:
            copies(blk0 + i - 1, 1 - par, wait)
        copies(blk0 + i, par, wait)


def _dispatch(seg, hn, ls_t, bt, blk0, tok0, prev=None):
    nb = ls_t.shape[0]
    in_specs = [pl.BlockSpec((bt, D_MODEL), lambda i, *_: (tok0 // bt + i, 0)),
                pl.BlockSpec((1, 2, bt), lambda i, *_: (i, 0, 0))]
    args = [hn, ls_t]
    aliases = {}
    if prev is not None:
        in_specs.append(pl.BlockSpec(memory_space=pl.ANY))
        args.append(prev)
        aliases = {5: 0}
    return pl.pallas_call(
        functools.partial(_dispatch_kernel, bt=bt, blk0=blk0, nb=nb),
        grid_spec=pltpu.PrefetchScalarGridSpec(
            num_scalar_prefetch=3,
            grid=(nb,),
            in_specs=in_specs,
            out_specs=pl.BlockSpec(memory_space=pl.ANY),
            scratch_shapes=[pltpu.VMEM((2, _local_rows(bt), D_MODEL), F32), pltpu.SemaphoreType.DMA((2,))],
        ),
        out_shape=jax.ShapeDtypeStruct((N_SLOTS, D_MODEL), F32),
        input_output_aliases=aliases,
        compiler_params=_params(1),
        name="moe_dispatch",
    )(*seg, *args)


def _experts_kernel(te_ref, tb_ref, nv_ref, x_ref, wg_ref, wu_ref, wd_ref, o_ref, xb_ref, acc_ref):
    t = pl.program_id(0)
    f = pl.program_id(1)
    nv = nv_ref[t]

    @pl.when(jnp.logical_and(nv > 0, f == 0))
    def _():
        rows = lax.broadcasted_iota(jnp.int32, (MOE_TILE, 1), 0)
        xb_ref[...] = jnp.where(rows < nv, x_ref[...], 0.0).astype(BF16)

    @pl.when(nv > 0)
    def _():
        xb = xb_ref[...]
        gate = jnp.dot(xb, wg_ref[...].astype(BF16), preferred_element_type=F32)
        up = jnp.dot(xb, wu_ref[...].astype(BF16), preferred_element_type=F32)
        act = (gate * jax.nn.sigmoid(gate) * up).astype(BF16)
        part = jnp.dot(act, wd_ref[...].astype(BF16), preferred_element_type=F32)

        @pl.when(f == 0)
        def _():
            acc_ref[...] = part

        @pl.when(f > 0)
        def _():
            acc_ref[...] = acc_ref[...] + part

    @pl.when(jnp.logical_and(nv > 0, f == MOE_NF - 1))
    def _():
        o_ref[...] = acc_ref[...]


def _experts(tile_expert, tile_blk, tile_rows, xs, wg, wu, wd, layer):
    def chunk(t, f, nv):
        return jnp.where(nv[t] > 0, f, MOE_NF - 1)
    up_spec = pl.BlockSpec((None, None, D_MODEL, MOE_FF_TILE),
                           lambda t, f, te, tb, nv: (layer, te[t], 0, chunk(t, f, nv)))
    down_spec = pl.BlockSpec((None, None, MOE_FF_TILE, D_MODEL),
                             lambda t, f, te, tb, nv: (layer, te[t], chunk(t, f, nv), 0))
    return pl.pallas_call(
        _experts_kernel,
        grid_spec=pltpu.PrefetchScalarGridSpec(
            num_scalar_prefetch=3,
            grid=(MOE_MAX_TILES, MOE_NF),
            in_specs=[pl.BlockSpec((MOE_TILE, D_MODEL), lambda t, f, te, tb, nv: (tb[t], 0)),
                      up_spec, up_spec, down_spec],
            out_specs=pl.BlockSpec((MOE_TILE, D_MODEL), lambda t, f, te, tb, nv: (tb[t], 0)),
            scratch_shapes=[pltpu.VMEM((MOE_TILE, D_MODEL), BF16), pltpu.VMEM((MOE_TILE, D_MODEL), F32)],
        ),
        out_shape=jax.ShapeDtypeStruct((N_SLOTS, D_MODEL), F32),
        compiler_params=_params(2),
        name="moe_experts",
    )(tile_expert, tile_blk, tile_rows, xs, wg, wu, wd)


def _combine_kernel(seg_n, seg_src, seg_dst, x_ref, cm_ref, eo_ref, y_ref, buf_ref, sem, *, bt, blk0, nb):
    i = pl.program_id(0)
    par = i % 2
    start = lambda cp: cp.start()
    wait = lambda cp: cp.wait()
    copies = lambda blk, p, action: _segment_copies(seg_n, seg_src, seg_dst, blk, buf_ref.at[p], eo_ref,
                                                     sem.at[p], bt, False, action)

    @pl.when(i == 0)
    def _():
        copies(blk0, 0, start)

    @pl.when(i + 1 < nb)
    def _():
        copies(blk0 + i + 1, 1 - par, start)

    copies(blk0 + i, par, wait)
    last = (blk0 + i) * N_EXPERTS + (N_EXPERTS - 1)
    used = seg_src[last] + seg_n[last]
    rows = lax.broadcasted_iota(jnp.int32, (_local_rows(bt), 1), 0)
    eo = jnp.where(rows < used, buf_ref[par], 0.0)
    hi = eo.astype(BF16)
    lo = (eo - hi.astype(F32)).astype(BF16)
    cm = cm_ref[...]
    cols = lax.broadcasted_iota(jnp.int32, (bt, _local_rows(bt)), 1).astype(F32)

    def pick(k):
        sel = jnp.where(cols == cm[:, k:k + 1], 1.0, 0.0).astype(BF16)
        return jnp.dot(sel, hi, preferred_element_type=F32) + jnp.dot(sel, lo, preferred_element_type=F32)

    y_ref[...] = x_ref[...] + (cm[:, 2:3] * pick(0) + cm[:, 3:4] * pick(1))


def _combine(seg, x, cm, eo, bt, blk0, tok0):
    n = x.shape[0]
    nb = n // bt
    return pl.pallas_call(
        functools.partial(_combine_kernel, bt=bt, blk0=blk0, nb=nb),
        grid_spec=pltpu.PrefetchScalarGridSpec(
            num_scalar_prefetch=3,
            grid=(nb,),
            in_specs=[pl.BlockSpec((bt, D_MODEL), lambda i, *_: (i, 0)),
                      pl.BlockSpec((bt, META_W), lambda i, *_: (tok0 // bt + i, 0)),
                      pl.BlockSpec(memory_space=pl.ANY)],
            out_specs=pl.BlockSpec((bt, D_MODEL), lambda i, *_: (i, 0)),
            scratch_shapes=[pltpu.VMEM((2, _local_rows(bt), D_MODEL), F32), pltpu.SemaphoreType.DMA((2,))],
        ),
        out_shape=jax.ShapeDtypeStruct((n, D_MODEL), F32),
        compiler_params=_params(1),
        name="moe_combine",
    )(*seg, x, cm, eo)


def _moe(xp, xs, g, w_router, wg, wu, wd, layer):
    g2 = g.reshape(1, D_MODEL)
    wr_pad = jnp.pad(w_router, ((0, 0), (0, 128 - N_EXPERTS)))
    hn, meta, cnt = _router(xp, g2, wr_pad, jnp.zeros((1, 128), F32), ROW_TILE, 0)
    hn, meta, cnt = _router(xs, g2, wr_pad, cnt, DEC_BATCH, N_PROMPT // DEC_BATCH, prev=(hn, meta))

    idx = meta[:, 0:2].astype(jnp.int32)
    rank = meta[:, 4:6].astype(jnp.int32)
    onehot = (idx[:, :, None] == jnp.arange(N_EXPERTS, dtype=jnp.int32)).astype(jnp.int32)
    per_tok = onehot.sum(axis=1)
    nbp = N_PROMPT // MOE_BLOCK
    seg_cnt = jnp.concatenate([per_tok[:N_PROMPT].reshape(nbp, MOE_BLOCK, N_EXPERTS).sum(axis=1),
                               per_tok[N_PROMPT:].sum(axis=0, keepdims=True)], axis=0)
    seg_n = (seg_cnt + (SEG_ALIGN - 1)) // SEG_ALIGN * SEG_ALIGN
    seg_src = jnp.cumsum(seg_n, axis=1) - seg_n
    rank0 = jnp.cumsum(seg_cnt, axis=0) - seg_cnt
    slot0 = jnp.cumsum(seg_n, axis=0) - seg_n

    counts = seg_n.sum(axis=0)
    tiles_e = (counts + (MOE_TILE - 1)) // MOE_TILE
    tile_end = jnp.cumsum(tiles_e)
    tile_start = tile_end - tiles_e
    n_tiles = tile_end[-1]
    t = jnp.arange(MOE_MAX_TILES, dtype=jnp.int32)
    tc = jnp.minimum(t, n_tiles - 1)
    tile_expert = jnp.sum((tc[:, None] >= tile_end[None, :]).astype(jnp.int32), axis=1)
    rows_left = counts[tile_expert] - (tc - tile_start[tile_expert]) * MOE_TILE
    tile_rows = jnp.where(t < n_tiles, jnp.minimum(rows_left, MOE_TILE), 0).astype(jnp.int32)
    seg_dst = (tile_start * MOE_TILE)[None, :] + slot0

    blk = jnp.concatenate([jnp.arange(N_PROMPT, dtype=jnp.int32) // MOE_BLOCK, jnp.full((DEC_BATCH,), nbp, jnp.int32)])
    local = ((seg_src - rank0)[blk][:, None, :] * onehot).sum(axis=-1) + rank
    cap = jnp.concatenate([jnp.full((nbp, 1), MOE_BLOCK, jnp.int32), jnp.full((1, 1), DEC_BATCH, jnp.int32)])
    local = jnp.clip(local, -1, _local_rows(MOE_BLOCK) - 1)
    seg_n = jnp.clip(seg_n, 0, cap)
    seg = (seg_n.reshape(-1), jnp.clip(seg_src, 0, 2 * cap + N_EXPERTS * SEG_ALIGN - seg_n).reshape(-1),
           jnp.clip(seg_dst, 0, N_SLOTS - MOE_BLOCK).reshape(-1))
    ls_p = local[:N_PROMPT].reshape(nbp, MOE_BLOCK, 2).transpose(0, 2, 1)
    ls_s = local[N_PROMPT:].T[None]
    cm = jnp.concatenate([local.astype(F32), meta[:, 2:4], jnp.zeros((N_TOK, META_W - 4), F32)], axis=1)

    slots = _dispatch(seg, hn, ls_p, MOE_BLOCK, 0, 0)
    slots = _dispatch(seg, hn, ls_s, DEC_BATCH, nbp, N_PROMPT, prev=slots)
    eo = _experts(tile_expert.astype(jnp.int32), tc, tile_rows, slots, wg, wu, wd, layer)
    yp = _combine(seg, xp, cm, eo, MOE_BLOCK, 0, 0)
    ys = _combine(seg, xs, cm, eo, DEC_BATCH, nbp, N_PROMPT)
    return yp, ys


def kernel(x_prompt, x_sample, state_conv, cache_k, cache_v, norm_mix_g, norm_ffn_g, w_in_ab, conv_w, conv_b,
           conv_ln_g, conv_ln_b, q_norm_g, k_norm_g, attn_sinks, w_out_ab, w_gate_dense, w_up_dense, w_down_dense,
           w_in_c, c_ln_g, c_ln_b, w_spatial, b_spatial, w_out_c, w_router, w_gate_exp, w_up_exp, w_down_exp):
    xp = x_prompt.reshape(N_PROMPT, D_MODEL)
    xs = x_sample.reshape(DEC_BATCH, D_MODEL)
    head = jnp.arange(B_WIDTH) // HEAD_DIM
    seg = (head[:, None] == head[None, :]).astype(BF16)
    conv_p, conv_s, kp_out, vp_out, ks_out, vs_out, vc_out = [], [], [], [], [], [], []
    row = lambda a: a.reshape(1, -1)
    for layer in range(DEPTH):
        i = layer // 2
        g_mix = row(norm_mix_g[layer])
        g_ffn = row(norm_ffn_g[layer])
        if layer % 2 == 0:
            w_in = w_in_ab[i].astype(BF16)
            qg = row(jnp.tile(q_norm_g[i], N_HEADS))
            kg = row(jnp.tile(k_norm_g[i], N_KV_HEADS))
            a_p, q_p, k_p, v_p = _inproj(xp, g_mix, w_in, qg, kg, seg, ROW_TILE)
            a_s, q_s, k_s, v_s = _inproj(xs, g_mix, w_in, qg, kg, seg, DEC_BATCH)
            cb, lg, lb = row(conv_b[i]), row(conv_ln_g[i]), row(conv_ln_b[i])
            ca_p = _conv_prompt(a_p, conv_w[i], cb, lg, lb)
            ca_s = _conv_sample(a_s, jnp.transpose(state_conv[i], (1, 0, 2)), conv_w[i], cb, lg, lb)
            ob_p = _attn_prompt(q_p, k_p, v_p, attn_sinks[i])
            sinks_b = jnp.broadcast_to(attn_sinks[i].reshape(N_KV_HEADS, Q_GROUP, 1), (N_KV_HEADS, Q_GROUP, 128))
            ob_s = _attn_sample(q_s.reshape(DEC_BATCH, N_KV_HEADS, Q_GROUP, HEAD_DIM),
                                k_s.reshape(DEC_BATCH, N_KV_HEADS, HEAD_DIM),
                                v_s.reshape(DEC_BATCH, N_KV_HEADS, HEAD_DIM),
                                cache_k[i].reshape(DEC_BATCH, WINDOW, KV_WIDTH),
                                cache_v[i].reshape(DEC_BATCH, WINDOW, KV_WIDTH), sinks_b)
            ob_s = ob_s.reshape(DEC_BATCH, B_WIDTH)
            conv_p.append(a_p.reshape(BATCH, SEQ, A_CH)[:, SEQ - (CONV_W - 1):])
            conv_s.append(jnp.concatenate([state_conv[i][:, 1:], a_s[:, None, :]], axis=1))
            kp_out.append(k_p.reshape(BATCH, SEQ, N_KV_HEADS, HEAD_DIM)[:, SEQ - WINDOW:])
            vp_out.append(v_p.reshape(BATCH, SEQ, N_KV_HEADS, HEAD_DIM)[:, SEQ - WINDOW:])
            ks_out.append(jnp.concatenate(
                [cache_k[i][:, 1:], k_s.reshape(DEC_BATCH, 1, N_KV_HEADS, HEAD_DIM)], axis=1))
            vs_out.append(jnp.concatenate(
                [cache_v[i][:, 1:], v_s.reshape(DEC_BATCH, 1, N_KV_HEADS, HEAD_DIM)], axis=1))
            wo = w_out_ab[i].astype(BF16)
            wg = w_gate_dense[i].astype(BF16)
            wu = w_up_dense[i].astype(BF16)
            wd = w_down_dense[i].astype(BF16)
            xp = _outproj_ffn(xp, ca_p, ob_p, wo, g_ffn, wg, wu, wd, ROW_TILE)
            xs = _outproj_ffn(xs, ca_s, ob_s, wo, g_ffn, wg, wu, wd, DEC_BATCH)
        else:
            win = w_in_c[i].astype(BF16)
            wout = w_out_c[i].astype(BF16)
            lg, lb = row(c_ln_g[i]), row(c_ln_b[i])
            xp = _gmlp_prompt(xp, g_mix, win, lg, lb, w_spatial[i], b_spatial[i].T, wout, ROW_TILE)
            gd = C_WIDTH // C_GROUPS
            w0 = row(jnp.repeat(w_spatial[i][:, 0, 0], gd))
            b0 = row(jnp.repeat(b_spatial[i][:, 0], gd))
            xs, v_s = _gmlp_sample(xs, g_mix, win, lg, lb, w0, b0, wout)
            vc_out.append(v_s.reshape(DEC_BATCH, 1, C_WIDTH))
            xp, xs = _moe(xp, xs, norm_ffn_g[layer], w_router[i], w_gate_exp, w_up_exp, w_down_exp, i)
    return (xp.reshape(BATCH, SEQ, D_MODEL), xs.reshape(DEC_BATCH, 1, D_MODEL),
            jnp.stack(conv_p), jnp.stack(conv_s), jnp.stack(kp_out), jnp.stack(vp_out),
            jnp.stack(ks_out), jnp.stack(vs_out), jnp.stack(vc_out))
```

```python
import functools

import jax
import jax.numpy as jnp
from jax import lax
from jax.experimental import pallas as pl
from jax.experimental.pallas import tpu as pltpu

F32 = jnp.float32
BF16 = jnp.bfloat16

D_MODEL = 1024
BATCH = 8
SEQ = 2048
DEPTH = 4
DEC_BATCH = 128
A_CH = 512
CONV_W = 31
N_HEADS = 8
N_KV_HEADS = 2
HEAD_DIM = 64
Q_GROUP = 4
WINDOW = 128
B_WIDTH = 512
KV_WIDTH = 128
AB_IN = 2 * A_CH + B_WIDTH + 2 * KV_WIDTH
C_WIDTH = 1024
C_GROUPS = 8
C_CHUNK = 128
D_FF = 2816
N_EXPERTS = 8
D_FF_EXPERT = 3584
EPS = 1e-6
ATTN_SCALE = HEAD_DIM ** -0.5

N_PROMPT = BATCH * SEQ
N_TOK = N_PROMPT + DEC_BATCH
ROW_TILE = 1024
FF_SPLIT = 2
MOE_TILE = 1024
MOE_FF_TILE = 512
MOE_FF_SUB = 256
MOE_NF = D_FF_EXPERT // MOE_FF_TILE
MOE_BLOCK = 256
MOE_N_BLOCKS = N_PROMPT // MOE_BLOCK + 1
SEG_ALIGN = 8
MOE_MAX_TILES = (2 * N_TOK + MOE_N_BLOCKS * N_EXPERTS * (SEG_ALIGN - 1) + N_EXPERTS * (MOE_TILE - 1)) // MOE_TILE
N_SLOTS = MOE_MAX_TILES * MOE_TILE
VMEM_LIMIT = 56 * 1024 * 1024


def _params(n_axes, vmem=VMEM_LIMIT):
    return pltpu.CompilerParams(dimension_semantics=("arbitrary",) * n_axes, vmem_limit_bytes=vmem)


def _rms(x, g):
    return x * lax.rsqrt(jnp.mean(x * x, axis=-1, keepdims=True) + EPS) * g


def _layer_norm(x, g, b):
    mu = jnp.mean(x, axis=-1, keepdims=True)
    xc = x - mu
    var = jnp.mean(xc * xc, axis=-1, keepdims=True)
    return xc * lax.rsqrt(var + EPS) * g + b


def _const_spec(shape):
    return pl.BlockSpec(shape, lambda *_: (0,) * len(shape))


def _inproj_kernel(x_ref, g_ref, w_ref, qg_ref, kg_ref, seg_ref, a_ref, q_ref, k_ref, v_ref):
    h = _rms(x_ref[...], g_ref[...]).astype(BF16)
    z = jnp.dot(h, w_ref[...], preferred_element_type=F32)
    a_ref[...] = z[:, :A_CH] * jax.nn.sigmoid(z[:, A_CH:2 * A_CH])
    seg = seg_ref[...]

    def head_norm(t, gain, seg_t):
        sq = t * t
        hi = sq.astype(BF16)
        lo = (sq - hi.astype(F32)).astype(BF16)
        ms = (jnp.dot(hi, seg_t, preferred_element_type=F32)
              + jnp.dot(lo, seg_t, preferred_element_type=F32)) * (1.0 / HEAD_DIM)
        return t * lax.rsqrt(ms + EPS) * gain

    o = 2 * A_CH
    q = head_norm(z[:, o:o + B_WIDTH], qg_ref[...], seg)
    q_ref[...] = (q * ATTN_SCALE).astype(BF16)
    o += B_WIDTH
    k_ref[...] = head_norm(z[:, o:o + KV_WIDTH], kg_ref[...], seg[:KV_WIDTH, :KV_WIDTH])
    o += KV_WIDTH
    v_ref[...] = z[:, o:o + KV_WIDTH]


def _inproj(x, g, w_bf, qg, kg, seg, tm):
    n = x.shape[0]
    row = lambda w: pl.BlockSpec((tm, w), lambda i: (i, 0))
    return pl.pallas_call(
        _inproj_kernel,
        grid=(n // tm,),
        in_specs=[row(D_MODEL), _const_spec((1, D_MODEL)), _const_spec((D_MODEL, AB_IN)),
                  _const_spec((1, B_WIDTH)), _const_spec((1, KV_WIDTH)), _const_spec((B_WIDTH, B_WIDTH))],
        out_specs=[row(A_CH), row(B_WIDTH), row(KV_WIDTH), row(KV_WIDTH)],
        out_shape=[jax.ShapeDtypeStruct((n, A_CH), F32), jax.ShapeDtypeStruct((n, B_WIDTH), BF16),
                   jax.ShapeDtypeStruct((n, KV_WIDTH), F32), jax.ShapeDtypeStruct((n, KV_WIDTH), F32)],
        compiler_params=_params(1),
        name="inproj_ab",
    )(x, g, w_bf, qg, kg, seg)


CONV_PAD = 32
CONV_ROWS = 128


def _conv_tail(c, lg, lb):
    y = _layer_norm(c, lg, lb)
    return y * jax.nn.sigmoid(y)


def _conv_prompt_kernel(a_ref, w_ref, b_ref, lg_ref, lb_ref, o_ref, pad_ref, sh_ref, c_ref):
    pad_ref[0:CONV_PAD, :] = jnp.zeros((CONV_PAD, A_CH), F32)
    pad_ref[CONV_PAD:, :] = a_ref[...]
    win = CONV_ROWS + CONV_PAD

    def body(r, carry):
        t0 = pl.multiple_of(r * CONV_ROWS, CONV_ROWS)
        sh_ref[0] = pad_ref[pl.ds(t0, win), :]
        for s in range(1, 8):
            sh_ref[s, 0:win - 8, :] = sh_ref[0, s:s + win - 8, :]
        for c in range(A_CH // 128):
            lanes = slice(c * 128, (c + 1) * 128)
            acc = jnp.zeros((CONV_ROWS, 128), F32)
            for k in range(CONV_W):
                start = CONV_PAD - (CONV_W - 1) + k
                s, base = start % 8, start - start % 8
                acc = acc + sh_ref[s, base:base + CONV_ROWS, lanes] * w_ref[k:k + 1, lanes]
            c_ref[:, lanes] = acc
        y = _conv_tail(c_ref[...] + b_ref[...], lg_ref[...], lb_ref[...])
        o_ref[pl.ds(t0, CONV_ROWS), :] = y.astype(BF16)
        return carry

    lax.fori_loop(0, SEQ // CONV_ROWS, body, 0)


def _conv_prompt(a, w, b, lg, lb):
    return pl.pallas_call(
        _conv_prompt_kernel,
        grid=(BATCH,),
        in_specs=[pl.BlockSpec((SEQ, A_CH), lambda i: (i, 0)), _const_spec((CONV_W, A_CH)),
                  _const_spec((1, A_CH)), _const_spec((1, A_CH)), _const_spec((1, A_CH))],
        out_specs=pl.BlockSpec((SEQ, A_CH), lambda i: (i, 0)),
        out_shape=jax.ShapeDtypeStruct((N_PROMPT, A_CH), BF16),
        scratch_shapes=[pltpu.VMEM((CONV_PAD + SEQ, A_CH), F32), pltpu.VMEM((8, CONV_ROWS + CONV_PAD, A_CH), F32),
                        pltpu.VMEM((CONV_ROWS, A_CH), F32)],
        compiler_params=_params(1),
        name="conv_prompt",
    )(a, w, b, lg, lb)


def _conv_sample_kernel(a_ref, st_ref, w_ref, b_ref, lg_ref, lb_ref, o_ref):
    acc = a_ref[...] * w_ref[CONV_W - 1:CONV_W, :]
    for k in range(CONV_W - 1):
        acc = acc + st_ref[k] * w_ref[k:k + 1, :]
    o_ref[...] = _conv_tail(acc + b_ref[...], lg_ref[...], lb_ref[...]).astype(BF16)


def _conv_sample(a, st_t, w, b, lg, lb):
    return pl.pallas_call(
        _conv_sample_kernel,
        grid=(1,),
        in_specs=[_const_spec((DEC_BATCH, A_CH)), _const_spec((CONV_W - 1, DEC_BATCH, A_CH)),
                  _const_spec((CONV_W, A_CH)), _const_spec((1, A_CH)), _const_spec((1, A_CH)),
                  _const_spec((1, A_CH))],
        out_specs=_const_spec((DEC_BATCH, A_CH)),
        out_shape=jax.ShapeDtypeStruct((DEC_BATCH, A_CH), BF16),
        compiler_params=_params(1),
        name="conv_sample",
    )(a, st_t, w, b, lg, lb)


def _alibi_slope(h):
    return 2.0 ** (-8.0 * (h + 1) / N_HEADS)


ATTN_P_WINDOWS = 4


def _attn_prompt_kernel(sink_ref, q_ref, kp_ref, kc_ref, vp_ref, vc_ref, o_ref):
    step = pl.program_id(1)
    kk = jnp.concatenate([kp_ref[...], kc_ref[...]], axis=0).astype(BF16)
    vv = jnp.concatenate([vp_ref[...], vc_ref[...]], axis=0).astype(BF16)
    qi = lax.broadcasted_iota(jnp.int32, (WINDOW, 2 * WINDOW), 0)
    kj = lax.broadcasted_iota(jnp.int32, (WINDOW, 2 * WINDOW), 1)
    dist_i = WINDOW + qi - kj
    in_window = (dist_i >= 0) & (dist_i < WINDOW)
    lo = jnp.where(step > 0, 0, WINDOW)
    dist = dist_i.astype(F32)
    for w in range(ATTN_P_WINDOWS):
        valid = in_window & (kj >= lo) if w == 0 else in_window
        rows = slice(w * WINDOW, (w + 1) * WINDOW)
        keys = slice(w * WINDOW, (w + 2) * WINDOW)
        outs = []
        for h in range(N_HEADS):
            kv = h // Q_GROUP
            lanes = slice(kv * HEAD_DIM, (kv + 1) * HEAD_DIM)
            qh = q_ref[rows, h * HEAD_DIM:(h + 1) * HEAD_DIM]
            s = lax.dot_general(qh, kk[keys, lanes], (((1,), (1,)), ((), ())), preferred_element_type=F32)
            s = jnp.where(valid, s - _alibi_slope(h) * dist, -jnp.inf)
            sink = sink_ref[h]
            m = jnp.maximum(jnp.max(s, axis=-1, keepdims=True), sink)
            p = jnp.exp(s - m)
            denom = jnp.sum(p, axis=-1, keepdims=True) + jnp.exp(sink - m)
            p = (p / denom).astype(BF16)
            outs.append(jnp.dot(p, vv[keys, lanes], preferred_element_type=F32))
        o_ref[rows, :] = jnp.concatenate(outs, axis=-1).astype(BF16)


def _attn_prompt(q, k, v, sinks):
    rows = ATTN_P_WINDOWS * WINDOW
    ns = SEQ // rows
    own = lambda b, j: (b * ns + j, 0)
    prev = lambda b, j: (b * (SEQ // WINDOW) + jnp.maximum(j * ATTN_P_WINDOWS - 1, 0), 0)
    return pl.pallas_call(
        _attn_prompt_kernel,
        grid=(BATCH, ns),
        in_specs=[pl.BlockSpec(memory_space=pltpu.SMEM),
                  pl.BlockSpec((rows, B_WIDTH), own),
                  pl.BlockSpec((WINDOW, KV_WIDTH), prev), pl.BlockSpec((rows, KV_WIDTH), own),
                  pl.BlockSpec((WINDOW, KV_WIDTH), prev), pl.BlockSpec((rows, KV_WIDTH), own)],
        out_specs=pl.BlockSpec((rows, B_WIDTH), own),
        out_shape=jax.ShapeDtypeStruct((N_PROMPT, B_WIDTH), BF16),
        compiler_params=_params(2),
        name="attn_prompt",
    )(sinks, q, k, k, v, v)


ATTN_S_BLK = 32


def _attn_sample_kernel(sink_ref, q_ref, kn_ref, vn_ref, kc_ref, vc_ref, o_ref):
    kc = kc_ref[...].astype(BF16)
    vc = vc_ref[...].astype(BF16)
    j = lax.broadcasted_iota(jnp.int32, (1, 1, WINDOW), 2)
    dist = (WINDOW - j).astype(F32)
    g_idx = lax.broadcasted_iota(jnp.int32, (1, Q_GROUP, 1), 1)
    for kv in range(N_KV_HEADS):
        lanes = slice(kv * HEAD_DIM, (kv + 1) * HEAD_DIM)
        qg = q_ref[:, kv]
        slope = jnp.exp2(-(g_idx + (kv * Q_GROUP + 1)).astype(F32) * (8.0 / N_HEADS))
        s = jnp.einsum('ngd,njd->ngj', qg, kc[:, :, lanes], preferred_element_type=F32)
        s = jnp.where(j >= 1, s - slope * dist, -jnp.inf)
        kn = kn_ref[:, kv].astype(BF16).astype(F32)
        s_new = jnp.sum(qg.astype(F32) * kn[:, None, :], axis=-1, keepdims=True)
        sink = sink_ref[kv][:, :1][None]
        m = jnp.maximum(jnp.maximum(jnp.max(s, axis=-1, keepdims=True), s_new), sink)
        p = jnp.exp(s - m)
        p_new = jnp.exp(s_new - m)
        denom = jnp.sum(p, axis=-1, keepdims=True) + p_new + jnp.exp(sink - m)
        o = jnp.einsum('ngj,njd->ngd', (p / denom).astype(BF16), vc[:, :, lanes], preferred_element_type=F32)
        vn = vn_ref[:, kv].astype(BF16).astype(F32)
        o = o + (p_new / denom).astype(BF16).astype(F32) * vn[:, None, :]
        o_ref[:, kv] = o.astype(BF16)


def _attn_sample(q4, kn3, vn3, kc, vc, sinks_b):
    nb = ATTN_S_BLK
    return pl.pallas_call(
        _attn_sample_kernel,
        grid=(DEC_BATCH // nb,),
        in_specs=[_const_spec((N_KV_HEADS, Q_GROUP, 128)),
                  pl.BlockSpec((nb, N_KV_HEADS, Q_GROUP, HEAD_DIM), lambda i: (i, 0, 0, 0)),
                  pl.BlockSpec((nb, N_KV_HEADS, HEAD_DIM), lambda i: (i, 0, 0)),
                  pl.BlockSpec((nb, N_KV_HEADS, HEAD_DIM), lambda i: (i, 0, 0)),
                  pl.BlockSpec((nb, WINDOW, KV_WIDTH), lambda i: (i, 0, 0)),
                  pl.BlockSpec((nb, WINDOW, KV_WIDTH), lambda i: (i, 0, 0))],
        out_specs=pl.BlockSpec((nb, N_KV_HEADS, Q_GROUP, HEAD_DIM), lambda i: (i, 0, 0, 0)),
        out_shape=jax.ShapeDtypeStruct((DEC_BATCH, N_KV_HEADS, Q_GROUP, HEAD_DIM), BF16),
        compiler_params=_params(1),
        name="attn_sample",
    )(sinks_b, q4, kn3, vn3, kc, vc)


def _outproj_ffn_kernel(x_ref, ca_ref, ob_ref, g_ref, wo_hbm, wg_hbm, wu_hbm, wd_hbm, o_ref,
                        wo_ref, wg_ref, wu_ref, wd_ref):
    @pl.when(pl.program_id(0) == 0)
    def _():
        pltpu.sync_copy(wo_hbm, wo_ref)
        pltpu.sync_copy(wg_hbm, wg_ref)
        pltpu.sync_copy(wu_hbm, wu_ref)
        pltpu.sync_copy(wd_hbm, wd_ref)

    o_ref[...] = (x_ref[...]
                  + jnp.dot(ca_ref[...], wo_ref[:A_CH, :], preferred_element_type=F32)
                  + jnp.dot(ob_ref[...], wo_ref[A_CH:, :], preferred_element_type=F32))
    h = _rms(o_ref[...], g_ref[...]).astype(BF16)
    ff = D_FF // FF_SPLIT
    for c in range(FF_SPLIT):
        cols = slice(c * ff, (c + 1) * ff)
        gate = jnp.dot(h, wg_ref[:, cols], preferred_element_type=F32)
        up = jnp.dot(h, wu_ref[:, cols], preferred_element_type=F32)
        act = (gate * jax.nn.sigmoid(gate) * up).astype(BF16)
        o_ref[...] += jnp.dot(act, wd_ref[cols, :], preferred_element_type=F32)


def _outproj_ffn(x, ca, ob, wo, g, wg, wu, wd, tm):
    n = x.shape[0]
    row = lambda w: pl.BlockSpec((tm, w), lambda i: (i, 0))
    hbm = pl.BlockSpec(memory_space=pl.ANY)
    return pl.pallas_call(
        _outproj_ffn_kernel,
        grid=(n // tm,),
        in_specs=[row(D_MODEL), row(A_CH), row(B_WIDTH), _const_spec((1, D_MODEL)), hbm, hbm, hbm, hbm],
        out_specs=row(D_MODEL),
        out_shape=jax.ShapeDtypeStruct((n, D_MODEL), F32),
        scratch_shapes=[pltpu.VMEM((D_MODEL, D_MODEL), BF16), pltpu.VMEM((D_MODEL, D_FF), BF16),
                        pltpu.VMEM((D_MODEL, D_FF), BF16), pltpu.VMEM((D_FF, D_MODEL), BF16)],
        compiler_params=_params(1),
        name="outproj_ffn",
    )(x, ca, ob, g, wo, wg, wu, wd)


def _gmlp_prompt_kernel(x_ref, g_ref, win_ref, lg_ref, lb_ref, ws_ref, bs_ref, wout_ref, o_ref, gate_ref):
    x = x_ref[...]
    h = _rms(x, g_ref[...]).astype(BF16)
    z = jax.nn.gelu(jnp.dot(h, win_ref[...], preferred_element_type=F32))
    u = z[:, :C_WIDTH]
    v = _layer_norm(z[:, C_WIDTH:], lg_ref[...], lb_ref[...]).astype(BF16)
    ri = lax.broadcasted_iota(jnp.int32, (C_CHUNK, C_CHUNK), 0)
    ci = lax.broadcasted_iota(jnp.int32, (C_CHUNK, C_CHUNK), 1)
    tm = x.shape[0]
    gd = C_WIDTH // C_GROUPS
    for grp in range(C_GROUPS):
        w = jnp.where(ri >= ci, ws_ref[grp], 0.0).astype(BF16)
        b = bs_ref[:, grp:grp + 1]
        lanes = slice(grp * gd, (grp + 1) * gd)
        for c in range(tm // C_CHUNK):
            rows = slice(c * C_CHUNK, (c + 1) * C_CHUNK)
            mixed = jnp.dot(w, v[rows, lanes], preferred_element_type=F32) + b
            gate_ref[rows, lanes] = (u[rows, lanes] * mixed).astype(BF16)
    o_ref[...] = x + jnp.dot(gate_ref[...], wout_ref[...], preferred_element_type=F32)


def _gmlp_prompt(x, g, win, lg, lb, ws, bs_t, wout, tm):
    n = x.shape[0]
    row = lambda w: pl.BlockSpec((tm, w), lambda i: (i, 0))
    return pl.pallas_call(
        _gmlp_prompt_kernel,
        grid=(n // tm,),
        in_specs=[row(D_MODEL), _const_spec((1, D_MODEL)), _const_spec((D_MODEL, 2 * C_WIDTH)),
                  _const_spec((1, C_WIDTH)), _const_spec((1, C_WIDTH)),
                  _const_spec((C_GROUPS, C_CHUNK, C_CHUNK)), _const_spec((C_CHUNK, C_GROUPS)),
                  _const_spec((C_WIDTH, D_MODEL))],
        out_specs=row(D_MODEL),
        out_shape=jax.ShapeDtypeStruct((n, D_MODEL), F32),
        scratch_shapes=[pltpu.VMEM((tm, C_WIDTH), BF16)],
        compiler_params=_params(1),
        name="gmlp_prompt",
    )(x, g, win, lg, lb, ws, bs_t, wout)


def _gmlp_sample_kernel(x_ref, g_ref, win_ref, lg_ref, lb_ref, w0_ref, b0_ref, wout_ref, o_ref, v_ref):
    x = x_ref[...]
    h = _rms(x, g_ref[...]).astype(BF16)
    z = jax.nn.gelu(jnp.dot(h, win_ref[...], preferred_element_type=F32))
    v = _layer_norm(z[:, C_WIDTH:], lg_ref[...], lb_ref[...])
    v_ref[...] = v
    gated = z[:, :C_WIDTH] * (w0_ref[...] * v + b0_ref[...])
    o_ref[...] = x + jnp.dot(gated.astype(BF16), wout_ref[...], preferred_element_type=F32)


def _gmlp_sample(x, g, win, lg, lb, w0, b0, wout):
    n = x.shape[0]
    return pl.pallas_call(
        _gmlp_sample_kernel,
        grid=(1,),
        in_specs=[_const_spec((n, D_MODEL)), _const_spec((1, D_MODEL)), _const_spec((D_MODEL, 2 * C_WIDTH)),
                  _const_spec((1, C_WIDTH)), _const_spec((1, C_WIDTH)), _const_spec((1, C_WIDTH)),
                  _const_spec((1, C_WIDTH)), _const_spec((C_WIDTH, D_MODEL))],
        out_specs=[_const_spec((n, D_MODEL)), _const_spec((n, C_WIDTH))],
        out_shape=[jax.ShapeDtypeStruct((n, D_MODEL), F32), jax.ShapeDtypeStruct((n, C_WIDTH), F32)],
        compiler_params=_params(1),
        name="gmlp_sample",
    )(x, g, win, lg, lb, w0, b0, wout)


META_W = 8


def _split_bf16(a):
    hi = a.astype(BF16)
    return hi, (a - hi.astype(F32)).astype(BF16)


def _router_kernel(x_ref, g_ref, wr_ref, *rest, bt):
    hn_ref, cm_ref, lst_ref, cnt_ref = rest[-4:]
    h = _rms(x_ref[...], g_ref[...])
    tm = h.shape[0]
    h_hi, h_lo = _split_bf16(h)
    w_hi, w_lo = _split_bf16(wr_ref[...])
    hn_ref[...] = h_hi
    logits = (jnp.dot(h_hi, w_hi, preferred_element_type=F32) + jnp.dot(h_lo, w_hi, preferred_element_type=F32)
              + jnp.dot(h_hi, w_lo, preferred_element_type=F32))
    lane = lax.broadcasted_iota(jnp.int32, (bt, 128), 1).astype(F32)
    neg = -jnp.inf
    ri = lax.broadcasted_iota(jnp.int32, (bt, bt), 0)
    ci = lax.broadcasted_iota(jnp.int32, (bt, bt), 1)
    earlier = jnp.where(ri > ci, 1.0, 0.0).astype(BF16)
    ei = lax.broadcasted_iota(jnp.int32, (128, 128), 0)
    ej = lax.broadcasted_iota(jnp.int32, (128, 128), 1)
    lower_experts = jnp.where(ei < ej, 1.0, 0.0).astype(BF16)
    for j in range(tm // bt):
        rows = slice(j * bt, (j + 1) * bt)
        lg = jnp.where(lane < N_EXPERTS, logits[rows], neg)
        m1 = jnp.max(lg, axis=-1, keepdims=True)
        i1 = jnp.min(jnp.where(lg == m1, lane, 128.0), axis=-1, keepdims=True)
        lg2 = jnp.where(lane == i1, neg, lg)
        m2 = jnp.max(lg2, axis=-1, keepdims=True)
        i2 = jnp.min(jnp.where(lg2 == m2, lane, 128.0), axis=-1, keepdims=True)
        e = jnp.exp(m2 - m1)
        g1 = 1.0 / (1.0 + e)
        g2 = e / (1.0 + e)
        sel1 = lane == i1
        sel2 = lane == i2
        oh = jnp.where(sel1, 1.0, jnp.where(sel2, 1.0, 0.0))
        before = jnp.dot(earlier, oh.astype(BF16), preferred_element_type=F32)
        cnt = jnp.sum(oh, axis=0, keepdims=True)
        padded = jnp.floor((cnt + (SEG_ALIGN - 1)) * (1.0 / SEG_ALIGN)) * SEG_ALIGN
        start = jnp.dot(jnp.broadcast_to(padded, (8, 128)).astype(BF16), lower_experts,
                        preferred_element_type=F32)[0:1]
        slot = before + start
        l1 = jnp.sum(jnp.where(sel1, slot, 0.0), axis=-1, keepdims=True)
        l2 = jnp.sum(jnp.where(sel2, slot, 0.0), axis=-1, keepdims=True)
        rec = jnp.where(lane == 0, l1, jnp.where(lane == 1, l2, jnp.where(lane == 2, g1, jnp.where(lane == 3, g2, 0.0))))
        cm_ref[rows, :] = rec[:, :META_W]
        lst_ref[j] = rec.T[0:2, :].astype(jnp.int32)
        cnt_ref[j] = cnt


def _router(x, g, wr_pad, tm, bt, row_blk0, prev=None):
    n = x.shape[0]
    nblk = tm // bt
    in_specs = [pl.BlockSpec((tm, D_MODEL), lambda i: (i, 0)), _const_spec((1, D_MODEL)),
                _const_spec((D_MODEL, 128))]
    args = [x, g, wr_pad]
    aliases = {}
    if prev is not None:
        in_specs += [pl.BlockSpec(memory_space=pl.ANY), pl.BlockSpec(memory_space=pl.ANY)]
        args += list(prev)
        aliases = {3: 0, 4: 1}
    return pl.pallas_call(
        functools.partial(_router_kernel, bt=bt),
        grid=(n // tm,),
        in_specs=in_specs,
        out_specs=[pl.BlockSpec((tm, D_MODEL), lambda i: (row_blk0 + i, 0)),
                   pl.BlockSpec((tm, META_W), lambda i: (row_blk0 + i, 0)),
                   pl.BlockSpec((nblk, 2, bt), lambda i: (i, 0, 0)),
                   pl.BlockSpec((nblk, 1, 128), lambda i: (i, 0, 0))],
        out_shape=[jax.ShapeDtypeStruct((N_TOK, D_MODEL), BF16), jax.ShapeDtypeStruct((N_TOK, META_W), F32),
                   jax.ShapeDtypeStruct((n // bt, 2, bt), jnp.int32), jax.ShapeDtypeStruct((n // bt, 1, 128), F32)],
        input_output_aliases=aliases,
        compiler_params=_params(1),
        name="moe_router",
    )(*args)


def _local_rows(bt):
    return 2 * bt + N_EXPERTS * SEG_ALIGN


def _segment_copies(seg_n, seg_src, seg_dst, blk, local_ref, slot_ref, sem, bt, to_slots, action):
    def per_expert(e, c):
        s = blk * N_EXPERTS + e
        n, src, dst = seg_n[s], seg_src[s], seg_dst[s]
        off = jnp.int32(0)
        p = bt
        while p >= SEG_ALIGN:
            hit = (n & p) != 0

            @pl.when(hit)
            def _(off=off, p=p):
                a = local_ref.at[pl.ds(pl.multiple_of(src + off, SEG_ALIGN), p)]
                b = slot_ref.at[pl.ds(pl.multiple_of(dst + off, SEG_ALIGN), p)]
                action(pltpu.make_async_copy(a, b, sem) if to_slots else pltpu.make_async_copy(b, a, sem))

            off = off + (n & p)
            p //= 2
        return c

    lax.fori_loop(0, N_EXPERTS, per_expert, 0)


def _dispatch_kernel(seg_n, seg_src, seg_dst, hn_ref, ls_ref, *rest, bt, blk0, nb):
    slots_ref, buf_ref, sem = rest[-3:]
    i = pl.program_id(0)
    par = i % 2
    start = lambda cp: cp.start()
    wait = lambda cp: cp.wait()
    copies = lambda blk, p, action: _segment_copies(seg_n, seg_src, seg_dst, blk, buf_ref.at[p], slots_ref,
                                                     sem.at[p], bt, True, action)

    @pl.when(i >= 2)
    def _():
        copies(blk0 + i - 2, par, wait)

    rows = lax.broadcasted_iota(jnp.int32, (_local_rows(bt), bt), 0)
    ls = ls_ref[0]
    onehot = jnp.where(rows == ls[0:1, :], 1.0, jnp.where(rows == ls[1:2, :], 1.0, 0.0)).astype(BF16)
    buf_ref[par] = jnp.dot(onehot, hn_ref[...], preferred_element_type=F32)
    copies(blk0 + i, par, start)

    @pl.when(i == nb - 1)
    def _():
        if nb >= 2:
            copies(blk0 + i - 1, 1 - par, wait)
        copies(blk0 + i, par, wait)


def _dispatch(seg, hn, ls_t, bt, blk0, tok0, prev=None):
    nb = ls_t.shape[0]
    in_specs = [pl.BlockSpec((bt, D_MODEL), lambda i, *_: (tok0 // bt + i, 0)),
                pl.BlockSpec((1, 2, bt), lambda i, *_: (i, 0, 0))]
    args = [hn, ls_t]
    aliases = {}
    if prev is not None:
        in_specs.append(pl.BlockSpec(memory_space=pl.ANY))
        args.append(prev)
        aliases = {5: 0}
    return pl.pallas_call(
        functools.partial(_dispatch_kernel, bt=bt, blk0=blk0, nb=nb),
        grid_spec=pltpu.PrefetchScalarGridSpec(
            num_scalar_prefetch=3,
            grid=(nb,),
            in_specs=in_specs,
            out_specs=pl.BlockSpec(memory_space=pl.ANY),
            scratch_shapes=[pltpu.VMEM((2, _local_rows(bt), D_MODEL), F32), pltpu.SemaphoreType.DMA((2,))],
        ),
        out_shape=jax.ShapeDtypeStruct((N_SLOTS, D_MODEL), F32),
        input_output_aliases=aliases,
        compiler_params=_params(1),
        name="moe_dispatch",
    )(*seg, *args)


def _experts_kernel(te_ref, tb_ref, nv_ref, x_ref, wg_ref, wu_ref, wd_ref, o_ref, xb_ref, acc_ref):
    t = pl.program_id(0)
    f = pl.program_id(1)
    nv = nv_ref[t]

    @pl.when(jnp.logical_and(nv > 0, f == 0))
    def _():
        rows = lax.broadcasted_iota(jnp.int32, (MOE_TILE, 1), 0)
        xb_ref[...] = jnp.where(rows < nv, x_ref[...], 0.0).astype(BF16)

    @pl.when(nv > 0)
    def _():
        xb = xb_ref[...]
        part = None
        for c in range(MOE_FF_TILE // MOE_FF_SUB):
            cols = slice(c * MOE_FF_SUB, (c + 1) * MOE_FF_SUB)
            gate = jnp.dot(xb, wg_ref[:, cols].astype(BF16), preferred_element_type=F32)
            up = jnp.dot(xb, wu_ref[:, cols].astype(BF16), preferred_element_type=F32)
            act = (gate * jax.nn.sigmoid(gate) * up).astype(BF16)
            p = jnp.dot(act, wd_ref[cols, :].astype(BF16), preferred_element_type=F32)
            part = p if part is None else part + p

        @pl.when(f == 0)
        def _():
            acc_ref[...] = part

        @pl.when(f > 0)
        def _():
            acc_ref[...] = acc_ref[...] + part

    @pl.when(jnp.logical_and(nv > 0, f == MOE_NF - 1))
    def _():
        o_ref[...] = acc_ref[...]


def _experts(tile_expert, tile_blk, tile_rows, xs, wg, wu, wd, layer):
    def chunk(t, f, nv):
        return jnp.where(nv[t] > 0, f, MOE_NF - 1)
    up_spec = pl.BlockSpec((None, None, D_MODEL, MOE_FF_TILE),
                           lambda t, f, te, tb, nv: (layer, te[t], 0, chunk(t, f, nv)))
    down_spec = pl.BlockSpec((None, None, MOE_FF_TILE, D_MODEL),
                             lambda t, f, te, tb, nv: (layer, te[t], chunk(t, f, nv), 0))
    return pl.pallas_call(
        _experts_kernel,
        grid_spec=pltpu.PrefetchScalarGridSpec(
            num_scalar_prefetch=3,
            grid=(MOE_MAX_TILES, MOE_NF),
            in_specs=[pl.BlockSpec((MOE_TILE, D_MODEL), lambda t, f, te, tb, nv: (tb[t], 0)),
                      up_spec, up_spec, down_spec],
            out_specs=pl.BlockSpec((MOE_TILE, D_MODEL), lambda t, f, te, tb, nv: (tb[t], 0)),
            scratch_shapes=[pltpu.VMEM((MOE_TILE, D_MODEL), BF16), pltpu.VMEM((MOE_TILE, D_MODEL), F32)],
        ),
        out_shape=jax.ShapeDtypeStruct((N_SLOTS, D_MODEL), F32),
        compiler_params=_params(2),
        name="moe_experts",
    )(tile_expert, tile_blk, tile_rows, xs, wg, wu, wd)


def _combine_kernel(seg_n, seg_src, seg_dst, x_ref, cm_ref, eo_ref, y_ref, buf_ref, sem, *, bt, blk0, nb):
    i = pl.program_id(0)
    par = i % 2
    start = lambda cp: cp.start()
    wait = lambda cp: cp.wait()
    copies = lambda blk, p, action: _segment_copies(seg_n, seg_src, seg_dst, blk, buf_ref.at[p], eo_ref,
                                                     sem.at[p], bt, False, action)

    @pl.when(i == 0)
    def _():
        copies(blk0, 0, start)

    @pl.when(i + 1 < nb)
    def _():
        copies(blk0 + i + 1, 1 - par, start)

    copies(blk0 + i, par, wait)
    last = (blk0 + i) * N_EXPERTS + (N_EXPERTS - 1)
    used = seg_src[last] + seg_n[last]
    rows = lax.broadcasted_iota(jnp.int32, (_local_rows(bt), 1), 0)
    eo = jnp.where(rows < used, buf_ref[par], 0.0)
    hi = eo.astype(BF16)
    lo = (eo - hi.astype(F32)).astype(BF16)
    cm = cm_ref[...]
    cols = lax.broadcasted_iota(jnp.int32, (bt, _local_rows(bt)), 1).astype(F32)

    def pick(k):
        sel = jnp.where(cols == cm[:, k:k + 1], 1.0, 0.0).astype(BF16)
        return jnp.dot(sel, hi, preferred_element_type=F32) + jnp.dot(sel, lo, preferred_element_type=F32)

    y_ref[...] = x_ref[...] + (cm[:, 2:3] * pick(0) + cm[:, 3:4] * pick(1))


def _combine(seg, x, cm, eo, bt, blk0, tok0):
    n = x.shape[0]
    nb = n // bt
    return pl.pallas_call(
        functools.partial(_combine_kernel, bt=bt, blk0=blk0, nb=nb),
        grid_spec=pltpu.PrefetchScalarGridSpec(
            num_scalar_prefetch=3,
            grid=(nb,),
            in_specs=[pl.BlockSpec((bt, D_MODEL), lambda i, *_: (i, 0)),
                      pl.BlockSpec((bt, META_W), lambda i, *_: (tok0 // bt + i, 0)),
                      pl.BlockSpec(memory_space=pl.ANY)],
            out_specs=pl.BlockSpec((bt, D_MODEL), lambda i, *_: (i, 0)),
            scratch_shapes=[pltpu.VMEM((2, _local_rows(bt), D_MODEL), F32), pltpu.SemaphoreType.DMA((2,))],
        ),
        out_shape=jax.ShapeDtypeStruct((n, D_MODEL), F32),
        compiler_params=_params(1),
        name="moe_combine",
    )(*seg, x, cm, eo)


def _moe(xp, xs, g, w_router, wg, wu, wd, layer):
    g2 = g.reshape(1, D_MODEL)
    wr_pad = jnp.pad(w_router, ((0, 0), (0, 128 - N_EXPERTS)))
    hn, cm, ls_p, cnt_p = _router(xp, g2, wr_pad, ROW_TILE, MOE_BLOCK, 0)
    hn, cm, ls_s, cnt_s = _router(xs, g2, wr_pad, DEC_BATCH, DEC_BATCH, N_PROMPT // DEC_BATCH, prev=(hn, cm))

    nbp = N_PROMPT // MOE_BLOCK
    seg_cnt = jnp.concatenate([cnt_p[:, 0, :N_EXPERTS], cnt_s[:, 0, :N_EXPERTS]], axis=0).astype(jnp.int32)
    seg_n = (seg_cnt + (SEG_ALIGN - 1)) // SEG_ALIGN * SEG_ALIGN
    seg_src = jnp.cumsum(seg_n, axis=1) - seg_n
    slot0 = jnp.cumsum(seg_n, axis=0) - seg_n

    counts = seg_n.sum(axis=0)
    tiles_e = (counts + (MOE_TILE - 1)) // MOE_TILE
    tile_end = jnp.cumsum(tiles_e)
    tile_start = tile_end - tiles_e
    n_tiles = tile_end[-1]
    t = jnp.arange(MOE_MAX_TILES, dtype=jnp.int32)
    tc = jnp.minimum(t, n_tiles - 1)
    tile_expert = jnp.sum((tc[:, None] >= tile_end[None, :]).astype(jnp.int32), axis=1)
    rows_left = counts[tile_expert] - (tc - tile_start[tile_expert]) * MOE_TILE
    tile_rows = jnp.where(t < n_tiles, jnp.minimum(rows_left, MOE_TILE), 0).astype(jnp.int32)
    seg_dst = (tile_start * MOE_TILE)[None, :] + slot0

    cap = jnp.concatenate([jnp.full((nbp, 1), MOE_BLOCK, jnp.int32), jnp.full((1, 1), DEC_BATCH, jnp.int32)])
    seg_n = jnp.clip(seg_n, 0, cap)
    seg = (seg_n.reshape(-1), jnp.clip(seg_src, 0, 2 * cap + N_EXPERTS * SEG_ALIGN - seg_n).reshape(-1),
           jnp.clip(seg_dst, 0, N_SLOTS - MOE_BLOCK).reshape(-1))

    slots = _dispatch(seg, hn, ls_p, MOE_BLOCK, 0, 0)
    slots = _dispatch(seg, hn, ls_s, DEC_BATCH, nbp, N_PROMPT, prev=slots)
    eo = _experts(tile_expert.astype(jnp.int32), tc, tile_rows, slots, wg, wu, wd, layer)
    yp = _combine(seg, xp, cm, eo, MOE_BLOCK, 0, 0)
    ys = _combine(seg, xs, cm, eo, DEC_BATCH, nbp, N_PROMPT)
    return yp, ys


def kernel(x_prompt, x_sample, state_conv, cache_k, cache_v, norm_mix_g, norm_ffn_g, w_in_ab, conv_w, conv_b,
           conv_ln_g, conv_ln_b, q_norm_g, k_norm_g, attn_sinks, w_out_ab, w_gate_dense, w_up_dense, w_down_dense,
           w_in_c, c_ln_g, c_ln_b, w_spatial, b_spatial, w_out_c, w_router, w_gate_exp, w_up_exp, w_down_exp):
    xp = x_prompt.reshape(N_PROMPT, D_MODEL)
    xs = x_sample.reshape(DEC_BATCH, D_MODEL)
    head = jnp.arange(B_WIDTH) // HEAD_DIM
    seg = (head[:, None] == head[None, :]).astype(BF16)
    conv_p, conv_s, kp_out, vp_out, ks_out, vs_out, vc_out = [], [], [], [], [], [], []
    row = lambda a: a.reshape(1, -1)
    for layer in range(DEPTH):
        i = layer // 2
        g_mix = row(norm_mix_g[layer])
        g_ffn = row(norm_ffn_g[layer])
        if layer % 2 == 0:
            w_in = w_in_ab[i].astype(BF16)
            qg = row(jnp.tile(q_norm_g[i], N_HEADS))
            kg = row(jnp.tile(k_norm_g[i], N_KV_HEADS))
            a_p, q_p, k_p, v_p = _inproj(xp, g_mix, w_in, qg, kg, seg, ROW_TILE)
            a_s, q_s, k_s, v_s = _inproj(xs, g_mix, w_in, qg, kg, seg, DEC_BATCH)
            cb, lg, lb = row(conv_b[i]), row(conv_ln_g[i]), row(conv_ln_b[i])
            ca_p = _conv_prompt(a_p, conv_w[i], cb, lg, lb)
            ca_s = _conv_sample(a_s, jnp.transpose(state_conv[i], (1, 0, 2)), conv_w[i], cb, lg, lb)
            ob_p = _attn_prompt(q_p, k_p, v_p, attn_sinks[i])
            sinks_b = jnp.broadcast_to(attn_sinks[i].reshape(N_KV_HEADS, Q_GROUP, 1), (N_KV_HEADS, Q_GROUP, 128))
            ob_s = _attn_sample(q_s.reshape(DEC_BATCH, N_KV_HEADS, Q_GROUP, HEAD_DIM),
                                k_s.reshape(DEC_BATCH, N_KV_HEADS, HEAD_DIM),
                                v_s.reshape(DEC_BATCH, N_KV_HEADS, HEAD_DIM),
                                cache_k[i].reshape(DEC_BATCH, WINDOW, KV_WIDTH),
                                cache_v[i].reshape(DEC_BATCH, WINDOW, KV_WIDTH), sinks_b)
            ob_s = ob_s.reshape(DEC_BATCH, B_WIDTH)
            conv_p.append(a_p.reshape(BATCH, SEQ, A_CH)[:, SEQ - (CONV_W - 1):])
            conv_s.append(jnp.concatenate([state_conv[i][:, 1:], a_s[:, None, :]], axis=1))
            kp_out.append(k_p.reshape(BATCH, SEQ, N_KV_HEADS, HEAD_DIM)[:, SEQ - WINDOW:])
            vp_out.append(v_p.reshape(BATCH, SEQ, N_KV_HEADS, HEAD_DIM)[:, SEQ - WINDOW:])
            ks_out.append(jnp.concatenate(
                [cache_k[i][:, 1:], k_s.reshape(DEC_BATCH, 1, N_KV_HEADS, HEAD_DIM)], axis=1))
            vs_out.append(jnp.concatenate(
                [cache_v[i][:, 1:], v_s.reshape(DEC_BATCH, 1, N_KV_HEADS, HEAD_DIM)], axis=1))
            wo = w_out_ab[i].astype(BF16)
            wg = w_gate_dense[i].astype(BF16)
            wu = w_up_dense[i].astype(BF16)
            wd = w_down_dense[i].astype(BF16)
            xp = _outproj_ffn(xp, ca_p, ob_p, wo, g_ffn, wg, wu, wd, ROW_TILE)
            xs = _outproj_ffn(xs, ca_s, ob_s, wo, g_ffn, wg, wu, wd, DEC_BATCH)
        else:
            win = w_in_c[i].astype(BF16)
            wout = w_out_c[i].astype(BF16)
            lg, lb = row(c_ln_g[i]), row(c_ln_b[i])
            xp = _gmlp_prompt(xp, g_mix, win, lg, lb, w_spatial[i], b_spatial[i].T, wout, ROW_TILE)
            gd = C_WIDTH // C_GROUPS
            w0 = row(jnp.repeat(w_spatial[i][:, 0, 0], gd))
            b0 = row(jnp.repeat(b_spatial[i][:, 0], gd))
            xs, v_s = _gmlp_sample(xs, g_mix, win, lg, lb, w0, b0, wout)
            vc_out.append(v_s.reshape(DEC_BATCH, 1, C_WIDTH))
            xp, xs = _moe(xp, xs, norm_ffn_g[layer], w_router[i], w_gate_exp, w_up_exp, w_down_exp, i)
    return (xp.reshape(BATCH, SEQ, D_MODEL), xs.reshape(DEC_BATCH, 1, D_MODEL),
            jnp.stack(conv_p), jnp.stack(conv_s), jnp.stack(kp_out), jnp.stack(vp_out),
            jnp.stack(ks_out), jnp.stack(vs_out), jnp.stack(vc_out))
```

```python
import functools

import jax
import jax.numpy as jnp
from jax import lax
from jax.experimental import pallas as pl
from jax.experimental.pallas import tpu as pltpu

F32 = jnp.float32
BF16 = jnp.bfloat16

D_MODEL = 1024
BATCH = 8
SEQ = 2048
DEPTH = 4
DEC_BATCH = 128
A_CH = 512
CONV_W = 31
N_HEADS = 8
N_KV_HEADS = 2
HEAD_DIM = 64
Q_GROUP = 4
WINDOW = 128
B_WIDTH = 512
KV_WIDTH = 128
AB_IN = 2 * A_CH + B_WIDTH + 2 * KV_WIDTH
C_WIDTH = 1024
C_GROUPS = 8
C_CHUNK = 128
D_FF = 2816
N_EXPERTS = 8
D_FF_EXPERT = 3584
EPS = 1e-6
ATTN_SCALE = HEAD_DIM ** -0.5

N_PROMPT = BATCH * SEQ
N_TOK = N_PROMPT + DEC_BATCH
ROW_TILE = 1024
FF_SPLIT = 2
MOE_TILE = 1024
MOE_FF_TILE = 512
MOE_FF_SUB = 256
MOE_NF = D_FF_EXPERT // MOE_FF_TILE
MOE_BLOCK = 256
MOE_N_BLOCKS = N_PROMPT // MOE_BLOCK + 1
SEG_ALIGN = 8
MOE_MAX_TILES = (2 * N_TOK + MOE_N_BLOCKS * N_EXPERTS * (SEG_ALIGN - 1) + N_EXPERTS * (MOE_TILE - 1)) // MOE_TILE
N_SLOTS = MOE_MAX_TILES * MOE_TILE
VMEM_LIMIT = 56 * 1024 * 1024


def _params(n_axes, vmem=VMEM_LIMIT):
    return pltpu.CompilerParams(dimension_semantics=("arbitrary",) * n_axes, vmem_limit_bytes=vmem)


def _rms(x, g):
    return x * lax.rsqrt(jnp.mean(x * x, axis=-1, keepdims=True) + EPS) * g


def _layer_norm(x, g, b):
    mu = jnp.mean(x, axis=-1, keepdims=True)
    xc = x - mu
    var = jnp.mean(xc * xc, axis=-1, keepdims=True)
    return xc * lax.rsqrt(var + EPS) * g + b


def _const_spec(shape):
    return pl.BlockSpec(shape, lambda *_: (0,) * len(shape))


def _inproj_kernel(x_ref, g_ref, w_ref, qg_ref, kg_ref, seg_ref, a_ref, q_ref, k_ref, v_ref):
    h = _rms(x_ref[...], g_ref[...]).astype(BF16)
    z = jnp.dot(h, w_ref[...], preferred_element_type=F32)
    a_ref[...] = z[:, :A_CH] * jax.nn.sigmoid(z[:, A_CH:2 * A_CH])
    seg = seg_ref[...]

    def head_norm(t, gain, seg_t):
        sq = t * t
        hi = sq.astype(BF16)
        lo = (sq - hi.astype(F32)).astype(BF16)
        ms = (jnp.dot(hi, seg_t, preferred_element_type=F32)
              + jnp.dot(lo, seg_t, preferred_element_type=F32)) * (1.0 / HEAD_DIM)
        return t * lax.rsqrt(ms + EPS) * gain

    o = 2 * A_CH
    q = head_norm(z[:, o:o + B_WIDTH], qg_ref[...], seg)
    q_ref[...] = (q * ATTN_SCALE).astype(BF16)
    o += B_WIDTH
    k_ref[...] = head_norm(z[:, o:o + KV_WIDTH], kg_ref[...], seg[:KV_WIDTH, :KV_WIDTH])
    o += KV_WIDTH
    v_ref[...] = z[:, o:o + KV_WIDTH]


def _inproj(x, g, w_bf, qg, kg, seg, tm):
    n = x.shape[0]
    row = lambda w: pl.BlockSpec((tm, w), lambda i: (i, 0))
    return pl.pallas_call(
        _inproj_kernel,
        grid=(n // tm,),
        in_specs=[row(D_MODEL), _const_spec((1, D_MODEL)), _const_spec((D_MODEL, AB_IN)),
                  _const_spec((1, B_WIDTH)), _const_spec((1, KV_WIDTH)), _const_spec((B_WIDTH, B_WIDTH))],
        out_specs=[row(A_CH), row(B_WIDTH), row(KV_WIDTH), row(KV_WIDTH)],
        out_shape=[jax.ShapeDtypeStruct((n, A_CH), F32), jax.ShapeDtypeStruct((n, B_WIDTH), BF16),
                   jax.ShapeDtypeStruct((n, KV_WIDTH), F32), jax.ShapeDtypeStruct((n, KV_WIDTH), F32)],
        compiler_params=_params(1),
        name="inproj_ab",
    )(x, g, w_bf, qg, kg, seg)


CONV_PAD = 32
CONV_ROWS = 128


def _conv_tail(c, lg, lb):
    y = _layer_norm(c, lg, lb)
    return y * jax.nn.sigmoid(y)


def _conv_prompt_kernel(a_ref, w_ref, b_ref, lg_ref, lb_ref, o_ref, pad_ref, sh_ref, c_ref):
    pad_ref[0:CONV_PAD, :] = jnp.zeros((CONV_PAD, A_CH), F32)
    pad_ref[CONV_PAD:, :] = a_ref[...]
    win = CONV_ROWS + CONV_PAD

    def body(r, carry):
        t0 = pl.multiple_of(r * CONV_ROWS, CONV_ROWS)
        sh_ref[0] = pad_ref[pl.ds(t0, win), :]
        for s in range(1, 8):
            sh_ref[s, 0:win - 8, :] = sh_ref[0, s:s + win - 8, :]
        first = CONV_PAD - (CONV_W - 1)
        for c in range(A_CH // 128):
            lanes = slice(c * 128, (c + 1) * 128)
            acc = jnp.zeros((CONV_ROWS, 128), F32)
            for s in range(8):
                taps = [k for k in range(CONV_W) if (first + k) % 8 == s]
                rows = sh_ref[s, 0:(win if s == 0 else win - 8), lanes]
                for k in taps:
                    base = first + k - s
                    acc = acc + rows[base:base + CONV_ROWS] * w_ref[k:k + 1, lanes]
            c_ref[:, lanes] = acc
        y = _conv_tail(c_ref[...] + b_ref[...], lg_ref[...], lb_ref[...])
        o_ref[pl.ds(t0, CONV_ROWS), :] = y.astype(BF16)
        return carry

    lax.fori_loop(0, SEQ // CONV_ROWS, body, 0)


def _conv_prompt(a, w, b, lg, lb):
    return pl.pallas_call(
        _conv_prompt_kernel,
        grid=(BATCH,),
        in_specs=[pl.BlockSpec((SEQ, A_CH), lambda i: (i, 0)), _const_spec((CONV_W, A_CH)),
                  _const_spec((1, A_CH)), _const_spec((1, A_CH)), _const_spec((1, A_CH))],
        out_specs=pl.BlockSpec((SEQ, A_CH), lambda i: (i, 0)),
        out_shape=jax.ShapeDtypeStruct((N_PROMPT, A_CH), BF16),
        scratch_shapes=[pltpu.VMEM((CONV_PAD + SEQ, A_CH), F32), pltpu.VMEM((8, CONV_ROWS + CONV_PAD, A_CH), F32),
                        pltpu.VMEM((CONV_ROWS, A_CH), F32)],
        compiler_params=_params(1),
        name="conv_prompt",
    )(a, w, b, lg, lb)


def _conv_sample_kernel(a_ref, st_ref, w_ref, b_ref, lg_ref, lb_ref, o_ref):
    acc = a_ref[...] * w_ref[CONV_W - 1:CONV_W, :]
    acc = acc + jnp.sum(st_ref[...] * w_ref[0:CONV_W - 1, :][None], axis=1)
    o_ref[...] = _conv_tail(acc + b_ref[...], lg_ref[...], lb_ref[...]).astype(BF16)


def _conv_sample(a, st_t, w, b, lg, lb):
    return pl.pallas_call(
        _conv_sample_kernel,
        grid=(1,),
        in_specs=[_const_spec((DEC_BATCH, A_CH)), _const_spec((DEC_BATCH, CONV_W - 1, A_CH)),
                  _const_spec((CONV_W, A_CH)), _const_spec((1, A_CH)), _const_spec((1, A_CH)),
                  _const_spec((1, A_CH))],
        out_specs=_const_spec((DEC_BATCH, A_CH)),
        out_shape=jax.ShapeDtypeStruct((DEC_BATCH, A_CH), BF16),
        compiler_params=_params(1),
        name="conv_sample",
    )(a, st_t, w, b, lg, lb)


def _alibi_slope(h):
    return 2.0 ** (-8.0 * (h + 1) / N_HEADS)


ATTN_P_WINDOWS = 4


def _attn_prompt_kernel(sink_ref, q_ref, kp_ref, kc_ref, vp_ref, vc_ref, o_ref):
    step = pl.program_id(1)
    kk = jnp.concatenate([kp_ref[...], kc_ref[...]], axis=0).astype(BF16)
    vv = jnp.concatenate([vp_ref[...], vc_ref[...]], axis=0).astype(BF16)
    qi = lax.broadcasted_iota(jnp.int32, (WINDOW, 2 * WINDOW), 0)
    kj = lax.broadcasted_iota(jnp.int32, (WINDOW, 2 * WINDOW), 1)
    dist_i = WINDOW + qi - kj
    in_window = (dist_i >= 0) & (dist_i < WINDOW)
    lo = jnp.where(step > 0, 0, WINDOW)
    dist = dist_i.astype(F32)
    for w in range(ATTN_P_WINDOWS):
        valid = in_window & (kj >= lo) if w == 0 else in_window
        rows = slice(w * WINDOW, (w + 1) * WINDOW)
        keys = slice(w * WINDOW, (w + 2) * WINDOW)
        outs = []
        for h in range(N_HEADS):
            kv = h // Q_GROUP
            lanes = slice(kv * HEAD_DIM, (kv + 1) * HEAD_DIM)
            qh = q_ref[rows, h * HEAD_DIM:(h + 1) * HEAD_DIM]
            s = lax.dot_general(qh, kk[keys, lanes], (((1,), (1,)), ((), ())), preferred_element_type=F32)
            s = jnp.where(valid, s - _alibi_slope(h) * dist, -jnp.inf)
            sink = sink_ref[h]
            m = jnp.maximum(jnp.max(s, axis=-1, keepdims=True), sink)
            p = jnp.exp(s - m)
            denom = jnp.sum(p, axis=-1, keepdims=True) + jnp.exp(sink - m)
            p = (p / denom).astype(BF16)
            outs.append(jnp.dot(p, vv[keys, lanes], preferred_element_type=F32))
        o_ref[rows, :] = jnp.concatenate(outs, axis=-1).astype(BF16)


def _attn_prompt(q, k, v, sinks):
    rows = ATTN_P_WINDOWS * WINDOW
    ns = SEQ // rows
    own = lambda b, j: (b * ns + j, 0)
    prev = lambda b, j: (b * (SEQ // WINDOW) + jnp.maximum(j * ATTN_P_WINDOWS - 1, 0), 0)
    return pl.pallas_call(
        _attn_prompt_kernel,
        grid=(BATCH, ns),
        in_specs=[pl.BlockSpec(memory_space=pltpu.SMEM),
                  pl.BlockSpec((rows, B_WIDTH), own),
                  pl.BlockSpec((WINDOW, KV_WIDTH), prev), pl.BlockSpec((rows, KV_WIDTH), own),
                  pl.BlockSpec((WINDOW, KV_WIDTH), prev), pl.BlockSpec((rows, KV_WIDTH), own)],
        out_specs=pl.BlockSpec((rows, B_WIDTH), own),
        out_shape=jax.ShapeDtypeStruct((N_PROMPT, B_WIDTH), BF16),
        compiler_params=_params(2),
        name="attn_prompt",
    )(sinks, q, k, k, v, v)


ATTN_S_BLK = 32


def _attn_sample_kernel(sink_ref, q_ref, kn_ref, vn_ref, kc_ref, vc_ref, o_ref):
    kc = kc_ref[...].astype(BF16)
    vc = vc_ref[...].astype(BF16)
    j = lax.broadcasted_iota(jnp.int32, (1, 1, WINDOW), 2)
    dist = (WINDOW - j).astype(F32)
    g_idx = lax.broadcasted_iota(jnp.int32, (1, Q_GROUP, 1), 1)
    for kv in range(N_KV_HEADS):
        lanes = slice(kv * HEAD_DIM, (kv + 1) * HEAD_DIM)
        qg = q_ref[:, kv]
        slope = jnp.exp2(-(g_idx + (kv * Q_GROUP + 1)).astype(F32) * (8.0 / N_HEADS))
        s = jnp.einsum('ngd,njd->ngj', qg, kc[:, :, lanes], preferred_element_type=F32)
        s = jnp.where(j >= 1, s - slope * dist, -jnp.inf)
        kn = kn_ref[:, kv].astype(BF16).astype(F32)
        s_new = jnp.sum(qg.astype(F32) * kn[:, None, :], axis=-1, keepdims=True)
        sink = sink_ref[kv][:, :1][None]
        m = jnp.maximum(jnp.maximum(jnp.max(s, axis=-1, keepdims=True), s_new), sink)
        p = jnp.exp(s - m)
        p_new = jnp.exp(s_new - m)
        denom = jnp.sum(p, axis=-1, keepdims=True) + p_new + jnp.exp(sink - m)
        o = jnp.einsum('ngj,njd->ngd', (p / denom).astype(BF16), vc[:, :, lanes], preferred_element_type=F32)
        vn = vn_ref[:, kv].astype(BF16).astype(F32)
        o = o + (p_new / denom).astype(BF16).astype(F32) * vn[:, None, :]
        o_ref[:, kv] = o.astype(BF16)


def _attn_sample(q4, kn3, vn3, kc, vc, sinks_b):
    nb = ATTN_S_BLK
    return pl.pallas_call(
        _attn_sample_kernel,
        grid=(DEC_BATCH // nb,),
        in_specs=[_const_spec((N_KV_HEADS, Q_GROUP, 128)),
                  pl.BlockSpec((nb, N_KV_HEADS, Q_GROUP, HEAD_DIM), lambda i: (i, 0, 0, 0)),
                  pl.BlockSpec((nb, N_KV_HEADS, HEAD_DIM), lambda i: (i, 0, 0)),
                  pl.BlockSpec((nb, N_KV_HEADS, HEAD_DIM), lambda i: (i, 0, 0)),
                  pl.BlockSpec((nb, WINDOW, KV_WIDTH), lambda i: (i, 0, 0)),
                  pl.BlockSpec((nb, WINDOW, KV_WIDTH), lambda i: (i, 0, 0))],
        out_specs=pl.BlockSpec((nb, N_KV_HEADS, Q_GROUP, HEAD_DIM), lambda i: (i, 0, 0, 0)),
        out_shape=jax.ShapeDtypeStruct((DEC_BATCH, N_KV_HEADS, Q_GROUP, HEAD_DIM), BF16),
        compiler_params=_params(1),
        name="attn_sample",
    )(sinks_b, q4, kn3, vn3, kc, vc)


def _outproj_ffn_kernel(x_ref, ca_ref, ob_ref, g_ref, wo_hbm, wg_hbm, wu_hbm, wd_hbm, o_ref,
                        wo_ref, wg_ref, wu_ref, wd_ref):
    @pl.when(pl.program_id(0) == 0)
    def _():
        pltpu.sync_copy(wo_hbm, wo_ref)
        pltpu.sync_copy(wg_hbm, wg_ref)
        pltpu.sync_copy(wu_hbm, wu_ref)
        pltpu.sync_copy(wd_hbm, wd_ref)

    o_ref[...] = (x_ref[...]
                  + jnp.dot(ca_ref[...], wo_ref[:A_CH, :], preferred_element_type=F32)
                  + jnp.dot(ob_ref[...], wo_ref[A_CH:, :], preferred_element_type=F32))
    h = _rms(o_ref[...], g_ref[...]).astype(BF16)
    ff = D_FF // FF_SPLIT
    for c in range(FF_SPLIT):
        cols = slice(c * ff, (c + 1) * ff)
        gate = jnp.dot(h, wg_ref[:, cols], preferred_element_type=F32)
        up = jnp.dot(h, wu_ref[:, cols], preferred_element_type=F32)
        act = (gate * jax.nn.sigmoid(gate) * up).astype(BF16)
        o_ref[...] += jnp.dot(act, wd_ref[cols, :], preferred_element_type=F32)


def _outproj_ffn(x, ca, ob, wo, g, wg, wu, wd, tm):
    n = x.shape[0]
    row = lambda w: pl.BlockSpec((tm, w), lambda i: (i, 0))
    hbm = pl.BlockSpec(memory_space=pl.ANY)
    return pl.pallas_call(
        _outproj_ffn_kernel,
        grid=(n // tm,),
        in_specs=[row(D_MODEL), row(A_CH), row(B_WIDTH), _const_spec((1, D_MODEL)), hbm, hbm, hbm, hbm],
        out_specs=row(D_MODEL),
        out_shape=jax.ShapeDtypeStruct((n, D_MODEL), F32),
        scratch_shapes=[pltpu.VMEM((D_MODEL, D_MODEL), BF16), pltpu.VMEM((D_MODEL, D_FF), BF16),
                        pltpu.VMEM((D_MODEL, D_FF), BF16), pltpu.VMEM((D_FF, D_MODEL), BF16)],
        compiler_params=_params(1),
        name="outproj_ffn",
    )(x, ca, ob, g, wo, wg, wu, wd)


def _gmlp_prompt_kernel(x_ref, g_ref, win_ref, lg_ref, lb_ref, ws_ref, bs_ref, wout_ref, o_ref, gate_ref):
    x = x_ref[...]
    h = _rms(x, g_ref[...]).astype(BF16)
    z = jax.nn.gelu(jnp.dot(h, win_ref[...], preferred_element_type=F32))
    u = z[:, :C_WIDTH]
    v = _layer_norm(z[:, C_WIDTH:], lg_ref[...], lb_ref[...]).astype(BF16)
    ri = lax.broadcasted_iota(jnp.int32, (C_CHUNK, C_CHUNK), 0)
    ci = lax.broadcasted_iota(jnp.int32, (C_CHUNK, C_CHUNK), 1)
    tm = x.shape[0]
    gd = C_WIDTH // C_GROUPS
    for grp in range(C_GROUPS):
        w = jnp.where(ri >= ci, ws_ref[grp], 0.0).astype(BF16)
        b = bs_ref[:, grp:grp + 1]
        lanes = slice(grp * gd, (grp + 1) * gd)
        for c in range(tm // C_CHUNK):
            rows = slice(c * C_CHUNK, (c + 1) * C_CHUNK)
            mixed = jnp.dot(w, v[rows, lanes], preferred_element_type=F32) + b
            gate_ref[rows, lanes] = (u[rows, lanes] * mixed).astype(BF16)
    o_ref[...] = x + jnp.dot(gate_ref[...], wout_ref[...], preferred_element_type=F32)


def _gmlp_prompt(x, g, win, lg, lb, ws, bs_t, wout, tm):
    n = x.shape[0]
    row = lambda w: pl.BlockSpec((tm, w), lambda i: (i, 0))
    return pl.pallas_call(
        _gmlp_prompt_kernel,
        grid=(n // tm,),
        in_specs=[row(D_MODEL), _const_spec((1, D_MODEL)), _const_spec((D_MODEL, 2 * C_WIDTH)),
                  _const_spec((1, C_WIDTH)), _const_spec((1, C_WIDTH)),
                  _const_spec((C_GROUPS, C_CHUNK, C_CHUNK)), _const_spec((C_CHUNK, C_GROUPS)),
                  _const_spec((C_WIDTH, D_MODEL))],
        out_specs=row(D_MODEL),
        out_shape=jax.ShapeDtypeStruct((n, D_MODEL), F32),
        scratch_shapes=[pltpu.VMEM((tm, C_WIDTH), BF16)],
        compiler_params=_params(1),
        name="gmlp_prompt",
    )(x, g, win, lg, lb, ws, bs_t, wout)


def _gmlp_sample_kernel(x_ref, g_ref, win_ref, lg_ref, lb_ref, w0_ref, b0_ref, wout_ref, o_ref, v_ref):
    x = x_ref[...]
    h = _rms(x, g_ref[...]).astype(BF16)
    z = jax.nn.gelu(jnp.dot(h, win_ref[...], preferred_element_type=F32))
    v = _layer_norm(z[:, C_WIDTH:], lg_ref[...], lb_ref[...])
    v_ref[...] = v
    gated = z[:, :C_WIDTH] * (w0_ref[...] * v + b0_ref[...])
    o_ref[...] = x + jnp.dot(gated.astype(BF16), wout_ref[...], preferred_element_type=F32)


def _gmlp_sample(x, g, win, lg, lb, w0, b0, wout):
    n = x.shape[0]
    return pl.pallas_call(
        _gmlp_sample_kernel,
        grid=(1,),
        in_specs=[_const_spec((n, D_MODEL)), _const_spec((1, D_MODEL)), _const_spec((D_MODEL, 2 * C_WIDTH)),
                  _const_spec((1, C_WIDTH)), _const_spec((1, C_WIDTH)), _const_spec((1, C_WIDTH)),
                  _const_spec((1, C_WIDTH)), _const_spec((C_WIDTH, D_MODEL))],
        out_specs=[_const_spec((n, D_MODEL)), _const_spec((n, C_WIDTH))],
        out_shape=[jax.ShapeDtypeStruct((n, D_MODEL), F32), jax.ShapeDtypeStruct((n, C_WIDTH), F32)],
        compiler_params=_params(1),
        name="gmlp_sample",
    )(x, g, win, lg, lb, w0, b0, wout)


META_W = 8


def _split_bf16(a):
    hi = a.astype(BF16)
    return hi, (a - hi.astype(F32)).astype(BF16)


def _router_kernel(x_ref, g_ref, wr_ref, *rest, bt):
    hn_ref, cm_ref, lst_ref, cnt_ref = rest[-4:]
    h = _rms(x_ref[...], g_ref[...])
    tm = h.shape[0]
    h_hi, h_lo = _split_bf16(h)
    w_hi, w_lo = _split_bf16(wr_ref[...])
    hn_ref[...] = h_hi
    logits = (jnp.dot(h_hi, w_hi, preferred_element_type=F32) + jnp.dot(h_lo, w_hi, preferred_element_type=F32)
              + jnp.dot(h_hi, w_lo, preferred_element_type=F32))
    lane = lax.broadcasted_iota(jnp.int32, (bt, 128), 1).astype(F32)
    neg = -jnp.inf
    ri = lax.broadcasted_iota(jnp.int32, (bt, bt), 0)
    ci = lax.broadcasted_iota(jnp.int32, (bt, bt), 1)
    earlier = jnp.where(ri > ci, 1.0, 0.0).astype(BF16)
    ei = lax.broadcasted_iota(jnp.int32, (128, 128), 0)
    ej = lax.broadcasted_iota(jnp.int32, (128, 128), 1)
    lower_experts = jnp.where(ei < ej, 1.0, 0.0).astype(BF16)
    for j in range(tm // bt):
        rows = slice(j * bt, (j + 1) * bt)
        lg = jnp.where(lane < N_EXPERTS, logits[rows], neg)
        m1 = jnp.max(lg, axis=-1, keepdims=True)
        i1 = jnp.min(jnp.where(lg == m1, lane, 128.0), axis=-1, keepdims=True)
        lg2 = jnp.where(lane == i1, neg, lg)
        m2 = jnp.max(lg2, axis=-1, keepdims=True)
        i2 = jnp.min(jnp.where(lg2 == m2, lane, 128.0), axis=-1, keepdims=True)
        e = jnp.exp(m2 - m1)
        g1 = 1.0 / (1.0 + e)
        g2 = e / (1.0 + e)
        sel1 = lane == i1
        sel2 = lane == i2
        oh = jnp.where(sel1, 1.0, jnp.where(sel2, 1.0, 0.0))
        before = jnp.dot(earlier, oh.astype(BF16), preferred_element_type=F32)
        cnt = jnp.sum(oh, axis=0, keepdims=True)
        padded = jnp.floor((cnt + (SEG_ALIGN - 1)) * (1.0 / SEG_ALIGN)) * SEG_ALIGN
        start = jnp.dot(jnp.broadcast_to(padded, (8, 128)).astype(BF16), lower_experts,
                        preferred_element_type=F32)[0:1]
        slot = before + start
        l1 = jnp.sum(jnp.where(sel1, slot, 0.0), axis=-1, keepdims=True)
        l2 = jnp.sum(jnp.where(sel2, slot, 0.0), axis=-1, keepdims=True)
        rec = jnp.where(lane == 0, l1, jnp.where(lane == 1, l2, jnp.where(lane == 2, g1, jnp.where(lane == 3, g2, 0.0))))
        cm_ref[rows, :] = rec[:, :META_W]
        lst_ref[j] = rec.T[0:2, :].astype(jnp.int32)
        cnt_ref[j] = cnt


def _router(x, g, wr_pad, tm, bt, row_blk0, prev=None):
    n = x.shape[0]
    nblk = tm // bt
    in_specs = [pl.BlockSpec((tm, D_MODEL), lambda i: (i, 0)), _const_spec((1, D_MODEL)),
                _const_spec((D_MODEL, 128))]
    args = [x, g, wr_pad]
    aliases = {}
    if prev is not None:
        in_specs += [pl.BlockSpec(memory_space=pl.ANY), pl.BlockSpec(memory_space=pl.ANY)]
        args += list(prev)
        aliases = {3: 0, 4: 1}
    return pl.pallas_call(
        functools.partial(_router_kernel, bt=bt),
        grid=(n // tm,),
        in_specs=in_specs,
        out_specs=[pl.BlockSpec((tm, D_MODEL), lambda i: (row_blk0 + i, 0)),
                   pl.BlockSpec((tm, META_W), lambda i: (row_blk0 + i, 0)),
                   pl.BlockSpec((nblk, 2, bt), lambda i: (i, 0, 0)),
                   pl.BlockSpec((nblk, 1, 128), lambda i: (i, 0, 0))],
        out_shape=[jax.ShapeDtypeStruct((N_TOK, D_MODEL), BF16), jax.ShapeDtypeStruct((N_TOK, META_W), F32),
                   jax.ShapeDtypeStruct((n // bt, 2, bt), jnp.int32), jax.ShapeDtypeStruct((n // bt, 1, 128), F32)],
        input_output_aliases=aliases,
        compiler_params=_params(1),
        name="moe_router",
    )(*args)


def _local_rows(bt):
    return 2 * bt + N_EXPERTS * SEG_ALIGN


def _segment_copies(seg_n, seg_src, seg_dst, blk, local_ref, slot_ref, sem, bt, to_slots, action):
    def per_expert(e, c):
        s = blk * N_EXPERTS + e
        n, src, dst = seg_n[s], seg_src[s], seg_dst[s]
        off = jnp.int32(0)
        p = bt
        while p >= SEG_ALIGN:
            hit = (n & p) != 0

            @pl.when(hit)
            def _(off=off, p=p):
                a = local_ref.at[pl.ds(pl.multiple_of(src + off, SEG_ALIGN), p)]
                b = slot_ref.at[pl.ds(pl.multiple_of(dst + off, SEG_ALIGN), p)]
                action(pltpu.make_async_copy(a, b, sem) if to_slots else pltpu.make_async_copy(b, a, sem))

            off = off + (n & p)
            p //= 2
        return c

    lax.fori_loop(0, N_EXPERTS, per_expert, 0)


def _dispatch_kernel(seg_n, seg_src, seg_dst, hn_ref, ls_ref, *rest, bt, blk0, nb):
    slots_ref, buf_ref, sem = rest[-3:]
    i = pl.program_id(0)
    par = i % 2
    start = lambda cp: cp.start()
    wait = lambda cp: cp.wait()
    copies = lambda blk, p, action: _segment_copies(seg_n, seg_src, seg_dst, blk, buf_ref.at[p], slots_ref,
                                                     sem.at[p], bt, True, action)

    @pl.when(i >= 2)
    def _():
        copies(blk0 + i - 2, par, wait)

    rows = lax.broadcasted_iota(jnp.int32, (_local_rows(bt), bt), 0)
    ls = ls_ref[0]
    onehot = jnp.where(rows == ls[0:1, :], 1.0, jnp.where(rows == ls[1:2, :], 1.0, 0.0)).astype(BF16)
    buf_ref[par] = jnp.dot(onehot, hn_ref[...], preferred_element_type=F32)
    copies(blk0 + i, par, start)

    @pl.when(i == nb - 1)
    def _():
        if nb >= 2:
            copies(blk0 + i - 1, 1 - par, wait)
        copies(blk0 + i, par, wait)


def _dispatch(seg, hn, ls_t, bt, blk0, tok0, prev=None):
    nb = ls_t.shape[0]
    in_specs = [pl.BlockSpec((bt, D_MODEL), lambda i, *_: (tok0 // bt + i, 0)),
                pl.BlockSpec((1, 2, bt), lambda i, *_: (i, 0, 0))]
    args = [hn, ls_t]
    aliases = {}
    if prev is not None:
        in_specs.append(pl.BlockSpec(memory_space=pl.ANY))
        args.append(prev)
        aliases = {5: 0}
    return pl.pallas_call(
        functools.partial(_dispatch_kernel, bt=bt, blk0=blk0, nb=nb),
        grid_spec=pltpu.PrefetchScalarGridSpec(
            num_scalar_prefetch=3,
            grid=(nb,),
            in_specs=in_specs,
            out_specs=pl.BlockSpec(memory_space=pl.ANY),
            scratch_shapes=[pltpu.VMEM((2, _local_rows(bt), D_MODEL), F32), pltpu.SemaphoreType.DMA((2,))],
        ),
        out_shape=jax.ShapeDtypeStruct((N_SLOTS, D_MODEL), F32),
        input_output_aliases=aliases,
        compiler_params=_params(1),
        name="moe_dispatch",
    )(*seg, *args)


def _experts_kernel(te_ref, tb_ref, nv_ref, x_ref, wg_ref, wu_ref, wd_ref, o_ref, xb_ref):
    t = pl.program_id(0)
    f = pl.program_id(1)
    nv = nv_ref[t]

    @pl.when(jnp.logical_and(nv > 0, f == 0))
    def _():
        rows = lax.broadcasted_iota(jnp.int32, (MOE_TILE, 1), 0)
        xb_ref[...] = jnp.where(rows < nv, x_ref[...], 0.0).astype(BF16)
        o_ref[...] = jnp.zeros_like(o_ref)

    @pl.when(nv > 0)
    def _():
        xb = xb_ref[...]
        for c in range(MOE_FF_TILE // MOE_FF_SUB):
            cols = slice(c * MOE_FF_SUB, (c + 1) * MOE_FF_SUB)
            gate = jnp.dot(xb, wg_ref[:, cols].astype(BF16), preferred_element_type=F32)
            up = jnp.dot(xb, wu_ref[:, cols].astype(BF16), preferred_element_type=F32)
            act = (gate * jax.nn.sigmoid(gate) * up).astype(BF16)
            o_ref[...] += jnp.dot(act, wd_ref[cols, :].astype(BF16), preferred_element_type=F32)


def _experts(tile_expert, tile_blk, tile_rows, xs, wg, wu, wd, layer):
    def chunk(t, f, nv):
        return jnp.where(nv[t] > 0, f, MOE_NF - 1)
    up_spec = pl.BlockSpec((None, None, D_MODEL, MOE_FF_TILE),
                           lambda t, f, te, tb, nv: (layer, te[t], 0, chunk(t, f, nv)))
    down_spec = pl.BlockSpec((None, None, MOE_FF_TILE, D_MODEL),
                             lambda t, f, te, tb, nv: (layer, te[t], chunk(t, f, nv), 0))
    return pl.pallas_call(
        _experts_kernel,
        grid_spec=pltpu.PrefetchScalarGridSpec(
            num_scalar_prefetch=3,
            grid=(MOE_MAX_TILES, MOE_NF),
            in_specs=[pl.BlockSpec((MOE_TILE, D_MODEL), lambda t, f, te, tb, nv: (tb[t], 0)),
                      up_spec, up_spec, down_spec],
            out_specs=pl.BlockSpec((MOE_TILE, D_MODEL), lambda t, f, te, tb, nv: (tb[t], 0)),
            scratch_shapes=[pltpu.VMEM((MOE_TILE, D_MODEL), BF16)],
        ),
        out_shape=jax.ShapeDtypeStruct((N_SLOTS, D_MODEL), F32),
        compiler_params=_params(2),
        name="moe_experts",
    )(tile_expert, tile_blk, tile_rows, xs, wg, wu, wd)


def _combine_kernel(seg_n, seg_src, seg_dst, x_ref, cm_ref, eo_ref, y_ref, buf_ref, sem, *, bt, blk0, nb):
    i = pl.program_id(0)
    par = i % 2
    start = lambda cp: cp.start()
    wait = lambda cp: cp.wait()
    copies = lambda blk, p, action: _segment_copies(seg_n, seg_src, seg_dst, blk, buf_ref.at[p], eo_ref,
                                                     sem.at[p], bt, False, action)

    @pl.when(i == 0)
    def _():
        copies(blk0, 0, start)

    @pl.when(i + 1 < nb)
    def _():
        copies(blk0 + i + 1, 1 - par, start)

    copies(blk0 + i, par, wait)
    last = (blk0 + i) * N_EXPERTS + (N_EXPERTS - 1)
    used = seg_src[last] + seg_n[last]
    rows = lax.broadcasted_iota(jnp.int32, (_local_rows(bt), 1), 0)
    e_hi, e_lo = _split_bf16(jnp.where(rows < used, buf_ref[par], 0.0))
    cm = cm_ref[...]
    cols = lax.broadcasted_iota(jnp.int32, (bt, _local_rows(bt)), 1).astype(F32)
    comb = jnp.where(cols == cm[:, 0:1], cm[:, 2:3], jnp.where(cols == cm[:, 1:2], cm[:, 3:4], 0.0))
    c_hi, c_lo = _split_bf16(comb)
    y_ref[...] = x_ref[...] + jnp.dot(c_hi, e_hi, preferred_element_type=F32)
    y_ref[...] += jnp.dot(c_lo, e_hi, preferred_element_type=F32)
    y_ref[...] += jnp.dot(c_hi, e_lo, preferred_element_type=F32)


def _combine(seg, x, cm, eo, bt, blk0, tok0):
    n = x.shape[0]
    nb = n // bt
    return pl.pallas_call(
        functools.partial(_combine_kernel, bt=bt, blk0=blk0, nb=nb),
        grid_spec=pltpu.PrefetchScalarGridSpec(
            num_scalar_prefetch=3,
            grid=(nb,),
            in_specs=[pl.BlockSpec((bt, D_MODEL), lambda i, *_: (i, 0)),
                      pl.BlockSpec((bt, META_W), lambda i, *_: (tok0 // bt + i, 0)),
                      pl.BlockSpec(memory_space=pl.ANY)],
            out_specs=pl.BlockSpec((bt, D_MODEL), lambda i, *_: (i, 0)),
            scratch_shapes=[pltpu.VMEM((2, _local_rows(bt), D_MODEL), F32), pltpu.SemaphoreType.DMA((2,))],
        ),
        out_shape=jax.ShapeDtypeStruct((n, D_MODEL), F32),
        compiler_params=_params(1),
        name="moe_combine",
    )(*seg, x, cm, eo)


def _moe(xp, xs, g, w_router, wg, wu, wd, layer):
    g2 = g.reshape(1, D_MODEL)
    wr_pad = jnp.pad(w_router, ((0, 0), (0, 128 - N_EXPERTS)))
    hn, cm, ls_p, cnt_p = _router(xp, g2, wr_pad, ROW_TILE, MOE_BLOCK, 0)
    hn, cm, ls_s, cnt_s = _router(xs, g2, wr_pad, DEC_BATCH, DEC_BATCH, N_PROMPT // DEC_BATCH, prev=(hn, cm))

    nbp = N_PROMPT // MOE_BLOCK
    seg_cnt = jnp.concatenate([cnt_p[:, 0, :N_EXPERTS], cnt_s[:, 0, :N_EXPERTS]], axis=0).astype(jnp.int32)
    seg_n = (seg_cnt + (SEG_ALIGN - 1)) // SEG_ALIGN * SEG_ALIGN
    seg_src = jnp.cumsum(seg_n, axis=1) - seg_n
    slot0 = jnp.cumsum(seg_n, axis=0) - seg_n

    counts = seg_n.sum(axis=0)
    tiles_e = (counts + (MOE_TILE - 1)) // MOE_TILE
    tile_end = jnp.cumsum(tiles_e)
    tile_start = tile_end - tiles_e
    n_tiles = tile_end[-1]
    t = jnp.arange(MOE_MAX_TILES, dtype=jnp.int32)
    tc = jnp.minimum(t, n_tiles - 1)
    tile_expert = jnp.sum((tc[:, None] >= tile_end[None, :]).astype(jnp.int32), axis=1)
    rows_left = counts[tile_expert] - (tc - tile_start[tile_expert]) * MOE_TILE
    tile_rows = jnp.where(t < n_tiles, jnp.minimum(rows_left, MOE_TILE), 0).astype(jnp.int32)
    seg_dst = (tile_start * MOE_TILE)[None, :] + slot0

    cap = jnp.concatenate([jnp.full((nbp, 1), MOE_BLOCK, jnp.int32), jnp.full((1, 1), DEC_BATCH, jnp.int32)])
    seg_n = jnp.clip(seg_n, 0, cap)
    seg = (seg_n.reshape(-1), jnp.clip(seg_src, 0, 2 * cap + N_EXPERTS * SEG_ALIGN - seg_n).reshape(-1),
           jnp.clip(seg_dst, 0, N_SLOTS - MOE_BLOCK).reshape(-1))

    slots = _dispatch(seg, hn, ls_p, MOE_BLOCK, 0, 0)
    slots = _dispatch(seg, hn, ls_s, DEC_BATCH, nbp, N_PROMPT, prev=slots)
    eo = _experts(tile_expert.astype(jnp.int32), tc, tile_rows, slots, wg, wu, wd, layer)
    yp = _combine(seg, xp, cm, eo, MOE_BLOCK, 0, 0)
    ys = _combine(seg, xs, cm, eo, DEC_BATCH, nbp, N_PROMPT)
    return yp, ys


def kernel(x_prompt, x_sample, state_conv, cache_k, cache_v, norm_mix_g, norm_ffn_g, w_in_ab, conv_w, conv_b,
           conv_ln_g, conv_ln_b, q_norm_g, k_norm_g, attn_sinks, w_out_ab, w_gate_dense, w_up_dense, w_down_dense,
           w_in_c, c_ln_g, c_ln_b, w_spatial, b_spatial, w_out_c, w_router, w_gate_exp, w_up_exp, w_down_exp):
    xp = x_prompt.reshape(N_PROMPT, D_MODEL)
    xs = x_sample.reshape(DEC_BATCH, D_MODEL)
    head = jnp.arange(B_WIDTH) // HEAD_DIM
    seg = (head[:, None] == head[None, :]).astype(BF16)
    conv_p, conv_s, kp_out, vp_out, ks_out, vs_out, vc_out = [], [], [], [], [], [], []
    row = lambda a: a.reshape(1, -1)
    for layer in range(DEPTH):
        i = layer // 2
        g_mix = row(norm_mix_g[layer])
        g_ffn = row(norm_ffn_g[layer])
        if layer % 2 == 0:
            w_in = w_in_ab[i].astype(BF16)
            qg = row(jnp.tile(q_norm_g[i], N_HEADS))
            kg = row(jnp.tile(k_norm_g[i], N_KV_HEADS))
            a_p, q_p, k_p, v_p = _inproj(xp, g_mix, w_in, qg, kg, seg, ROW_TILE)
            a_s, q_s, k_s, v_s = _inproj(xs, g_mix, w_in, qg, kg, seg, DEC_BATCH)
            cb, lg, lb = row(conv_b[i]), row(conv_ln_g[i]), row(conv_ln_b[i])
            ca_p = _conv_prompt(a_p, conv_w[i], cb, lg, lb)
            ca_s = _conv_sample(a_s, state_conv[i], conv_w[i], cb, lg, lb)
            ob_p = _attn_prompt(q_p, k_p, v_p, attn_sinks[i])
            sinks_b = jnp.broadcast_to(attn_sinks[i].reshape(N_KV_HEADS, Q_GROUP, 1), (N_KV_HEADS, Q_GROUP, 128))
            ob_s = _attn_sample(q_s.reshape(DEC_BATCH, N_KV_HEADS, Q_GROUP, HEAD_DIM),
                                k_s.reshape(DEC_BATCH, N_KV_HEADS, HEAD_DIM),
                                v_s.reshape(DEC_BATCH, N_KV_HEADS, HEAD_DIM),
                                cache_k[i].reshape(DEC_BATCH, WINDOW, KV_WIDTH),
                                cache_v[i].reshape(DEC_BATCH, WINDOW, KV_WIDTH), sinks_b)
            ob_s = ob_s.reshape(DEC_BATCH, B_WIDTH)
            conv_p.append(a_p.reshape(BATCH, SEQ, A_CH)[:, SEQ - (CONV_W - 1):])
            conv_s.append(jnp.concatenate([state_conv[i][:, 1:], a_s[:, None, :]], axis=1))
            kp_out.append(k_p.reshape(BATCH, SEQ, N_KV_HEADS, HEAD_DIM)[:, SEQ - WINDOW:])
            vp_out.append(v_p.reshape(BATCH, SEQ, N_KV_HEADS, HEAD_DIM)[:, SEQ - WINDOW:])
            ks_out.append(jnp.concatenate(
                [cache_k[i][:, 1:], k_s.reshape(DEC_BATCH, 1, N_KV_HEADS, HEAD_DIM)], axis=1))
            vs_out.append(jnp.concatenate(
                [cache_v[i][:, 1:], v_s.reshape(DEC_BATCH, 1, N_KV_HEADS, HEAD_DIM)], axis=1))
            wo = w_out_ab[i].astype(BF16)
            wg = w_gate_dense[i].astype(BF16)
            wu = w_up_dense[i].astype(BF16)
            wd = w_down_dense[i].astype(BF16)
            xp = _outproj_ffn(xp, ca_p, ob_p, wo, g_ffn, wg, wu, wd, ROW_TILE)
            xs = _outproj_ffn(xs, ca_s, ob_s, wo, g_ffn, wg, wu, wd, DEC_BATCH)
        else:
            win = w_in_c[i].astype(BF16)
            wout = w_out_c[i].astype(BF16)
            lg, lb = row(c_ln_g[i]), row(c_ln_b[i])
            xp = _gmlp_prompt(xp, g_mix, win, lg, lb, w_spatial[i], b_spatial[i].T, wout, ROW_TILE)
            gd = C_WIDTH // C_GROUPS
            w0 = row(jnp.repeat(w_spatial[i][:, 0, 0], gd))
            b0 = row(jnp.repeat(b_spatial[i][:, 0], gd))
            xs, v_s = _gmlp_sample(xs, g_mix, win, lg, lb, w0, b0, wout)
            vc_out.append(v_s.reshape(DEC_BATCH, 1, C_WIDTH))
            xp, xs = _moe(xp, xs, norm_ffn_g[layer], w_router[i], w_gate_exp, w_up_exp, w_down_exp, i)
    return (xp.reshape(BATCH, SEQ, D_MODEL), xs.reshape(DEC_BATCH, 1, D_MODEL),
            jnp.stack(conv_p), jnp.stack(conv_s), jnp.stack(kp_out), jnp.stack(vp_out),
            jnp.stack(ks_out), jnp.stack(vs_out), jnp.stack(vc_out))
```

```python
import functools

import jax
import jax.numpy as jnp
from jax import lax
from jax.experimental import pallas as pl
from jax.experimental.pallas import tpu as pltpu

F32 = jnp.float32
BF16 = jnp.bfloat16

D_MODEL = 1024
BATCH = 8
SEQ = 2048
DEPTH = 4
DEC_BATCH = 128
A_CH = 512
CONV_W = 31
N_HEADS = 8
N_KV_HEADS = 2
HEAD_DIM = 64
Q_GROUP = 4
WINDOW = 128
B_WIDTH = 512
KV_WIDTH = 128
AB_IN = 2 * A_CH + B_WIDTH + 2 * KV_WIDTH
C_WIDTH = 1024
C_GROUPS = 8
C_CHUNK = 128
D_FF = 2816
N_EXPERTS = 8
D_FF_EXPERT = 3584
EPS = 1e-6
ATTN_SCALE = HEAD_DIM ** -0.5

N_PROMPT = BATCH * SEQ
N_TOK = N_PROMPT + DEC_BATCH
ROW_TILE = 1024
FF_SPLIT = 11
MOE_TILE = 1024
MOE_FF_TILE = 512
MOE_FF_SUB = 256
MOE_ROW_SUB = 512
MOE_NF = D_FF_EXPERT // MOE_FF_TILE
MOE_BLOCK = 256
MOE_N_BLOCKS = N_PROMPT // MOE_BLOCK + 1
SEG_ALIGN = 8
MOE_MAX_TILES = (2 * N_TOK + MOE_N_BLOCKS * N_EXPERTS * (SEG_ALIGN - 1) + N_EXPERTS * (MOE_TILE - 1)) // MOE_TILE
N_SLOTS = MOE_MAX_TILES * MOE_TILE
VMEM_LIMIT = 56 * 1024 * 1024


def _params(n_axes, vmem=VMEM_LIMIT):
    return pltpu.CompilerParams(dimension_semantics=("arbitrary",) * n_axes, vmem_limit_bytes=vmem)


def _rms(x, g):
    return x * lax.rsqrt(jnp.mean(x * x, axis=-1, keepdims=True) + EPS) * g


def _layer_norm(x, g, b):
    mu = jnp.mean(x, axis=-1, keepdims=True)
    xc = x - mu
    var = jnp.mean(xc * xc, axis=-1, keepdims=True)
    return xc * lax.rsqrt(var + EPS) * g + b


def _const_spec(shape):
    return pl.BlockSpec(shape, lambda *_: (0,) * len(shape))


def _inproj_kernel(x_ref, g_ref, w_ref, qg_ref, kg_ref, seg_ref, a_ref, q_ref, k_ref, v_ref):
    h = _rms(x_ref[...], g_ref[...]).astype(BF16)
    z = jnp.dot(h, w_ref[...], preferred_element_type=F32)
    a_ref[...] = z[:, :A_CH] * jax.nn.sigmoid(z[:, A_CH:2 * A_CH])
    seg = seg_ref[...]

    def head_norm(t, gain, seg_t):
        sq = t * t
        hi = sq.astype(BF16)
        lo = (sq - hi.astype(F32)).astype(BF16)
        ms = (jnp.dot(hi, seg_t, preferred_element_type=F32)
              + jnp.dot(lo, seg_t, preferred_element_type=F32)) * (1.0 / HEAD_DIM)
        return t * lax.rsqrt(ms + EPS) * gain

    o = 2 * A_CH
    q = head_norm(z[:, o:o + B_WIDTH], qg_ref[...], seg)
    q_ref[...] = (q * ATTN_SCALE).astype(BF16)
    o += B_WIDTH
    k_ref[...] = head_norm(z[:, o:o + KV_WIDTH], kg_ref[...], seg[:KV_WIDTH, :KV_WIDTH])
    o += KV_WIDTH
    v_ref[...] = z[:, o:o + KV_WIDTH]


def _inproj(x, g, w_bf, qg, kg, seg, tm):
    n = x.shape[0]
    row = lambda w: pl.BlockSpec((tm, w), lambda i: (i, 0))
    return pl.pallas_call(
        _inproj_kernel,
        grid=(n // tm,),
        in_specs=[row(D_MODEL), _const_spec((1, D_MODEL)), _const_spec((D_MODEL, AB_IN)),
                  _const_spec((1, B_WIDTH)), _const_spec((1, KV_WIDTH)), _const_spec((B_WIDTH, B_WIDTH))],
        out_specs=[row(A_CH), row(B_WIDTH), row(KV_WIDTH), row(KV_WIDTH)],
        out_shape=[jax.ShapeDtypeStruct((n, A_CH), F32), jax.ShapeDtypeStruct((n, B_WIDTH), BF16),
                   jax.ShapeDtypeStruct((n, KV_WIDTH), F32), jax.ShapeDtypeStruct((n, KV_WIDTH), F32)],
        compiler_params=_params(1),
        name="inproj_ab",
    )(x, g, w_bf, qg, kg, seg)


CONV_PAD = 32
CONV_ROWS = 128


def _conv_tail(c, lg, lb):
    y = _layer_norm(c, lg, lb)
    return y * jax.nn.sigmoid(y)


def _conv_prompt_kernel(a_ref, w_ref, b_ref, lg_ref, lb_ref, o_ref, pad_ref, sh_ref, c_ref):
    pad_ref[0:CONV_PAD, :] = jnp.zeros((CONV_PAD, A_CH), F32)
    pad_ref[CONV_PAD:, :] = a_ref[...]
    win = CONV_ROWS + CONV_PAD

    def body(r, carry):
        t0 = pl.multiple_of(r * CONV_ROWS, CONV_ROWS)
        sh_ref[0] = pad_ref[pl.ds(t0, win), :]
        for s in range(1, 8):
            sh_ref[s, 0:win - 8, :] = sh_ref[0, s:s + win - 8, :]
        first = CONV_PAD - (CONV_W - 1)
        for c in range(A_CH // 128):
            lanes = slice(c * 128, (c + 1) * 128)
            acc = jnp.zeros((CONV_ROWS, 128), F32)
            for s in range(8):
                taps = [k for k in range(CONV_W) if (first + k) % 8 == s]
                rows = sh_ref[s, 0:(win if s == 0 else win - 8), lanes]
                for k in taps:
                    base = first + k - s
                    acc = acc + rows[base:base + CONV_ROWS] * w_ref[k:k + 1, lanes]
            c_ref[:, lanes] = acc
        y = _conv_tail(c_ref[...] + b_ref[...], lg_ref[...], lb_ref[...])
        o_ref[pl.ds(t0, CONV_ROWS), :] = y.astype(BF16)
        return carry

    lax.fori_loop(0, SEQ // CONV_ROWS, body, 0)


def _conv_prompt(a, w, b, lg, lb):
    return pl.pallas_call(
        _conv_prompt_kernel,
        grid=(BATCH,),
        in_specs=[pl.BlockSpec((SEQ, A_CH), lambda i: (i, 0)), _const_spec((CONV_W, A_CH)),
                  _const_spec((1, A_CH)), _const_spec((1, A_CH)), _const_spec((1, A_CH))],
        out_specs=pl.BlockSpec((SEQ, A_CH), lambda i: (i, 0)),
        out_shape=jax.ShapeDtypeStruct((N_PROMPT, A_CH), BF16),
        scratch_shapes=[pltpu.VMEM((CONV_PAD + SEQ, A_CH), F32), pltpu.VMEM((8, CONV_ROWS + CONV_PAD, A_CH), F32),
                        pltpu.VMEM((CONV_ROWS, A_CH), F32)],
        compiler_params=_params(1),
        name="conv_prompt",
    )(a, w, b, lg, lb)


def _conv_sample_kernel(a_ref, st_ref, w_ref, b_ref, lg_ref, lb_ref, o_ref):
    acc = a_ref[...] * w_ref[CONV_W - 1:CONV_W, :]
    acc = acc + jnp.sum(st_ref[...] * w_ref[0:CONV_W - 1, :][None], axis=1)
    o_ref[...] = _conv_tail(acc + b_ref[...], lg_ref[...], lb_ref[...]).astype(BF16)


def _conv_sample(a, st_t, w, b, lg, lb):
    return pl.pallas_call(
        _conv_sample_kernel,
        grid=(1,),
        in_specs=[_const_spec((DEC_BATCH, A_CH)), _const_spec((DEC_BATCH, CONV_W - 1, A_CH)),
                  _const_spec((CONV_W, A_CH)), _const_spec((1, A_CH)), _const_spec((1, A_CH)),
                  _const_spec((1, A_CH))],
        out_specs=_const_spec((DEC_BATCH, A_CH)),
        out_shape=jax.ShapeDtypeStruct((DEC_BATCH, A_CH), BF16),
        compiler_params=_params(1),
        name="conv_sample",
    )(a, st_t, w, b, lg, lb)


def _alibi_slope(h):
    return 2.0 ** (-8.0 * (h + 1) / N_HEADS)


ATTN_P_WINDOWS = 4


def _attn_prompt_kernel(sink_ref, q_ref, kp_ref, kc_ref, vp_ref, vc_ref, o_ref):
    step = pl.program_id(1)
    kk = jnp.concatenate([kp_ref[...], kc_ref[...]], axis=0).astype(BF16)
    vv = jnp.concatenate([vp_ref[...], vc_ref[...]], axis=0).astype(BF16)
    qi = lax.broadcasted_iota(jnp.int32, (WINDOW, 2 * WINDOW), 0)
    kj = lax.broadcasted_iota(jnp.int32, (WINDOW, 2 * WINDOW), 1)
    dist_i = WINDOW + qi - kj
    in_window = (dist_i >= 0) & (dist_i < WINDOW)
    lo = jnp.where(step > 0, 0, WINDOW)
    dist = dist_i.astype(F32)
    for w in range(ATTN_P_WINDOWS):
        valid = in_window & (kj >= lo) if w == 0 else in_window
        rows = slice(w * WINDOW, (w + 1) * WINDOW)
        keys = slice(w * WINDOW, (w + 2) * WINDOW)
        outs = []
        for h in range(N_HEADS):
            kv = h // Q_GROUP
            lanes = slice(kv * HEAD_DIM, (kv + 1) * HEAD_DIM)
            qh = q_ref[rows, h * HEAD_DIM:(h + 1) * HEAD_DIM]
            s = lax.dot_general(qh, kk[keys, lanes], (((1,), (1,)), ((), ())), preferred_element_type=F32)
            s = jnp.where(valid, s - _alibi_slope(h) * dist, -jnp.inf)
            sink = sink_ref[h]
            m = jnp.maximum(jnp.max(s, axis=-1, keepdims=True), sink)
            p = jnp.exp(s - m)
            denom = jnp.sum(p, axis=-1, keepdims=True) + jnp.exp(sink - m)
            p = (p / denom).astype(BF16)
            outs.append(jnp.dot(p, vv[keys, lanes], preferred_element_type=F32))
        o_ref[rows, :] = jnp.concatenate(outs, axis=-1).astype(BF16)


def _attn_prompt(q, k, v, sinks):
    rows = ATTN_P_WINDOWS * WINDOW
    ns = SEQ // rows
    own = lambda b, j: (b * ns + j, 0)
    prev = lambda b, j: (b * (SEQ // WINDOW) + jnp.maximum(j * ATTN_P_WINDOWS - 1, 0), 0)
    return pl.pallas_call(
        _attn_prompt_kernel,
        grid=(BATCH, ns),
        in_specs=[pl.BlockSpec(memory_space=pltpu.SMEM),
                  pl.BlockSpec((rows, B_WIDTH), own),
                  pl.BlockSpec((WINDOW, KV_WIDTH), prev), pl.BlockSpec((rows, KV_WIDTH), own),
                  pl.BlockSpec((WINDOW, KV_WIDTH), prev), pl.BlockSpec((rows, KV_WIDTH), own)],
        out_specs=pl.BlockSpec((rows, B_WIDTH), own),
        out_shape=jax.ShapeDtypeStruct((N_PROMPT, B_WIDTH), BF16),
        compiler_params=_params(2),
        name="attn_prompt",
    )(sinks, q, k, k, v, v)


ATTN_S_BLK = 32


def _attn_sample_kernel(sink_ref, q_ref, kn_ref, vn_ref, kc_ref, vc_ref, o_ref):
    kc = kc_ref[...].astype(BF16)
    vc = vc_ref[...].astype(BF16)
    j = lax.broadcasted_iota(jnp.int32, (1, 1, WINDOW), 2)
    dist = (WINDOW - j).astype(F32)
    g_idx = lax.broadcasted_iota(jnp.int32, (1, Q_GROUP, 1), 1)
    for kv in range(N_KV_HEADS):
        lanes = slice(kv * HEAD_DIM, (kv + 1) * HEAD_DIM)
        qg = q_ref[:, kv]
        slope = jnp.exp2(-(g_idx + (kv * Q_GROUP + 1)).astype(F32) * (8.0 / N_HEADS))
        s = jnp.einsum('ngd,njd->ngj', qg, kc[:, :, lanes], preferred_element_type=F32)
        s = jnp.where(j >= 1, s - slope * dist, -jnp.inf)
        kn = kn_ref[:, kv].astype(BF16).astype(F32)
        s_new = jnp.sum(qg.astype(F32) * kn[:, None, :], axis=-1, keepdims=True)
        sink = sink_ref[kv][:, :1][None]
        m = jnp.maximum(jnp.maximum(jnp.max(s, axis=-1, keepdims=True), s_new), sink)
        p = jnp.exp(s - m)
        p_new = jnp.exp(s_new - m)
        denom = jnp.sum(p, axis=-1, keepdims=True) + p_new + jnp.exp(sink - m)
        o = jnp.einsum('ngj,njd->ngd', (p / denom).astype(BF16), vc[:, :, lanes], preferred_element_type=F32)
        vn = vn_ref[:, kv].astype(BF16).astype(F32)
        o = o + (p_new / denom).astype(BF16).astype(F32) * vn[:, None, :]
        o_ref[:, kv] = o.astype(BF16)


def _attn_sample(q4, kn3, vn3, kc, vc, sinks_b):
    nb = ATTN_S_BLK
    return pl.pallas_call(
        _attn_sample_kernel,
        grid=(DEC_BATCH // nb,),
        in_specs=[_const_spec((N_KV_HEADS, Q_GROUP, 128)),
                  pl.BlockSpec((nb, N_KV_HEADS, Q_GROUP, HEAD_DIM), lambda i: (i, 0, 0, 0)),
                  pl.BlockSpec((nb, N_KV_HEADS, HEAD_DIM), lambda i: (i, 0, 0)),
                  pl.BlockSpec((nb, N_KV_HEADS, HEAD_DIM), lambda i: (i, 0, 0)),
                  pl.BlockSpec((nb, WINDOW, KV_WIDTH), lambda i: (i, 0, 0)),
                  pl.BlockSpec((nb, WINDOW, KV_WIDTH), lambda i: (i, 0, 0))],
        out_specs=pl.BlockSpec((nb, N_KV_HEADS, Q_GROUP, HEAD_DIM), lambda i: (i, 0, 0, 0)),
        out_shape=jax.ShapeDtypeStruct((DEC_BATCH, N_KV_HEADS, Q_GROUP, HEAD_DIM), BF16),
        compiler_params=_params(1),
        name="attn_sample",
    )(sinks_b, q4, kn3, vn3, kc, vc)


def _cache_shift_kernel(cache_ref, new_ref, *rest, layer):
    out_ref, sem = rest[-2:]
    old = pltpu.make_async_copy(cache_ref.at[layer, :, pl.ds(1, WINDOW - 1)],
                                out_ref.at[layer, :, pl.ds(0, WINDOW - 1)], sem.at[0])
    new = pltpu.make_async_copy(new_ref, out_ref.at[layer, :, WINDOW - 1], sem.at[1])
    old.start()
    new.start()
    old.wait()
    new.wait()


def _cache_shift(cache, new, layer, prev):
    hbm = pl.BlockSpec(memory_space=pl.ANY)
    args = [cache, new] + ([] if prev is None else [prev])
    return pl.pallas_call(
        functools.partial(_cache_shift_kernel, layer=layer),
        in_specs=[hbm] * len(args),
        out_specs=hbm,
        out_shape=jax.ShapeDtypeStruct(cache.shape, cache.dtype),
        scratch_shapes=[pltpu.SemaphoreType.DMA((2,))],
        input_output_aliases={} if prev is None else {2: 0},
        name="cache_shift",
    )(*args)


def _outproj_ffn_kernel(x_ref, ca_ref, ob_ref, g_ref, wo_hbm, wg_hbm, wu_hbm, wd_hbm, o_ref,
                        wo_ref, wg_ref, wu_ref, wd_ref):
    @pl.when(pl.program_id(0) == 0)
    def _():
        pltpu.sync_copy(wo_hbm, wo_ref)
        pltpu.sync_copy(wg_hbm, wg_ref)
        pltpu.sync_copy(wu_hbm, wu_ref)
        pltpu.sync_copy(wd_hbm, wd_ref)

    o_ref[...] = (x_ref[...]
                  + jnp.dot(ca_ref[...], wo_ref[:A_CH, :], preferred_element_type=F32)
                  + jnp.dot(ob_ref[...], wo_ref[A_CH:, :], preferred_element_type=F32))
    h = _rms(o_ref[...], g_ref[...]).astype(BF16)
    ff = D_FF // FF_SPLIT
    for c in range(FF_SPLIT):
        cols = slice(c * ff, (c + 1) * ff)
        gate = jnp.dot(h, wg_ref[:, cols], preferred_element_type=F32)
        up = jnp.dot(h, wu_ref[:, cols], preferred_element_type=F32)
        act = (gate * jax.nn.sigmoid(gate) * up).astype(BF16)
        o_ref[...] += jnp.dot(act, wd_ref[cols, :], preferred_element_type=F32)


def _outproj_ffn(x, ca, ob, wo, g, wg, wu, wd, tm):
    n = x.shape[0]
    row = lambda w: pl.BlockSpec((tm, w), lambda i: (i, 0))
    hbm = pl.BlockSpec(memory_space=pl.ANY)
    return pl.pallas_call(
        _outproj_ffn_kernel,
        grid=(n // tm,),
        in_specs=[row(D_MODEL), row(A_CH), row(B_WIDTH), _const_spec((1, D_MODEL)), hbm, hbm, hbm, hbm],
        out_specs=row(D_MODEL),
        out_shape=jax.ShapeDtypeStruct((n, D_MODEL), F32),
        scratch_shapes=[pltpu.VMEM((D_MODEL, D_MODEL), BF16), pltpu.VMEM((D_MODEL, D_FF), BF16),
                        pltpu.VMEM((D_MODEL, D_FF), BF16), pltpu.VMEM((D_FF, D_MODEL), BF16)],
        compiler_params=_params(1),
        name="outproj_ffn",
    )(x, ca, ob, g, wo, wg, wu, wd)


def _gmlp_prompt_kernel(x_ref, g_ref, win_ref, lg_ref, lb_ref, ws_ref, bs_ref, wout_ref, o_ref, gate_ref):
    x = x_ref[...]
    h = _rms(x, g_ref[...]).astype(BF16)
    z = jax.nn.gelu(jnp.dot(h, win_ref[...], preferred_element_type=F32))
    u = z[:, :C_WIDTH]
    v = _layer_norm(z[:, C_WIDTH:], lg_ref[...], lb_ref[...]).astype(BF16)
    ri = lax.broadcasted_iota(jnp.int32, (C_CHUNK, C_CHUNK), 0)
    ci = lax.broadcasted_iota(jnp.int32, (C_CHUNK, C_CHUNK), 1)
    tm = x.shape[0]
    gd = C_WIDTH // C_GROUPS
    for grp in range(C_GROUPS):
        w = jnp.where(ri >= ci, ws_ref[grp], 0.0).astype(BF16)
        b = bs_ref[:, grp:grp + 1]
        lanes = slice(grp * gd, (grp + 1) * gd)
        for c in range(tm // C_CHUNK):
            rows = slice(c * C_CHUNK, (c + 1) * C_CHUNK)
            mixed = jnp.dot(w, v[rows, lanes], preferred_element_type=F32) + b
            gate_ref[rows, lanes] = (u[rows, lanes] * mixed).astype(BF16)
    o_ref[...] = x + jnp.dot(gate_ref[...], wout_ref[...], preferred_element_type=F32)


def _gmlp_prompt(x, g, win, lg, lb, ws, bs_t, wout, tm):
    n = x.shape[0]
    row = lambda w: pl.BlockSpec((tm, w), lambda i: (i, 0))
    return pl.pallas_call(
        _gmlp_prompt_kernel,
        grid=(n // tm,),
        in_specs=[row(D_MODEL), _const_spec((1, D_MODEL)), _const_spec((D_MODEL, 2 * C_WIDTH)),
                  _const_spec((1, C_WIDTH)), _const_spec((1, C_WIDTH)),
                  _const_spec((C_GROUPS, C_CHUNK, C_CHUNK)), _const_spec((C_CHUNK, C_GROUPS)),
                  _const_spec((C_WIDTH, D_MODEL))],
        out_specs=row(D_MODEL),
        out_shape=jax.ShapeDtypeStruct((n, D_MODEL), F32),
        scratch_shapes=[pltpu.VMEM((tm, C_WIDTH), BF16)],
        compiler_params=_params(1),
        name="gmlp_prompt",
    )(x, g, win, lg, lb, ws, bs_t, wout)


def _gmlp_sample_kernel(x_ref, g_ref, win_ref, lg_ref, lb_ref, w0_ref, b0_ref, wout_ref, o_ref, v_ref):
    x = x_ref[...]
    h = _rms(x, g_ref[...]).astype(BF16)
    z = jax.nn.gelu(jnp.dot(h, win_ref[...], preferred_element_type=F32))
    v = _layer_norm(z[:, C_WIDTH:], lg_ref[...], lb_ref[...])
    v_ref[...] = v
    gated = z[:, :C_WIDTH] * (w0_ref[...] * v + b0_ref[...])
    o_ref[...] = x + jnp.dot(gated.astype(BF16), wout_ref[...], preferred_element_type=F32)


def _gmlp_sample(x, g, win, lg, lb, w0, b0, wout):
    n = x.shape[0]
    return pl.pallas_call(
        _gmlp_sample_kernel,
        grid=(1,),
        in_specs=[_const_spec((n, D_MODEL)), _const_spec((1, D_MODEL)), _const_spec((D_MODEL, 2 * C_WIDTH)),
                  _const_spec((1, C_WIDTH)), _const_spec((1, C_WIDTH)), _const_spec((1, C_WIDTH)),
                  _const_spec((1, C_WIDTH)), _const_spec((C_WIDTH, D_MODEL))],
        out_specs=[_const_spec((n, D_MODEL)), _const_spec((n, C_WIDTH))],
        out_shape=[jax.ShapeDtypeStruct((n, D_MODEL), F32), jax.ShapeDtypeStruct((n, C_WIDTH), F32)],
        compiler_params=_params(1),
        name="gmlp_sample",
    )(x, g, win, lg, lb, w0, b0, wout)


META_W = 8


def _split_bf16(a):
    hi = a.astype(BF16)
    return hi, (a - hi.astype(F32)).astype(BF16)


def _router_kernel(x_ref, g_ref, wr_ref, *rest, bt):
    hn_ref, cm_ref, lst_ref, cnt_ref = rest[-4:]
    h = _rms(x_ref[...], g_ref[...])
    tm = h.shape[0]
    h_hi, h_lo = _split_bf16(h)
    w_hi, w_lo = _split_bf16(wr_ref[...])
    hn_ref[...] = h_hi
    logits = (jnp.dot(h_hi, w_hi, preferred_element_type=F32) + jnp.dot(h_lo, w_hi, preferred_element_type=F32)
              + jnp.dot(h_hi, w_lo, preferred_element_type=F32))
    lane = lax.broadcasted_iota(jnp.int32, (bt, 128), 1).astype(F32)
    neg = -jnp.inf
    ri = lax.broadcasted_iota(jnp.int32, (bt, bt), 0)
    ci = lax.broadcasted_iota(jnp.int32, (bt, bt), 1)
    earlier = jnp.where(ri > ci, 1.0, 0.0).astype(BF16)
    ei = lax.broadcasted_iota(jnp.int32, (128, 128), 0)
    ej = lax.broadcasted_iota(jnp.int32, (128, 128), 1)
    lower_experts = jnp.where(ei < ej, 1.0, 0.0).astype(BF16)
    for j in range(tm // bt):
        rows = slice(j * bt, (j + 1) * bt)
        lg = jnp.where(lane < N_EXPERTS, logits[rows], neg)
        m1 = jnp.max(lg, axis=-1, keepdims=True)
        i1 = jnp.min(jnp.where(lg == m1, lane, 128.0), axis=-1, keepdims=True)
        lg2 = jnp.where(lane == i1, neg, lg)
        m2 = jnp.max(lg2, axis=-1, keepdims=True)
        i2 = jnp.min(jnp.where(lg2 == m2, lane, 128.0), axis=-1, keepdims=True)
        e = jnp.exp(m2 - m1)
        g1 = 1.0 / (1.0 + e)
        g2 = e / (1.0 + e)
        sel1 = lane == i1
        sel2 = lane == i2
        oh = jnp.where(sel1, 1.0, jnp.where(sel2, 1.0, 0.0))
        before = jnp.dot(earlier, oh.astype(BF16), preferred_element_type=F32)
        cnt = jnp.sum(oh, axis=0, keepdims=True)
        padded = jnp.floor((cnt + (SEG_ALIGN - 1)) * (1.0 / SEG_ALIGN)) * SEG_ALIGN
        start = jnp.dot(jnp.broadcast_to(padded, (8, 128)).astype(BF16), lower_experts,
                        preferred_element_type=F32)[0:1]
        slot = before + start
        l1 = jnp.sum(jnp.where(sel1, slot, 0.0), axis=-1, keepdims=True)
        l2 = jnp.sum(jnp.where(sel2, slot, 0.0), axis=-1, keepdims=True)
        rec = jnp.where(lane == 0, l1, jnp.where(lane == 1, l2, jnp.where(lane == 2, g1, jnp.where(lane == 3, g2, 0.0))))
        cm_ref[rows, :] = rec[:, :META_W]
        lst_ref[j] = rec.T[0:2, :].astype(jnp.int32)
        cnt_ref[j] = cnt


def _router(x, g, wr_pad, tm, bt, row_blk0, prev=None):
    n = x.shape[0]
    nblk = tm // bt
    in_specs = [pl.BlockSpec((tm, D_MODEL), lambda i: (i, 0)), _const_spec((1, D_MODEL)),
                _const_spec((D_MODEL, 128))]
    args = [x, g, wr_pad]
    aliases = {}
    if prev is not None:
        in_specs += [pl.BlockSpec(memory_space=pl.ANY), pl.BlockSpec(memory_space=pl.ANY)]
        args += list(prev)
        aliases = {3: 0, 4: 1}
    return pl.pallas_call(
        functools.partial(_router_kernel, bt=bt),
        grid=(n // tm,),
        in_specs=in_specs,
        out_specs=[pl.BlockSpec((tm, D_MODEL), lambda i: (row_blk0 + i, 0)),
                   pl.BlockSpec((tm, META_W), lambda i: (row_blk0 + i, 0)),
                   pl.BlockSpec((nblk, 2, bt), lambda i: (i, 0, 0)),
                   pl.BlockSpec((nblk, 1, 128), lambda i: (i, 0, 0))],
        out_shape=[jax.ShapeDtypeStruct((N_TOK, D_MODEL), BF16), jax.ShapeDtypeStruct((N_TOK, META_W), F32),
                   jax.ShapeDtypeStruct((n // bt, 2, bt), jnp.int32), jax.ShapeDtypeStruct((n // bt, 1, 128), F32)],
        input_output_aliases=aliases,
        compiler_params=_params(1),
        name="moe_router",
    )(*args)


def _local_rows(bt):
    return 2 * bt + N_EXPERTS * SEG_ALIGN


def _segment_copies(seg_n, seg_src, seg_dst, blk, local_ref, slot_ref, sem, bt, to_slots, action):
    def per_expert(e, c):
        s = blk * N_EXPERTS + e
        n, src, dst = seg_n[s], seg_src[s], seg_dst[s]
        off = jnp.int32(0)
        p = bt
        while p >= SEG_ALIGN:
            hit = (n & p) != 0

            @pl.when(hit)
            def _(off=off, p=p):
                a = local_ref.at[pl.ds(pl.multiple_of(src + off, SEG_ALIGN), p)]
                b = slot_ref.at[pl.ds(pl.multiple_of(dst + off, SEG_ALIGN), p)]
                action(pltpu.make_async_copy(a, b, sem) if to_slots else pltpu.make_async_copy(b, a, sem))

            off = off + (n & p)
            p //= 2
        return c

    lax.fori_loop(0, N_EXPERTS, per_expert, 0)


def _dispatch_kernel(seg_n, seg_src, seg_dst, hn_ref, ls_ref, *rest, bt, blk0, nb):
    slots_ref, buf_ref, sem = rest[-3:]
    i = pl.program_id(0)
    par = i % 2
    start = lambda cp: cp.start()
    wait = lambda cp: cp.wait()
    copies = lambda blk, p, action: _segment_copies(seg_n, seg_src, seg_dst, blk, buf_ref.at[p], slots_ref,
                                                     sem.at[p], bt, True, action)

    @pl.when(i >= 2)
    def _():
        copies(blk0 + i - 2, par, wait)

    rows = lax.broadcasted_iota(jnp.int32, (_local_rows(bt), bt), 0)
    ls = ls_ref[0]
    onehot = jnp.where(rows == ls[0:1, :], 1.0, jnp.where(rows == ls[1:2, :], 1.0, 0.0)).astype(BF16)
    buf_ref[par] = jnp.dot(onehot, hn_ref[...], preferred_element_type=F32)
    copies(blk0 + i, par, start)

    @pl.when(i == nb - 1)
    def _():
        if nb >= 2:
            copies(blk0 + i - 1, 1 - par, wait)
        copies(blk0 + i, par, wait)


def _dispatch(seg, hn, ls_t, bt, blk0, tok0, prev=None):
    nb = ls_t.shape[0]
    in_specs = [pl.BlockSpec((bt, D_MODEL), lambda i, *_: (tok0 // bt + i, 0)),
                pl.BlockSpec((1, 2, bt), lambda i, *_: (i, 0, 0))]
    args = [hn, ls_t]
    aliases = {}
    if prev is not None:
        in_specs.append(pl.BlockSpec(memory_space=pl.ANY))
        args.append(prev)
        aliases = {5: 0}
    return pl.pallas_call(
        functools.partial(_dispatch_kernel, bt=bt, blk0=blk0, nb=nb),
        grid_spec=pltpu.PrefetchScalarGridSpec(
            num_scalar_prefetch=3,
            grid=(nb,),
            in_specs=in_specs,
            out_specs=pl.BlockSpec(memory_space=pl.ANY),
            scratch_shapes=[pltpu.VMEM((2, _local_rows(bt), D_MODEL), F32), pltpu.SemaphoreType.DMA((2,))],
        ),
        out_shape=jax.ShapeDtypeStruct((N_SLOTS, D_MODEL), F32),
        input_output_aliases=aliases,
        compiler_params=_params(1),
        name="moe_dispatch",
    )(*seg, *args)


def _experts_kernel(te_ref, tb_ref, nv_ref, x_ref, wg_ref, wu_ref, wd_ref, o_ref, xb_ref):
    t = pl.program_id(0)
    f = pl.program_id(1)
    nv = nv_ref[t]

    @pl.when(jnp.logical_and(nv > 0, f == 0))
    def _():
        rows = lax.broadcasted_iota(jnp.int32, (MOE_TILE, 1), 0)
        xb_ref[...] = jnp.where(rows < nv, x_ref[...], 0.0).astype(BF16)
        o_ref[...] = jnp.zeros_like(o_ref)

    def swiglu_rows(n_rows):
        rows = slice(0, n_rows)
        xb = xb_ref[rows, :]
        for c in range(MOE_FF_TILE // MOE_FF_SUB):
            cols = slice(c * MOE_FF_SUB, (c + 1) * MOE_FF_SUB)
            gate = jnp.dot(xb, wg_ref[:, cols].astype(BF16), preferred_element_type=F32)
            up = jnp.dot(xb, wu_ref[:, cols].astype(BF16), preferred_element_type=F32)
            act = (gate * jax.nn.sigmoid(gate) * up).astype(BF16)
            o_ref[rows, :] += jnp.dot(act, wd_ref[cols, :].astype(BF16), preferred_element_type=F32)

    @pl.when(nv > MOE_ROW_SUB)
    def _():
        swiglu_rows(MOE_TILE)

    @pl.when(jnp.logical_and(nv > 0, nv <= MOE_ROW_SUB))
    def _():
        swiglu_rows(MOE_ROW_SUB)


def _experts(tile_expert, tile_blk, tile_rows, xs, wg, wu, wd, layer):
    def chunk(t, f, nv):
        return jnp.where(nv[t] > 0, f, MOE_NF - 1)
    up_spec = pl.BlockSpec((None, None, D_MODEL, MOE_FF_TILE),
                           lambda t, f, te, tb, nv: (layer, te[t], 0, chunk(t, f, nv)))
    down_spec = pl.BlockSpec((None, None, MOE_FF_TILE, D_MODEL),
                             lambda t, f, te, tb, nv: (layer, te[t], chunk(t, f, nv), 0))
    return pl.pallas_call(
        _experts_kernel,
        grid_spec=pltpu.PrefetchScalarGridSpec(
            num_scalar_prefetch=3,
            grid=(MOE_MAX_TILES, MOE_NF),
            in_specs=[pl.BlockSpec((MOE_TILE, D_MODEL), lambda t, f, te, tb, nv: (tb[t], 0)),
                      up_spec, up_spec, down_spec],
            out_specs=pl.BlockSpec((MOE_TILE, D_MODEL), lambda t, f, te, tb, nv: (tb[t], 0)),
            scratch_shapes=[pltpu.VMEM((MOE_TILE, D_MODEL), BF16)],
        ),
        out_shape=jax.ShapeDtypeStruct((N_SLOTS, D_MODEL), F32),
        compiler_params=_params(2),
        name="moe_experts",
    )(tile_expert, tile_blk, tile_rows, xs, wg, wu, wd)


def _combine_kernel(seg_n, seg_src, seg_dst, x_ref, cm_ref, eo_ref, y_ref, buf_ref, sem, *, bt, blk0, nb):
    i = pl.program_id(0)
    par = i % 2
    start = lambda cp: cp.start()
    wait = lambda cp: cp.wait()
    copies = lambda blk, p, action: _segment_copies(seg_n, seg_src, seg_dst, blk, buf_ref.at[p], eo_ref,
                                                     sem.at[p], bt, False, action)

    @pl.when(i == 0)
    def _():
        copies(blk0, 0, start)

    @pl.when(i + 1 < nb)
    def _():
        copies(blk0 + i + 1, 1 - par, start)

    copies(blk0 + i, par, wait)
    last = (blk0 + i) * N_EXPERTS + (N_EXPERTS - 1)
    used = seg_src[last] + seg_n[last]
    rows = lax.broadcasted_iota(jnp.int32, (_local_rows(bt), 1), 0)
    e_hi, e_lo = _split_bf16(jnp.where(rows < used, buf_ref[par], 0.0))
    cm = cm_ref[...]
    cols = lax.broadcasted_iota(jnp.int32, (bt, _local_rows(bt)), 1).astype(F32)
    comb = jnp.where(cols == cm[:, 0:1], cm[:, 2:3], jnp.where(cols == cm[:, 1:2], cm[:, 3:4], 0.0))
    c_hi, c_lo = _split_bf16(comb)
    y_ref[...] = x_ref[...] + jnp.dot(c_hi, e_hi, preferred_element_type=F32)
    y_ref[...] += jnp.dot(c_lo, e_hi, preferred_element_type=F32)
    y_ref[...] += jnp.dot(c_hi, e_lo, preferred_element_type=F32)


def _combine(seg, x, cm, eo, bt, blk0, tok0):
    n = x.shape[0]
    nb = n // bt
    return pl.pallas_call(
        functools.partial(_combine_kernel, bt=bt, blk0=blk0, nb=nb),
        grid_spec=pltpu.PrefetchScalarGridSpec(
            num_scalar_prefetch=3,
            grid=(nb,),
            in_specs=[pl.BlockSpec((bt, D_MODEL), lambda i, *_: (i, 0)),
                      pl.BlockSpec((bt, META_W), lambda i, *_: (tok0 // bt + i, 0)),
                      pl.BlockSpec(memory_space=pl.ANY)],
            out_specs=pl.BlockSpec((bt, D_MODEL), lambda i, *_: (i, 0)),
            scratch_shapes=[pltpu.VMEM((2, _local_rows(bt), D_MODEL), F32), pltpu.SemaphoreType.DMA((2,))],
        ),
        out_shape=jax.ShapeDtypeStruct((n, D_MODEL), F32),
        compiler_params=_params(1),
        name="moe_combine",
    )(*seg, x, cm, eo)


def _moe(xp, xs, g, w_router, wg, wu, wd, layer):
    g2 = g.reshape(1, D_MODEL)
    wr_pad = jnp.pad(w_router, ((0, 0), (0, 128 - N_EXPERTS)))
    hn, cm, ls_p, cnt_p = _router(xp, g2, wr_pad, ROW_TILE, MOE_BLOCK, 0)
    hn, cm, ls_s, cnt_s = _router(xs, g2, wr_pad, DEC_BATCH, DEC_BATCH, N_PROMPT // DEC_BATCH, prev=(hn, cm))

    nbp = N_PROMPT // MOE_BLOCK
    seg_cnt = jnp.concatenate([cnt_p[:, 0, :N_EXPERTS], cnt_s[:, 0, :N_EXPERTS]], axis=0).astype(jnp.int32)
    seg_n = (seg_cnt + (SEG_ALIGN - 1)) // SEG_ALIGN * SEG_ALIGN
    seg_src = jnp.cumsum(seg_n, axis=1) - seg_n
    slot0 = jnp.cumsum(seg_n, axis=0) - seg_n

    counts = seg_n.sum(axis=0)
    tiles_e = (counts + (MOE_TILE - 1)) // MOE_TILE
    tile_end = jnp.cumsum(tiles_e)
    tile_start = tile_end - tiles_e
    n_tiles = tile_end[-1]
    t = jnp.arange(MOE_MAX_TILES, dtype=jnp.int32)
    tc = jnp.minimum(t, n_tiles - 1)
    tile_expert = jnp.sum((tc[:, None] >= tile_end[None, :]).astype(jnp.int32), axis=1)
    rows_left = counts[tile_expert] - (tc - tile_start[tile_expert]) * MOE_TILE
    tile_rows = jnp.where(t < n_tiles, jnp.minimum(rows_left, MOE_TILE), 0).astype(jnp.int32)
    seg_dst = (tile_start * MOE_TILE)[None, :] + slot0

    cap = jnp.concatenate([jnp.full((nbp, 1), MOE_BLOCK, jnp.int32), jnp.full((1, 1), DEC_BATCH, jnp.int32)])
    seg_n = jnp.clip(seg_n, 0, cap)
    seg = (seg_n.reshape(-1), jnp.clip(seg_src, 0, 2 * cap + N_EXPERTS * SEG_ALIGN - seg_n).reshape(-1),
           jnp.clip(seg_dst, 0, N_SLOTS - MOE_BLOCK).reshape(-1))

    slots = _dispatch(seg, hn, ls_p, MOE_BLOCK, 0, 0)
    slots = _dispatch(seg, hn, ls_s, DEC_BATCH, nbp, N_PROMPT, prev=slots)
    eo = _experts(tile_expert.astype(jnp.int32), tc, tile_rows, slots, wg, wu, wd, layer)
    yp = _combine(seg, xp, cm, eo, MOE_BLOCK, 0, 0)
    ys = _combine(seg, xs, cm, eo, DEC_BATCH, nbp, N_PROMPT)
    return yp, ys


def kernel(x_prompt, x_sample, state_conv, cache_k, cache_v, norm_mix_g, norm_ffn_g, w_in_ab, conv_w, conv_b,
           conv_ln_g, conv_ln_b, q_norm_g, k_norm_g, attn_sinks, w_out_ab, w_gate_dense, w_up_dense, w_down_dense,
           w_in_c, c_ln_g, c_ln_b, w_spatial, b_spatial, w_out_c, w_router, w_gate_exp, w_up_exp, w_down_exp):
    xp = x_prompt.reshape(N_PROMPT, D_MODEL)
    xs = x_sample.reshape(DEC_BATCH, D_MODEL)
    head = jnp.arange(B_WIDTH) // HEAD_DIM
    seg = (head[:, None] == head[None, :]).astype(BF16)
    conv_p, conv_s, kp_out, vp_out, vc_out = [], [], [], [], []
    ks_out = vs_out = None
    row = lambda a: a.reshape(1, -1)
    for layer in range(DEPTH):
        i = layer // 2
        g_mix = row(norm_mix_g[layer])
        g_ffn = row(norm_ffn_g[layer])
        if layer % 2 == 0:
            w_in = w_in_ab[i].astype(BF16)
            qg = row(jnp.tile(q_norm_g[i], N_HEADS))
            kg = row(jnp.tile(k_norm_g[i], N_KV_HEADS))
            a_p, q_p, k_p, v_p = _inproj(xp, g_mix, w_in, qg, kg, seg, ROW_TILE)
            a_s, q_s, k_s, v_s = _inproj(xs, g_mix, w_in, qg, kg, seg, DEC_BATCH)
            cb, lg, lb = row(conv_b[i]), row(conv_ln_g[i]), row(conv_ln_b[i])
            ca_p = _conv_prompt(a_p, conv_w[i], cb, lg, lb)
            ca_s = _conv_sample(a_s, state_conv[i], conv_w[i], cb, lg, lb)
            ob_p = _attn_prompt(q_p, k_p, v_p, attn_sinks[i])
            sinks_b = jnp.broadcast_to(attn_sinks[i].reshape(N_KV_HEADS, Q_GROUP, 1), (N_KV_HEADS, Q_GROUP, 128))
            ob_s = _attn_sample(q_s.reshape(DEC_BATCH, N_KV_HEADS, Q_GROUP, HEAD_DIM),
                                k_s.reshape(DEC_BATCH, N_KV_HEADS, HEAD_DIM),
                                v_s.reshape(DEC_BATCH, N_KV_HEADS, HEAD_DIM),
                                cache_k[i].reshape(DEC_BATCH, WINDOW, KV_WIDTH),
                                cache_v[i].reshape(DEC_BATCH, WINDOW, KV_WIDTH), sinks_b)
            ob_s = ob_s.reshape(DEC_BATCH, B_WIDTH)
            conv_p.append(a_p.reshape(BATCH, SEQ, A_CH)[:, SEQ - (CONV_W - 1):])
            conv_s.append(jnp.concatenate([state_conv[i][:, 1:], a_s[:, None, :]], axis=1))
            kp_out.append(k_p.reshape(BATCH, SEQ, N_KV_HEADS, HEAD_DIM)[:, SEQ - WINDOW:])
            vp_out.append(v_p.reshape(BATCH, SEQ, N_KV_HEADS, HEAD_DIM)[:, SEQ - WINDOW:])
            ks_out = _cache_shift(cache_k, k_s.reshape(DEC_BATCH, N_KV_HEADS, HEAD_DIM), i, ks_out)
            vs_out = _cache_shift(cache_v, v_s.reshape(DEC_BATCH, N_KV_HEADS, HEAD_DIM), i, vs_out)
            wo = w_out_ab[i].astype(BF16)
            wg = w_gate_dense[i].astype(BF16)
            wu = w_up_dense[i].astype(BF16)
            wd = w_down_dense[i].astype(BF16)
            xp = _outproj_ffn(xp, ca_p, ob_p, wo, g_ffn, wg, wu, wd, ROW_TILE)
            xs = _outproj_ffn(xs, ca_s, ob_s, wo, g_ffn, wg, wu, wd, DEC_BATCH)
        else:
            win = w_in_c[i].astype(BF16)
            wout = w_out_c[i].astype(BF16)
            lg, lb = row(c_ln_g[i]), row(c_ln_b[i])
            xp = _gmlp_prompt(xp, g_mix, win, lg, lb, w_spatial[i], b_spatial[i].T, wout, ROW_TILE)
            gd = C_WIDTH // C_GROUPS
            w0 = row(jnp.repeat(w_spatial[i][:, 0, 0], gd))
            b0 = row(jnp.repeat(b_spatial[i][:, 0], gd))
            xs, v_s = _gmlp_sample(xs, g_mix, win, lg, lb, w0, b0, wout)
            vc_out.append(v_s.reshape(DEC_BATCH, 1, C_WIDTH))
            xp, xs = _moe(xp, xs, norm_ffn_g[layer], w_router[i], w_gate_exp, w_up_exp, w_down_exp, i)
    return (xp.reshape(BATCH, SEQ, D_MODEL), xs.reshape(DEC_BATCH, 1, D_MODEL),
            jnp.stack(conv_p), jnp.stack(conv_s), jnp.stack(kp_out), jnp.stack(vp_out),
            ks_out, vs_out, jnp.stack(vc_out))
```

```python
import functools

import jax
import jax.numpy as jnp
from jax import lax
from jax.experimental import pallas as pl
from jax.experimental.pallas import tpu as pltpu

F32 = jnp.float32
BF16 = jnp.bfloat16

D_MODEL = 1024
BATCH = 8
SEQ = 2048
DEPTH = 4
DEC_BATCH = 128
A_CH = 512
CONV_W = 31
N_HEADS = 8
N_KV_HEADS = 2
HEAD_DIM = 64
Q_GROUP = 4
WINDOW = 128
B_WIDTH = 512
KV_WIDTH = 128
AB_IN = 2 * A_CH + B_WIDTH + 2 * KV_WIDTH
C_WIDTH = 1024
C_GROUPS = 8
C_CHUNK = 128
D_FF = 2816
N_EXPERTS = 8
D_FF_EXPERT = 3584
EPS = 1e-6
ATTN_SCALE = HEAD_DIM ** -0.5

N_PROMPT = BATCH * SEQ
N_TOK = N_PROMPT + DEC_BATCH
ROW_TILE = 1024
FF_SPLIT = 11
MOE_TILE = 1024
MOE_FF_TILE = 512
MOE_FF_SUB = 256
MOE_ROW_PATHS = (256, 512, MOE_TILE)
MOE_NF = D_FF_EXPERT // MOE_FF_TILE
MOE_BLOCK = 256
MOE_N_BLOCKS = N_PROMPT // MOE_BLOCK + 1
SEG_ALIGN = 8
MOE_MAX_TILES = (2 * N_TOK + MOE_N_BLOCKS * N_EXPERTS * (SEG_ALIGN - 1) + N_EXPERTS * (MOE_TILE - 1)) // MOE_TILE
N_SLOTS = MOE_MAX_TILES * MOE_TILE
VMEM_LIMIT = 56 * 1024 * 1024


def _params(n_axes, vmem=VMEM_LIMIT):
    return pltpu.CompilerParams(dimension_semantics=("arbitrary",) * n_axes, vmem_limit_bytes=vmem)


def _rms(x, g):
    return x * lax.rsqrt(jnp.mean(x * x, axis=-1, keepdims=True) + EPS) * g


def _layer_norm(x, g, b):
    mu = jnp.mean(x, axis=-1, keepdims=True)
    xc = x - mu
    var = jnp.mean(xc * xc, axis=-1, keepdims=True)
    return xc * lax.rsqrt(var + EPS) * g + b


def _const_spec(shape):
    return pl.BlockSpec(shape, lambda *_: (0,) * len(shape))


def _inproj_kernel(x_ref, g_ref, w_ref, qg_ref, kg_ref, seg_ref, a_ref, q_ref, k_ref, v_ref):
    h = _rms(x_ref[...], g_ref[...]).astype(BF16)
    z = jnp.dot(h, w_ref[...], preferred_element_type=F32)
    a_ref[...] = z[:, :A_CH] * jax.nn.sigmoid(z[:, A_CH:2 * A_CH])
    seg = seg_ref[...]

    def head_norm(t, gain, seg_t):
        sq = t * t
        hi = sq.astype(BF16)
        lo = (sq - hi.astype(F32)).astype(BF16)
        ms = (jnp.dot(hi, seg_t, preferred_element_type=F32)
              + jnp.dot(lo, seg_t, preferred_element_type=F32)) * (1.0 / HEAD_DIM)
        return t * lax.rsqrt(ms + EPS) * gain

    o = 2 * A_CH
    q = head_norm(z[:, o:o + B_WIDTH], qg_ref[...], seg)
    q_ref[...] = (q * ATTN_SCALE).astype(BF16)
    o += B_WIDTH
    k_ref[...] = head_norm(z[:, o:o + KV_WIDTH], kg_ref[...], seg[:KV_WIDTH, :KV_WIDTH])
    o += KV_WIDTH
    v_ref[...] = z[:, o:o + KV_WIDTH]


def _inproj(x, g, w_bf, qg, kg, seg, tm):
    n = x.shape[0]
    row = lambda w: pl.BlockSpec((tm, w), lambda i: (i, 0))
    return pl.pallas_call(
        _inproj_kernel,
        grid=(n // tm,),
        in_specs=[row(D_MODEL), _const_spec((1, D_MODEL)), _const_spec((D_MODEL, AB_IN)),
                  _const_spec((1, B_WIDTH)), _const_spec((1, KV_WIDTH)), _const_spec((B_WIDTH, B_WIDTH))],
        out_specs=[row(A_CH), row(B_WIDTH), row(KV_WIDTH), row(KV_WIDTH)],
        out_shape=[jax.ShapeDtypeStruct((n, A_CH), F32), jax.ShapeDtypeStruct((n, B_WIDTH), BF16),
                   jax.ShapeDtypeStruct((n, KV_WIDTH), F32), jax.ShapeDtypeStruct((n, KV_WIDTH), F32)],
        compiler_params=_params(1),
        name="inproj_ab",
    )(x, g, w_bf, qg, kg, seg)


CONV_PAD = 32
CONV_ROWS = 128


def _conv_tail(c, lg, lb):
    y = _layer_norm(c, lg, lb)
    return y * jax.nn.sigmoid(y)


def _conv_prompt_kernel(a_ref, w_ref, b_ref, lg_ref, lb_ref, o_ref, pad_ref, sh_ref, c_ref):
    pad_ref[0:CONV_PAD, :] = jnp.zeros((CONV_PAD, A_CH), F32)
    pad_ref[CONV_PAD:, :] = a_ref[...]
    win = CONV_ROWS + CONV_PAD

    def body(r, carry):
        t0 = pl.multiple_of(r * CONV_ROWS, CONV_ROWS)
        sh_ref[0] = pad_ref[pl.ds(t0, win), :]
        for s in range(1, 8):
            sh_ref[s, 0:win - 8, :] = sh_ref[0, s:s + win - 8, :]
        first = CONV_PAD - (CONV_W - 1)
        for c in range(A_CH // 128):
            lanes = slice(c * 128, (c + 1) * 128)
            acc = jnp.zeros((CONV_ROWS, 128), F32)
            for s in range(8):
                taps = [k for k in range(CONV_W) if (first + k) % 8 == s]
                rows = sh_ref[s, 0:(win if s == 0 else win - 8), lanes]
                for k in taps:
                    base = first + k - s
                    acc = acc + rows[base:base + CONV_ROWS] * w_ref[k:k + 1, lanes]
            c_ref[:, lanes] = acc
        y = _conv_tail(c_ref[...] + b_ref[...], lg_ref[...], lb_ref[...])
        o_ref[pl.ds(t0, CONV_ROWS), :] = y.astype(BF16)
        return carry

    lax.fori_loop(0, SEQ // CONV_ROWS, body, 0)


def _conv_prompt(a, w, b, lg, lb):
    return pl.pallas_call(
        _conv_prompt_kernel,
        grid=(BATCH,),
        in_specs=[pl.BlockSpec((SEQ, A_CH), lambda i: (i, 0)), _const_spec((CONV_W, A_CH)),
                  _const_spec((1, A_CH)), _const_spec((1, A_CH)), _const_spec((1, A_CH))],
        out_specs=pl.BlockSpec((SEQ, A_CH), lambda i: (i, 0)),
        out_shape=jax.ShapeDtypeStruct((N_PROMPT, A_CH), BF16),
        scratch_shapes=[pltpu.VMEM((CONV_PAD + SEQ, A_CH), F32), pltpu.VMEM((8, CONV_ROWS + CONV_PAD, A_CH), F32),
                        pltpu.VMEM((CONV_ROWS, A_CH), F32)],
        compiler_params=_params(1),
        name="conv_prompt",
    )(a, w, b, lg, lb)


def _conv_sample_kernel(a_ref, st_ref, w_ref, b_ref, lg_ref, lb_ref, o_ref):
    acc = a_ref[...] * w_ref[CONV_W - 1:CONV_W, :]
    acc = acc + jnp.sum(st_ref[...] * w_ref[0:CONV_W - 1, :][None], axis=1)
    o_ref[...] = _conv_tail(acc + b_ref[...], lg_ref[...], lb_ref[...]).astype(BF16)


def _conv_sample(a, st_t, w, b, lg, lb):
    return pl.pallas_call(
        _conv_sample_kernel,
        grid=(1,),
        in_specs=[_const_spec((DEC_BATCH, A_CH)), _const_spec((DEC_BATCH, CONV_W - 1, A_CH)),
                  _const_spec((CONV_W, A_CH)), _const_spec((1, A_CH)), _const_spec((1, A_CH)),
                  _const_spec((1, A_CH))],
        out_specs=_const_spec((DEC_BATCH, A_CH)),
        out_shape=jax.ShapeDtypeStruct((DEC_BATCH, A_CH), BF16),
        compiler_params=_params(1),
        name="conv_sample",
    )(a, st_t, w, b, lg, lb)


def _alibi_slope(h):
    return 2.0 ** (-8.0 * (h + 1) / N_HEADS)


ATTN_P_WINDOWS = 4


def _attn_prompt_kernel(sink_ref, q_ref, kp_ref, kc_ref, vp_ref, vc_ref, o_ref):
    step = pl.program_id(1)
    kk = jnp.concatenate([kp_ref[...], kc_ref[...]], axis=0).astype(BF16)
    vv = jnp.concatenate([vp_ref[...], vc_ref[...]], axis=0).astype(BF16)
    qi = lax.broadcasted_iota(jnp.int32, (WINDOW, 2 * WINDOW), 0)
    kj = lax.broadcasted_iota(jnp.int32, (WINDOW, 2 * WINDOW), 1)
    dist_i = WINDOW + qi - kj
    in_window = (dist_i >= 0) & (dist_i < WINDOW)
    lo = jnp.where(step > 0, 0, WINDOW)
    dist = dist_i.astype(F32)
    for w in range(ATTN_P_WINDOWS):
        valid = in_window & (kj >= lo) if w == 0 else in_window
        rows = slice(w * WINDOW, (w + 1) * WINDOW)
        keys = slice(w * WINDOW, (w + 2) * WINDOW)
        outs = []
        for h in range(N_HEADS):
            kv = h // Q_GROUP
            lanes = slice(kv * HEAD_DIM, (kv + 1) * HEAD_DIM)
            qh = q_ref[rows, h * HEAD_DIM:(h + 1) * HEAD_DIM]
            s = lax.dot_general(qh, kk[keys, lanes], (((1,), (1,)), ((), ())), preferred_element_type=F32)
            s = jnp.where(valid, s - _alibi_slope(h) * dist, -jnp.inf)
            sink = sink_ref[h]
            m = jnp.maximum(jnp.max(s, axis=-1, keepdims=True), sink)
            p = jnp.exp(s - m)
            denom = jnp.sum(p, axis=-1, keepdims=True) + jnp.exp(sink - m)
            p = (p / denom).astype(BF16)
            outs.append(jnp.dot(p, vv[keys, lanes], preferred_element_type=F32))
        o_ref[rows, :] = jnp.concatenate(outs, axis=-1).astype(BF16)


def _attn_prompt(q, k, v, sinks):
    rows = ATTN_P_WINDOWS * WINDOW
    ns = SEQ // rows
    own = lambda b, j: (b * ns + j, 0)
    prev = lambda b, j: (b * (SEQ // WINDOW) + jnp.maximum(j * ATTN_P_WINDOWS - 1, 0), 0)
    return pl.pallas_call(
        _attn_prompt_kernel,
        grid=(BATCH, ns),
        in_specs=[pl.BlockSpec(memory_space=pltpu.SMEM),
                  pl.BlockSpec((rows, B_WIDTH), own),
                  pl.BlockSpec((WINDOW, KV_WIDTH), prev), pl.BlockSpec((rows, KV_WIDTH), own),
                  pl.BlockSpec((WINDOW, KV_WIDTH), prev), pl.BlockSpec((rows, KV_WIDTH), own)],
        out_specs=pl.BlockSpec((rows, B_WIDTH), own),
        out_shape=jax.ShapeDtypeStruct((N_PROMPT, B_WIDTH), BF16),
        compiler_params=_params(2),
        name="attn_prompt",
    )(sinks, q, k, k, v, v)


ATTN_S_BLK = 32


def _attn_sample_kernel(sink_ref, q_ref, kn_ref, vn_ref, kc_ref, vc_ref, o_ref):
    kc = kc_ref[...].astype(BF16)
    vc = vc_ref[...].astype(BF16)
    j = lax.broadcasted_iota(jnp.int32, (1, 1, WINDOW), 2)
    dist = (WINDOW - j).astype(F32)
    g_idx = lax.broadcasted_iota(jnp.int32, (1, Q_GROUP, 1), 1)
    for kv in range(N_KV_HEADS):
        lanes = slice(kv * HEAD_DIM, (kv + 1) * HEAD_DIM)
        qg = q_ref[:, kv]
        slope = jnp.exp2(-(g_idx + (kv * Q_GROUP + 1)).astype(F32) * (8.0 / N_HEADS))
        s = jnp.einsum('ngd,njd->ngj', qg, kc[:, :, lanes], preferred_element_type=F32)
        s = jnp.where(j >= 1, s - slope * dist, -jnp.inf)
        kn = kn_ref[:, kv].astype(BF16).astype(F32)
        s_new = jnp.sum(qg.astype(F32) * kn[:, None, :], axis=-1, keepdims=True)
        sink = sink_ref[kv][:, :1][None]
        m = jnp.maximum(jnp.maximum(jnp.max(s, axis=-1, keepdims=True), s_new), sink)
        p = jnp.exp(s - m)
        p_new = jnp.exp(s_new - m)
        denom = jnp.sum(p, axis=-1, keepdims=True) + p_new + jnp.exp(sink - m)
        o = jnp.einsum('ngj,njd->ngd', (p / denom).astype(BF16), vc[:, :, lanes], preferred_element_type=F32)
        vn = vn_ref[:, kv].astype(BF16).astype(F32)
        o = o + (p_new / denom).astype(BF16).astype(F32) * vn[:, None, :]
        o_ref[:, kv] = o.astype(BF16)


def _attn_sample(q4, kn3, vn3, kc, vc, sinks_b):
    nb = ATTN_S_BLK
    return pl.pallas_call(
        _attn_sample_kernel,
        grid=(DEC_BATCH // nb,),
        in_specs=[_const_spec((N_KV_HEADS, Q_GROUP, 128)),
                  pl.BlockSpec((nb, N_KV_HEADS, Q_GROUP, HEAD_DIM), lambda i: (i, 0, 0, 0)),
                  pl.BlockSpec((nb, N_KV_HEADS, HEAD_DIM), lambda i: (i, 0, 0)),
                  pl.BlockSpec((nb, N_KV_HEADS, HEAD_DIM), lambda i: (i, 0, 0)),
                  pl.BlockSpec((nb, WINDOW, KV_WIDTH), lambda i: (i, 0, 0)),
                  pl.BlockSpec((nb, WINDOW, KV_WIDTH), lambda i: (i, 0, 0))],
        out_specs=pl.BlockSpec((nb, N_KV_HEADS, Q_GROUP, HEAD_DIM), lambda i: (i, 0, 0, 0)),
        out_shape=jax.ShapeDtypeStruct((DEC_BATCH, N_KV_HEADS, Q_GROUP, HEAD_DIM), BF16),
        compiler_params=_params(1),
        name="attn_sample",
    )(sinks_b, q4, kn3, vn3, kc, vc)


def _outproj_ffn_kernel(x_ref, ca_ref, ob_ref, g_ref, wo_hbm, wg_hbm, wu_hbm, wd_hbm, o_ref,
                        wo_ref, wg_ref, wu_ref, wd_ref, *, layer):
    @pl.when(pl.program_id(0) == 0)
    def _():
        pltpu.sync_copy(wo_hbm.at[layer], wo_ref)
        pltpu.sync_copy(wg_hbm.at[layer], wg_ref)
        pltpu.sync_copy(wu_hbm.at[layer], wu_ref)
        pltpu.sync_copy(wd_hbm.at[layer], wd_ref)

    o_ref[...] = (x_ref[...]
                  + jnp.dot(ca_ref[...], wo_ref[:A_CH, :], preferred_element_type=F32)
                  + jnp.dot(ob_ref[...], wo_ref[A_CH:, :], preferred_element_type=F32))
    h = _rms(o_ref[...], g_ref[...]).astype(BF16)
    ff = D_FF // FF_SPLIT
    for c in range(FF_SPLIT):
        cols = slice(c * ff, (c + 1) * ff)
        gate = jnp.dot(h, wg_ref[:, cols], preferred_element_type=F32)
        up = jnp.dot(h, wu_ref[:, cols], preferred_element_type=F32)
        act = (gate * jax.nn.sigmoid(gate) * up).astype(BF16)
        o_ref[...] += jnp.dot(act, wd_ref[cols, :], preferred_element_type=F32)


def _outproj_ffn(x, ca, ob, wo, g, wg, wu, wd, tm, layer):
    n = x.shape[0]
    row = lambda w: pl.BlockSpec((tm, w), lambda i: (i, 0))
    hbm = pl.BlockSpec(memory_space=pl.ANY)
    return pl.pallas_call(
        functools.partial(_outproj_ffn_kernel, layer=layer),
        grid=(n // tm,),
        in_specs=[row(D_MODEL), row(A_CH), row(B_WIDTH), _const_spec((1, D_MODEL)), hbm, hbm, hbm, hbm],
        out_specs=row(D_MODEL),
        out_shape=jax.ShapeDtypeStruct((n, D_MODEL), F32),
        scratch_shapes=[pltpu.VMEM((D_MODEL, D_MODEL), BF16), pltpu.VMEM((D_MODEL, D_FF), BF16),
                        pltpu.VMEM((D_MODEL, D_FF), BF16), pltpu.VMEM((D_FF, D_MODEL), BF16)],
        compiler_params=_params(1),
        name="outproj_ffn",
    )(x, ca, ob, g, wo, wg, wu, wd)


def _gmlp_prompt_kernel(x_ref, g_ref, win_ref, lg_ref, lb_ref, ws_ref, bs_ref, wout_ref, o_ref, gate_ref):
    x = x_ref[...]
    h = _rms(x, g_ref[...]).astype(BF16)
    z = jax.nn.gelu(jnp.dot(h, win_ref[...], preferred_element_type=F32))
    u = z[:, :C_WIDTH]
    v = _layer_norm(z[:, C_WIDTH:], lg_ref[...], lb_ref[...]).astype(BF16)
    ri = lax.broadcasted_iota(jnp.int32, (C_CHUNK, C_CHUNK), 0)
    ci = lax.broadcasted_iota(jnp.int32, (C_CHUNK, C_CHUNK), 1)
    tm = x.shape[0]
    gd = C_WIDTH // C_GROUPS
    for grp in range(C_GROUPS):
        w = jnp.where(ri >= ci, ws_ref[grp], 0.0).astype(BF16)
        b = bs_ref[:, grp:grp + 1]
        lanes = slice(grp * gd, (grp + 1) * gd)
        for c in range(tm // C_CHUNK):
            rows = slice(c * C_CHUNK, (c + 1) * C_CHUNK)
            mixed = jnp.dot(w, v[rows, lanes], preferred_element_type=F32) + b
            gate_ref[rows, lanes] = (u[rows, lanes] * mixed).astype(BF16)
    o_ref[...] = x + jnp.dot(gate_ref[...], wout_ref[...], preferred_element_type=F32)


def _gmlp_prompt(x, g, win, lg, lb, ws, bs_t, wout, tm):
    n = x.shape[0]
    row = lambda w: pl.BlockSpec((tm, w), lambda i: (i, 0))
    return pl.pallas_call(
        _gmlp_prompt_kernel,
        grid=(n // tm,),
        in_specs=[row(D_MODEL), _const_spec((1, D_MODEL)), _const_spec((D_MODEL, 2 * C_WIDTH)),
                  _const_spec((1, C_WIDTH)), _const_spec((1, C_WIDTH)),
                  _const_spec((C_GROUPS, C_CHUNK, C_CHUNK)), _const_spec((C_CHUNK, C_GROUPS)),
                  _const_spec((C_WIDTH, D_MODEL))],
        out_specs=row(D_MODEL),
        out_shape=jax.ShapeDtypeStruct((n, D_MODEL), F32),
        scratch_shapes=[pltpu.VMEM((tm, C_WIDTH), BF16)],
        compiler_params=_params(1),
        name="gmlp_prompt",
    )(x, g, win, lg, lb, ws, bs_t, wout)


def _gmlp_sample_kernel(x_ref, g_ref, win_ref, lg_ref, lb_ref, w0_ref, b0_ref, wout_ref, o_ref, v_ref):
    x = x_ref[...]
    h = _rms(x, g_ref[...]).astype(BF16)
    z = jax.nn.gelu(jnp.dot(h, win_ref[...], preferred_element_type=F32))
    v = _layer_norm(z[:, C_WIDTH:], lg_ref[...], lb_ref[...])
    v_ref[...] = v
    gated = z[:, :C_WIDTH] * (w0_ref[...] * v + b0_ref[...])
    o_ref[...] = x + jnp.dot(gated.astype(BF16), wout_ref[...], preferred_element_type=F32)


def _gmlp_sample(x, g, win, lg, lb, w0, b0, wout):
    n = x.shape[0]
    return pl.pallas_call(
        _gmlp_sample_kernel,
        grid=(1,),
        in_specs=[_const_spec((n, D_MODEL)), _const_spec((1, D_MODEL)), _const_spec((D_MODEL, 2 * C_WIDTH)),
                  _const_spec((1, C_WIDTH)), _const_spec((1, C_WIDTH)), _const_spec((1, C_WIDTH)),
                  _const_spec((1, C_WIDTH)), _const_spec((C_WIDTH, D_MODEL))],
        out_specs=[_const_spec((n, D_MODEL)), _const_spec((n, C_WIDTH))],
        out_shape=[jax.ShapeDtypeStruct((n, D_MODEL), F32), jax.ShapeDtypeStruct((n, C_WIDTH), F32)],
        compiler_params=_params(1),
        name="gmlp_sample",
    )(x, g, win, lg, lb, w0, b0, wout)


META_W = 8


def _split_bf16(a):
    hi = a.astype(BF16)
    return hi, (a - hi.astype(F32)).astype(BF16)


def _router_kernel(x_ref, g_ref, wr_ref, *rest, bt):
    hn_ref, cm_ref, lst_ref, cnt_ref = rest[-4:]
    h = _rms(x_ref[...], g_ref[...])
    tm = h.shape[0]
    h_hi, h_lo = _split_bf16(h)
    w_hi, w_lo = _split_bf16(wr_ref[...])
    hn_ref[...] = h_hi
    logits = (jnp.dot(h_hi, w_hi, preferred_element_type=F32) + jnp.dot(h_lo, w_hi, preferred_element_type=F32)
              + jnp.dot(h_hi, w_lo, preferred_element_type=F32))
    lane = lax.broadcasted_iota(jnp.int32, (bt, 128), 1).astype(F32)
    neg = -jnp.inf
    ri = lax.broadcasted_iota(jnp.int32, (bt, bt), 0)
    ci = lax.broadcasted_iota(jnp.int32, (bt, bt), 1)
    earlier = jnp.where(ri > ci, 1.0, 0.0).astype(BF16)
    ei = lax.broadcasted_iota(jnp.int32, (128, 128), 0)
    ej = lax.broadcasted_iota(jnp.int32, (128, 128), 1)
    lower_experts = jnp.where(ei < ej, 1.0, 0.0).astype(BF16)
    for j in range(tm // bt):
        rows = slice(j * bt, (j + 1) * bt)
        lg = jnp.where(lane < N_EXPERTS, logits[rows], neg)
        m1 = jnp.max(lg, axis=-1, keepdims=True)
        i1 = jnp.min(jnp.where(lg == m1, lane, 128.0), axis=-1, keepdims=True)
        lg2 = jnp.where(lane == i1, neg, lg)
        m2 = jnp.max(lg2, axis=-1, keepdims=True)
        i2 = jnp.min(jnp.where(lg2 == m2, lane, 128.0), axis=-1, keepdims=True)
        e = jnp.exp(m2 - m1)
        g1 = 1.0 / (1.0 + e)
        g2 = e / (1.0 + e)
        sel1 = lane == i1
        sel2 = lane == i2
        oh = jnp.where(sel1, 1.0, jnp.where(sel2, 1.0, 0.0))
        before = jnp.dot(earlier, oh.astype(BF16), preferred_element_type=F32)
        cnt = jnp.sum(oh, axis=0, keepdims=True)
        padded = jnp.floor((cnt + (SEG_ALIGN - 1)) * (1.0 / SEG_ALIGN)) * SEG_ALIGN
        start = jnp.dot(jnp.broadcast_to(padded, (8, 128)).astype(BF16), lower_experts,
                        preferred_element_type=F32)[0:1]
        slot = before + start
        l1 = jnp.sum(jnp.where(sel1, slot, 0.0), axis=-1, keepdims=True)
        l2 = jnp.sum(jnp.where(sel2, slot, 0.0), axis=-1, keepdims=True)
        rec = jnp.where(lane == 0, l1, jnp.where(lane == 1, l2, jnp.where(lane == 2, g1, jnp.where(lane == 3, g2, 0.0))))
        cm_ref[rows, :] = rec[:, :META_W]
        lst_ref[j] = rec.T[0:2, :].astype(jnp.int32)
        cnt_ref[j] = cnt


def _router(x, g, wr_pad, tm, bt, row_blk0, prev=None):
    n = x.shape[0]
    nblk = tm // bt
    in_specs = [pl.BlockSpec((tm, D_MODEL), lambda i: (i, 0)), _const_spec((1, D_MODEL)),
                _const_spec((D_MODEL, 128))]
    args = [x, g, wr_pad]
    aliases = {}
    if prev is not None:
        in_specs += [pl.BlockSpec(memory_space=pl.ANY), pl.BlockSpec(memory_space=pl.ANY)]
        args += list(prev)
        aliases = {3: 0, 4: 1}
    return pl.pallas_call(
        functools.partial(_router_kernel, bt=bt),
        grid=(n // tm,),
        in_specs=in_specs,
        out_specs=[pl.BlockSpec((tm, D_MODEL), lambda i: (row_blk0 + i, 0)),
                   pl.BlockSpec((tm, META_W), lambda i: (row_blk0 + i, 0)),
                   pl.BlockSpec((nblk, 2, bt), lambda i: (i, 0, 0)),
                   pl.BlockSpec((nblk, 1, 128), lambda i: (i, 0, 0))],
        out_shape=[jax.ShapeDtypeStruct((N_TOK, D_MODEL), BF16), jax.ShapeDtypeStruct((N_TOK, META_W), F32),
                   jax.ShapeDtypeStruct((n // bt, 2, bt), jnp.int32), jax.ShapeDtypeStruct((n // bt, 1, 128), F32)],
        input_output_aliases=aliases,
        compiler_params=_params(1),
        name="moe_router",
    )(*args)


def _local_rows(bt):
    return 2 * bt + N_EXPERTS * SEG_ALIGN


def _segment_copies(seg_n, seg_src, seg_dst, blk, local_ref, slot_ref, sem, bt, to_slots, action):
    def per_expert(e, c):
        s = blk * N_EXPERTS + e
        n, src, dst = seg_n[s], seg_src[s], seg_dst[s]
        off = jnp.int32(0)
        p = bt
        while p >= SEG_ALIGN:
            hit = (n & p) != 0

            @pl.when(hit)
            def _(off=off, p=p):
                a = local_ref.at[pl.ds(pl.multiple_of(src + off, SEG_ALIGN), p)]
                b = slot_ref.at[pl.ds(pl.multiple_of(dst + off, SEG_ALIGN), p)]
                action(pltpu.make_async_copy(a, b, sem) if to_slots else pltpu.make_async_copy(b, a, sem))

            off = off + (n & p)
            p //= 2
        return c

    lax.fori_loop(0, N_EXPERTS, per_expert, 0)


def _dispatch_kernel(seg_n, seg_src, seg_dst, hn_ref, ls_ref, *rest, bt, blk0, nb):
    slots_ref, buf_ref, sem = rest[-3:]
    i = pl.program_id(0)
    par = i % 2
    start = lambda cp: cp.start()
    wait = lambda cp: cp.wait()
    copies = lambda blk, p, action: _segment_copies(seg_n, seg_src, seg_dst, blk, buf_ref.at[p], slots_ref,
                                                     sem.at[p], bt, True, action)

    @pl.when(i >= 2)
    def _():
        copies(blk0 + i - 2, par, wait)

    rows = lax.broadcasted_iota(jnp.int32, (_local_rows(bt), bt), 0)
    ls = ls_ref[0]
    onehot = jnp.where(rows == ls[0:1, :], 1.0, jnp.where(rows == ls[1:2, :], 1.0, 0.0)).astype(BF16)
    buf_ref[par] = jnp.dot(onehot, hn_ref[...], preferred_element_type=F32)
    copies(blk0 + i, par, start)

    @pl.when(i == nb - 1)
    def _():
        if nb >= 2:
            copies(blk0 + i - 1, 1 - par, wait)
        copies(blk0 + i, par, wait)


def _dispatch(seg, hn, ls_t, bt, blk0, tok0, prev=None):
    nb = ls_t.shape[0]
    in_specs = [pl.BlockSpec((bt, D_MODEL), lambda i, *_: (tok0 // bt + i, 0)),
                pl.BlockSpec((1, 2, bt), lambda i, *_: (i, 0, 0))]
    args = [hn, ls_t]
    aliases = {}
    if prev is not None:
        in_specs.append(pl.BlockSpec(memory_space=pl.ANY))
        args.append(prev)
        aliases = {5: 0}
    return pl.pallas_call(
        functools.partial(_dispatch_kernel, bt=bt, blk0=blk0, nb=nb),
        grid_spec=pltpu.PrefetchScalarGridSpec(
            num_scalar_prefetch=3,
            grid=(nb,),
            in_specs=in_specs,
            out_specs=pl.BlockSpec(memory_space=pl.ANY),
            scratch_shapes=[pltpu.VMEM((2, _local_rows(bt), D_MODEL), F32), pltpu.SemaphoreType.DMA((2,))],
        ),
        out_shape=jax.ShapeDtypeStruct((N_SLOTS, D_MODEL), F32),
        input_output_aliases=aliases,
        compiler_params=_params(1),
        name="moe_dispatch",
    )(*seg, *args)


def _experts_kernel(te_ref, tb_ref, nv_ref, x_ref, wg_ref, wu_ref, wd_ref, o_ref, xb_ref):
    t = pl.program_id(0)
    f = pl.program_id(1)
    nv = nv_ref[t]

    @pl.when(jnp.logical_and(nv > 0, f == 0))
    def _():
        rows = lax.broadcasted_iota(jnp.int32, (MOE_TILE, 1), 0)
        xb_ref[...] = jnp.where(rows < nv, x_ref[...], 0.0).astype(BF16)
        o_ref[...] = jnp.zeros_like(o_ref)

    def swiglu_rows(n_rows):
        rows = slice(0, n_rows)
        xb = xb_ref[rows, :]
        for c in range(MOE_FF_TILE // MOE_FF_SUB):
            cols = slice(c * MOE_FF_SUB, (c + 1) * MOE_FF_SUB)
            gate = jnp.dot(xb, wg_ref[:, cols].astype(BF16), preferred_element_type=F32)
            up = jnp.dot(xb, wu_ref[:, cols].astype(BF16), preferred_element_type=F32)
            act = (gate * jax.nn.sigmoid(gate) * up).astype(BF16)
            o_ref[rows, :] += jnp.dot(act, wd_ref[cols, :].astype(BF16), preferred_element_type=F32)

    lo = 0
    for n_rows in MOE_ROW_PATHS:
        @pl.when(jnp.logical_and(nv > lo, nv <= n_rows))
        def _(n_rows=n_rows):
            swiglu_rows(n_rows)
        lo = n_rows


def _experts(tile_expert, tile_blk, tile_rows, xs, wg, wu, wd, layer):
    def chunk(t, f, nv):
        return jnp.where(nv[t] > 0, f, MOE_NF - 1)
    up_spec = pl.BlockSpec((None, None, D_MODEL, MOE_FF_TILE),
                           lambda t, f, te, tb, nv: (layer, te[t], 0, chunk(t, f, nv)))
    down_spec = pl.BlockSpec((None, None, MOE_FF_TILE, D_MODEL),
                             lambda t, f, te, tb, nv: (layer, te[t], chunk(t, f, nv), 0))
    return pl.pallas_call(
        _experts_kernel,
        grid_spec=pltpu.PrefetchScalarGridSpec(
            num_scalar_prefetch=3,
            grid=(MOE_MAX_TILES, MOE_NF),
            in_specs=[pl.BlockSpec((MOE_TILE, D_MODEL), lambda t, f, te, tb, nv: (tb[t], 0)),
                      up_spec, up_spec, down_spec],
            out_specs=pl.BlockSpec((MOE_TILE, D_MODEL), lambda t, f, te, tb, nv: (tb[t], 0)),
            scratch_shapes=[pltpu.VMEM((MOE_TILE, D_MODEL), BF16)],
        ),
        out_shape=jax.ShapeDtypeStruct((N_SLOTS, D_MODEL), F32),
        compiler_params=_params(2),
        name="moe_experts",
    )(tile_expert, tile_blk, tile_rows, xs, wg, wu, wd)


def _combine_kernel(seg_n, seg_src, seg_dst, x_ref, cm_ref, eo_ref, y_ref, buf_ref, sem, *, bt, blk0, nb):
    i = pl.program_id(0)
    par = i % 2
    start = lambda cp: cp.start()
    wait = lambda cp: cp.wait()
    copies = lambda blk, p, action: _segment_copies(seg_n, seg_src, seg_dst, blk, buf_ref.at[p], eo_ref,
                                                     sem.at[p], bt, False, action)

    @pl.when(i == 0)
    def _():
        copies(blk0, 0, start)

    @pl.when(i + 1 < nb)
    def _():
        copies(blk0 + i + 1, 1 - par, start)

    copies(blk0 + i, par, wait)
    last = (blk0 + i) * N_EXPERTS + (N_EXPERTS - 1)
    used = seg_src[last] + seg_n[last]
    rows = lax.broadcasted_iota(jnp.int32, (_local_rows(bt), 1), 0)
    e_hi, e_lo = _split_bf16(jnp.where(rows < used, buf_ref[par], 0.0))
    cm = cm_ref[...]
    cols = lax.broadcasted_iota(jnp.int32, (bt, _local_rows(bt)), 1).astype(F32)
    comb = jnp.where(cols == cm[:, 0:1], cm[:, 2:3], jnp.where(cols == cm[:, 1:2], cm[:, 3:4], 0.0))
    c_hi, c_lo = _split_bf16(comb)
    y_ref[...] = x_ref[...] + jnp.dot(c_hi, e_hi, preferred_element_type=F32)
    y_ref[...] += jnp.dot(c_lo, e_hi, preferred_element_type=F32)
    y_ref[...] += jnp.dot(c_hi, e_lo, preferred_element_type=F32)


def _combine(seg, x, cm, eo, bt, blk0, tok0):
    n = x.shape[0]
    nb = n // bt
    return pl.pallas_call(
        functools.partial(_combine_kernel, bt=bt, blk0=blk0, nb=nb),
        grid_spec=pltpu.PrefetchScalarGridSpec(
            num_scalar_prefetch=3,
            grid=(nb,),
            in_specs=[pl.BlockSpec((bt, D_MODEL), lambda i, *_: (i, 0)),
                      pl.BlockSpec((bt, META_W), lambda i, *_: (tok0 // bt + i, 0)),
                      pl.BlockSpec(memory_space=pl.ANY)],
            out_specs=pl.BlockSpec((bt, D_MODEL), lambda i, *_: (i, 0)),
            scratch_shapes=[pltpu.VMEM((2, _local_rows(bt), D_MODEL), F32), pltpu.SemaphoreType.DMA((2,))],
        ),
        out_shape=jax.ShapeDtypeStruct((n, D_MODEL), F32),
        compiler_params=_params(1),
        name="moe_combine",
    )(*seg, x, cm, eo)


def _moe(xp, xs, g, w_router, wg, wu, wd, layer):
    g2 = g.reshape(1, D_MODEL)
    wr_pad = jnp.pad(w_router, ((0, 0), (0, 128 - N_EXPERTS)))
    hn, cm, ls_p, cnt_p = _router(xp, g2, wr_pad, ROW_TILE, MOE_BLOCK, 0)
    hn, cm, ls_s, cnt_s = _router(xs, g2, wr_pad, DEC_BATCH, DEC_BATCH, N_PROMPT // DEC_BATCH, prev=(hn, cm))

    nbp = N_PROMPT // MOE_BLOCK
    seg_cnt = jnp.concatenate([cnt_p[:, 0, :N_EXPERTS], cnt_s[:, 0, :N_EXPERTS]], axis=0).astype(jnp.int32)
    seg_n = (seg_cnt + (SEG_ALIGN - 1)) // SEG_ALIGN * SEG_ALIGN
    seg_src = jnp.cumsum(seg_n, axis=1) - seg_n
    slot0 = jnp.cumsum(seg_n, axis=0) - seg_n

    counts = seg_n.sum(axis=0)
    tiles_e = (counts + (MOE_TILE - 1)) // MOE_TILE
    tile_end = jnp.cumsum(tiles_e)
    tile_start = tile_end - tiles_e
    n_tiles = tile_end[-1]
    t = jnp.arange(MOE_MAX_TILES, dtype=jnp.int32)
    tc = jnp.minimum(t, n_tiles - 1)
    tile_expert = jnp.sum((tc[:, None] >= tile_end[None, :]).astype(jnp.int32), axis=1)
    rows_left = counts[tile_expert] - (tc - tile_start[tile_expert]) * MOE_TILE
    tile_rows = jnp.where(t < n_tiles, jnp.minimum(rows_left, MOE_TILE), 0).astype(jnp.int32)
    seg_dst = (tile_start * MOE_TILE)[None, :] + slot0

    cap = jnp.concatenate([jnp.full((nbp, 1), MOE_BLOCK, jnp.int32), jnp.full((1, 1), DEC_BATCH, jnp.int32)])
    seg_n = jnp.clip(seg_n, 0, cap)
    seg = (seg_n.reshape(-1), jnp.clip(seg_src, 0, 2 * cap + N_EXPERTS * SEG_ALIGN - seg_n).reshape(-1),
           jnp.clip(seg_dst, 0, N_SLOTS - MOE_BLOCK).reshape(-1))

    slots = _dispatch(seg, hn, ls_p, MOE_BLOCK, 0, 0)
    slots = _dispatch(seg, hn, ls_s, DEC_BATCH, nbp, N_PROMPT, prev=slots)
    eo = _experts(tile_expert.astype(jnp.int32), tc, tile_rows, slots, wg, wu, wd, layer)
    yp = _combine(seg, xp, cm, eo, MOE_BLOCK, 0, 0)
    ys = _combine(seg, xs, cm, eo, DEC_BATCH, nbp, N_PROMPT)
    return yp, ys


def kernel(x_prompt, x_sample, state_conv, cache_k, cache_v, norm_mix_g, norm_ffn_g, w_in_ab, conv_w, conv_b,
           conv_ln_g, conv_ln_b, q_norm_g, k_norm_g, attn_sinks, w_out_ab, w_gate_dense, w_up_dense, w_down_dense,
           w_in_c, c_ln_g, c_ln_b, w_spatial, b_spatial, w_out_c, w_router, w_gate_exp, w_up_exp, w_down_exp):
    xp = x_prompt.reshape(N_PROMPT, D_MODEL)
    xs = x_sample.reshape(DEC_BATCH, D_MODEL)
    head = jnp.arange(B_WIDTH) // HEAD_DIM
    seg = (head[:, None] == head[None, :]).astype(BF16)
    conv_p, conv_s, kp_out, vp_out, ks_out, vs_out, vc_out = [], [], [], [], [], [], []
    row = lambda a: a.reshape(1, -1)
    wo_all, wg_all = w_out_ab.astype(BF16), w_gate_dense.astype(BF16)
    wu_all, wd_all = w_up_dense.astype(BF16), w_down_dense.astype(BF16)
    for layer in range(DEPTH):
        i = layer // 2
        g_mix = row(norm_mix_g[layer])
        g_ffn = row(norm_ffn_g[layer])
        if layer % 2 == 0:
            w_in = w_in_ab[i].astype(BF16)
            qg = row(jnp.tile(q_norm_g[i], N_HEADS))
            kg = row(jnp.tile(k_norm_g[i], N_KV_HEADS))
            a_p, q_p, k_p, v_p = _inproj(xp, g_mix, w_in, qg, kg, seg, ROW_TILE)
            a_s, q_s, k_s, v_s = _inproj(xs, g_mix, w_in, qg, kg, seg, DEC_BATCH)
            cb, lg, lb = row(conv_b[i]), row(conv_ln_g[i]), row(conv_ln_b[i])
            ca_p = _conv_prompt(a_p, conv_w[i], cb, lg, lb)
            ca_s = _conv_sample(a_s, state_conv[i], conv_w[i], cb, lg, lb)
            ob_p = _attn_prompt(q_p, k_p, v_p, attn_sinks[i])
            sinks_b = jnp.broadcast_to(attn_sinks[i].reshape(N_KV_HEADS, Q_GROUP, 1), (N_KV_HEADS, Q_GROUP, 128))
            ob_s = _attn_sample(q_s.reshape(DEC_BATCH, N_KV_HEADS, Q_GROUP, HEAD_DIM),
                                k_s.reshape(DEC_BATCH, N_KV_HEADS, HEAD_DIM),
                                v_s.reshape(DEC_BATCH, N_KV_HEADS, HEAD_DIM),
                                cache_k[i].reshape(DEC_BATCH, WINDOW, KV_WIDTH),
                                cache_v[i].reshape(DEC_BATCH, WINDOW, KV_WIDTH), sinks_b)
            ob_s = ob_s.reshape(DEC_BATCH, B_WIDTH)
            conv_p.append(a_p.reshape(BATCH, SEQ, A_CH)[:, SEQ - (CONV_W - 1):])
            conv_s.append(jnp.concatenate([state_conv[i][:, 1:], a_s[:, None, :]], axis=1))
            last = lambda t: (t.reshape(BATCH, SEQ, KV_WIDTH)[:, SEQ - WINDOW:]
                              .reshape(BATCH, WINDOW, N_KV_HEADS, HEAD_DIM))
            kp_out.append(last(k_p))
            vp_out.append(last(v_p))
            ks_out.append(jnp.concatenate(
                [cache_k[i][:, 1:], k_s.reshape(DEC_BATCH, 1, N_KV_HEADS, HEAD_DIM)], axis=1))
            vs_out.append(jnp.concatenate(
                [cache_v[i][:, 1:], v_s.reshape(DEC_BATCH, 1, N_KV_HEADS, HEAD_DIM)], axis=1))
            xp = _outproj_ffn(xp, ca_p, ob_p, wo_all, g_ffn, wg_all, wu_all, wd_all, ROW_TILE, i)
            xs = _outproj_ffn(xs, ca_s, ob_s, wo_all, g_ffn, wg_all, wu_all, wd_all, DEC_BATCH, i)
        else:
            win = w_in_c[i].astype(BF16)
            wout = w_out_c[i].astype(BF16)
            lg, lb = row(c_ln_g[i]), row(c_ln_b[i])
            xp = _gmlp_prompt(xp, g_mix, win, lg, lb, w_spatial[i], b_spatial[i].T, wout, ROW_TILE)
            gd = C_WIDTH // C_GROUPS
            w0 = row(jnp.repeat(w_spatial[i][:, 0, 0], gd))
            b0 = row(jnp.repeat(b_spatial[i][:, 0], gd))
            xs, v_s = _gmlp_sample(xs, g_mix, win, lg, lb, w0, b0, wout)
            vc_out.append(v_s.reshape(DEC_BATCH, 1, C_WIDTH))
            xp, xs = _moe(xp, xs, norm_ffn_g[layer], w_router[i], w_gate_exp, w_up_exp, w_down_exp, i)
    return (xp.reshape(BATCH, SEQ, D_MODEL), xs.reshape(DEC_BATCH, 1, D_MODEL),
            jnp.stack(conv_p), jnp.stack(conv_s), jnp.stack(kp_out), jnp.stack(vp_out),
            jnp.stack(ks_out), jnp.stack(vs_out), jnp.stack(vc_out))
```

```python
import functools

import jax
import jax.numpy as jnp
from jax import lax
from jax.experimental import pallas as pl
from jax.experimental.pallas import tpu as pltpu

F32 = jnp.float32
BF16 = jnp.bfloat16

D_MODEL = 1024
BATCH = 8
SEQ = 2048
DEPTH = 4
DEC_BATCH = 128
A_CH = 512
CONV_W = 31
N_HEADS = 8
N_KV_HEADS = 2
HEAD_DIM = 64
Q_GROUP = 4
WINDOW = 128
B_WIDTH = 512
KV_WIDTH = 128
AB_IN = 2 * A_CH + B_WIDTH + 2 * KV_WIDTH
C_WIDTH = 1024
C_GROUPS = 8
C_CHUNK = 128
D_FF = 2816
N_EXPERTS = 8
D_FF_EXPERT = 3584
EPS = 1e-6
ATTN_SCALE = HEAD_DIM ** -0.5

N_PROMPT = BATCH * SEQ
N_TOK = N_PROMPT + DEC_BATCH
ROW_TILE = 1024
FF_SPLIT = 11
MOE_TILE = 1024
MOE_FF_TILE = 512
MOE_FF_SUB = 256
MOE_ROW_PATHS = (256, 512, MOE_TILE)
MOE_NF = D_FF_EXPERT // MOE_FF_TILE
MOE_BLOCK = 256
MOE_N_BLOCKS = N_PROMPT // MOE_BLOCK + 1
SEG_ALIGN = 8
MOE_MAX_TILES = (2 * N_TOK + MOE_N_BLOCKS * N_EXPERTS * (SEG_ALIGN - 1) + N_EXPERTS * (MOE_TILE - 1)) // MOE_TILE
N_SLOTS = MOE_MAX_TILES * MOE_TILE
VMEM_LIMIT = 56 * 1024 * 1024


def _params(n_axes, vmem=VMEM_LIMIT):
    return pltpu.CompilerParams(dimension_semantics=("arbitrary",) * n_axes, vmem_limit_bytes=vmem)


def _rms(x, g):
    return x * lax.rsqrt(jnp.mean(x * x, axis=-1, keepdims=True) + EPS) * g


def _layer_norm(x, g, b):
    mu = jnp.mean(x, axis=-1, keepdims=True)
    xc = x - mu
    var = jnp.mean(xc * xc, axis=-1, keepdims=True)
    return xc * lax.rsqrt(var + EPS) * g + b


def _const_spec(shape):
    return pl.BlockSpec(shape, lambda *_: (0,) * len(shape))


def _inproj_kernel(x_ref, g_ref, w_ref, qg_ref, kg_ref, seg_ref, a_ref, q_ref, k_ref, v_ref):
    h = _rms(x_ref[...], g_ref[...]).astype(BF16)
    z = jnp.dot(h, w_ref[...], preferred_element_type=F32)
    a_ref[...] = z[:, :A_CH] * jax.nn.sigmoid(z[:, A_CH:2 * A_CH])
    seg = seg_ref[...]

    def head_norm(t, gain, seg_t):
        sq = t * t
        hi = sq.astype(BF16)
        lo = (sq - hi.astype(F32)).astype(BF16)
        ms = (jnp.dot(hi, seg_t, preferred_element_type=F32)
              + jnp.dot(lo, seg_t, preferred_element_type=F32)) * (1.0 / HEAD_DIM)
        return t * lax.rsqrt(ms + EPS) * gain

    o = 2 * A_CH
    q = head_norm(z[:, o:o + B_WIDTH], qg_ref[...], seg)
    q_ref[...] = (q * ATTN_SCALE).astype(BF16)
    o += B_WIDTH
    k_ref[...] = head_norm(z[:, o:o + KV_WIDTH], kg_ref[...], seg[:KV_WIDTH, :KV_WIDTH])
    o += KV_WIDTH
    v_ref[...] = z[:, o:o + KV_WIDTH]


def _inproj(x, g, w_bf, qg, kg, seg, tm):
    n = x.shape[0]
    row = lambda w: pl.BlockSpec((tm, w), lambda i: (i, 0))
    return pl.pallas_call(
        _inproj_kernel,
        grid=(n // tm,),
        in_specs=[row(D_MODEL), _const_spec((1, D_MODEL)), _const_spec((D_MODEL, AB_IN)),
                  _const_spec((1, B_WIDTH)), _const_spec((1, KV_WIDTH)), _const_spec((B_WIDTH, B_WIDTH))],
        out_specs=[row(A_CH), row(B_WIDTH), row(KV_WIDTH), row(KV_WIDTH)],
        out_shape=[jax.ShapeDtypeStruct((n, A_CH), F32), jax.ShapeDtypeStruct((n, B_WIDTH), BF16),
                   jax.ShapeDtypeStruct((n, KV_WIDTH), F32), jax.ShapeDtypeStruct((n, KV_WIDTH), F32)],
        compiler_params=_params(1),
        name="inproj_ab",
    )(x, g, w_bf, qg, kg, seg)


CONV_PAD = 32
CONV_ROWS = 128
CONV_SUB = 32


def _conv_tail(c, lg, lb):
    y = _layer_norm(c, lg, lb)
    return y * jax.nn.sigmoid(y)


def _conv_prompt_kernel(a_ref, w_ref, b_ref, lg_ref, lb_ref, o_ref, pad_ref, sh_ref, c_ref):
    pad_ref[0:CONV_PAD, :] = jnp.zeros((CONV_PAD, A_CH), F32)
    pad_ref[CONV_PAD:, :] = a_ref[...]
    win = CONV_ROWS + CONV_PAD

    def body(r, carry):
        t0 = pl.multiple_of(r * CONV_ROWS, CONV_ROWS)
        nc = A_CH // 128
        for c in range(nc):
            sh_ref[0, c] = pad_ref[pl.ds(t0, win), c * 128:(c + 1) * 128]
        for s in range(1, 8):
            for c in range(nc):
                sh_ref[s, c, 0:win - 8, :] = sh_ref[0, c, s:s + win - 8, :]
        first = CONV_PAD - (CONV_W - 1)

        def chunk(c, carry2):
            for r0 in range(0, CONV_ROWS, CONV_SUB):
                acc = jnp.zeros((CONV_SUB, 128), F32)
                for s in range(8):
                    taps = [k for k in range(CONV_W) if (first + k) % 8 == s]
                    span = max(first + k - s for k in taps) + CONV_SUB
                    rows = sh_ref[s, c, r0:r0 + span, :]
                    for k in taps:
                        base = first + k - s
                        acc = acc + rows[base:base + CONV_SUB] * w_ref[c, k:k + 1, :]
                c_ref[c, r0:r0 + CONV_SUB, :] = acc
            return carry2

        lax.fori_loop(0, nc, chunk, 0)
        conv = jnp.concatenate([c_ref[c] for c in range(nc)], axis=-1)
        y = _conv_tail(conv + b_ref[...], lg_ref[...], lb_ref[...])
        o_ref[pl.ds(t0, CONV_ROWS), :] = y.astype(BF16)
        return carry

    lax.fori_loop(0, SEQ // CONV_ROWS, body, 0)


def _conv_prompt(a, w, b, lg, lb):
    nc = A_CH // 128
    w_chunks = w.reshape(CONV_W, nc, 128).transpose(1, 0, 2)
    return pl.pallas_call(
        _conv_prompt_kernel,
        grid=(BATCH,),
        in_specs=[pl.BlockSpec((SEQ, A_CH), lambda i: (i, 0)), _const_spec((nc, CONV_W, 128)),
                  _const_spec((1, A_CH)), _const_spec((1, A_CH)), _const_spec((1, A_CH))],
        out_specs=pl.BlockSpec((SEQ, A_CH), lambda i: (i, 0)),
        out_shape=jax.ShapeDtypeStruct((N_PROMPT, A_CH), BF16),
        scratch_shapes=[pltpu.VMEM((CONV_PAD + SEQ, A_CH), F32),
                        pltpu.VMEM((8, nc, CONV_ROWS + CONV_PAD, 128), F32),
                        pltpu.VMEM((nc, CONV_ROWS, 128), F32)],
        compiler_params=_params(1),
        name="conv_prompt",
    )(a, w_chunks, b, lg, lb)


def _conv_sample_kernel(a_ref, st_ref, w_ref, b_ref, lg_ref, lb_ref, o_ref):
    acc = a_ref[...] * w_ref[CONV_W - 1:CONV_W, :]
    acc = acc + jnp.sum(st_ref[...] * w_ref[0:CONV_W - 1, :][None], axis=1)
    o_ref[...] = _conv_tail(acc + b_ref[...], lg_ref[...], lb_ref[...]).astype(BF16)


def _conv_sample(a, st_t, w, b, lg, lb):
    return pl.pallas_call(
        _conv_sample_kernel,
        grid=(1,),
        in_specs=[_const_spec((DEC_BATCH, A_CH)), _const_spec((DEC_BATCH, CONV_W - 1, A_CH)),
                  _const_spec((CONV_W, A_CH)), _const_spec((1, A_CH)), _const_spec((1, A_CH)),
                  _const_spec((1, A_CH))],
        out_specs=_const_spec((DEC_BATCH, A_CH)),
        out_shape=jax.ShapeDtypeStruct((DEC_BATCH, A_CH), BF16),
        compiler_params=_params(1),
        name="conv_sample",
    )(a, st_t, w, b, lg, lb)


def _alibi_slope(h):
    return 2.0 ** (-8.0 * (h + 1) / N_HEADS)


ATTN_P_WINDOWS = 4


def _attn_prompt_kernel(sink_ref, q_ref, kp_ref, kc_ref, vp_ref, vc_ref, o_ref):
    step = pl.program_id(1)
    kk = jnp.concatenate([kp_ref[...], kc_ref[...]], axis=0).astype(BF16)
    vv = jnp.concatenate([vp_ref[...], vc_ref[...]], axis=0).astype(BF16)
    qi = lax.broadcasted_iota(jnp.int32, (WINDOW, 2 * WINDOW), 0)
    kj = lax.broadcasted_iota(jnp.int32, (WINDOW, 2 * WINDOW), 1)
    dist_i = WINDOW + qi - kj
    in_window = (dist_i >= 0) & (dist_i < WINDOW)
    lo = jnp.where(step > 0, 0, WINDOW)
    dist = dist_i.astype(F32)
    for w in range(ATTN_P_WINDOWS):
        valid = in_window & (kj >= lo) if w == 0 else in_window
        rows = slice(w * WINDOW, (w + 1) * WINDOW)
        keys = slice(w * WINDOW, (w + 2) * WINDOW)
        outs = []
        for h in range(N_HEADS):
            kv = h // Q_GROUP
            lanes = slice(kv * HEAD_DIM, (kv + 1) * HEAD_DIM)
            qh = q_ref[rows, h * HEAD_DIM:(h + 1) * HEAD_DIM]
            s = lax.dot_general(qh, kk[keys, lanes], (((1,), (1,)), ((), ())), preferred_element_type=F32)
            s = jnp.where(valid, s - _alibi_slope(h) * dist, -jnp.inf)
            sink = sink_ref[h]
            m = jnp.maximum(jnp.max(s, axis=-1, keepdims=True), sink)
            p = jnp.exp(s - m)
            denom = jnp.sum(p, axis=-1, keepdims=True) + jnp.exp(sink - m)
            p = (p / denom).astype(BF16)
            outs.append(jnp.dot(p, vv[keys, lanes], preferred_element_type=F32))
        o_ref[rows, :] = jnp.concatenate(outs, axis=-1).astype(BF16)


def _attn_prompt(q, k, v, sinks):
    rows = ATTN_P_WINDOWS * WINDOW
    ns = SEQ // rows
    own = lambda b, j: (b * ns + j, 0)
    prev = lambda b, j: (b * (SEQ // WINDOW) + jnp.maximum(j * ATTN_P_WINDOWS - 1, 0), 0)
    return pl.pallas_call(
        _attn_prompt_kernel,
        grid=(BATCH, ns),
        in_specs=[pl.BlockSpec(memory_space=pltpu.SMEM),
                  pl.BlockSpec((rows, B_WIDTH), own),
                  pl.BlockSpec((WINDOW, KV_WIDTH), prev), pl.BlockSpec((rows, KV_WIDTH), own),
                  pl.BlockSpec((WINDOW, KV_WIDTH), prev), pl.BlockSpec((rows, KV_WIDTH), own)],
        out_specs=pl.BlockSpec((rows, B_WIDTH), own),
        out_shape=jax.ShapeDtypeStruct((N_PROMPT, B_WIDTH), BF16),
        compiler_params=_params(2),
        name="attn_prompt",
    )(sinks, q, k, k, v, v)


ATTN_S_BLK = 32


def _attn_sample_kernel(sink_ref, q_ref, kn_ref, vn_ref, kc_ref, vc_ref, o_ref):
    kc = kc_ref[...].astype(BF16)
    vc = vc_ref[...].astype(BF16)
    j = lax.broadcasted_iota(jnp.int32, (1, 1, WINDOW), 2)
    dist = (WINDOW - j).astype(F32)
    g_idx = lax.broadcasted_iota(jnp.int32, (1, Q_GROUP, 1), 1)
    for kv in range(N_KV_HEADS):
        lanes = slice(kv * HEAD_DIM, (kv + 1) * HEAD_DIM)
        qg = q_ref[:, kv]
        slope = jnp.exp2(-(g_idx + (kv * Q_GROUP + 1)).astype(F32) * (8.0 / N_HEADS))
        s = jnp.einsum('ngd,njd->ngj', qg, kc[:, :, lanes], preferred_element_type=F32)
        s = jnp.where(j >= 1, s - slope * dist, -jnp.inf)
        kn = kn_ref[:, kv].astype(BF16).astype(F32)
        s_new = jnp.sum(qg.astype(F32) * kn[:, None, :], axis=-1, keepdims=True)
        sink = sink_ref[kv][:, :1][None]
        m = jnp.maximum(jnp.maximum(jnp.max(s, axis=-1, keepdims=True), s_new), sink)
        p = jnp.exp(s - m)
        p_new = jnp.exp(s_new - m)
        denom = jnp.sum(p, axis=-1, keepdims=True) + p_new + jnp.exp(sink - m)
        o = jnp.einsum('ngj,njd->ngd', (p / denom).astype(BF16), vc[:, :, lanes], preferred_element_type=F32)
        vn = vn_ref[:, kv].astype(BF16).astype(F32)
        o = o + (p_new / denom).astype(BF16).astype(F32) * vn[:, None, :]
        o_ref[:, kv] = o.astype(BF16)


def _attn_sample(q4, kn3, vn3, kc, vc, sinks_b):
    nb = ATTN_S_BLK
    return pl.pallas_call(
        _attn_sample_kernel,
        grid=(DEC_BATCH // nb,),
        in_specs=[_const_spec((N_KV_HEADS, Q_GROUP, 128)),
                  pl.BlockSpec((nb, N_KV_HEADS, Q_GROUP, HEAD_DIM), lambda i: (i, 0, 0, 0)),
                  pl.BlockSpec((nb, N_KV_HEADS, HEAD_DIM), lambda i: (i, 0, 0)),
                  pl.BlockSpec((nb, N_KV_HEADS, HEAD_DIM), lambda i: (i, 0, 0)),
                  pl.BlockSpec((nb, WINDOW, KV_WIDTH), lambda i: (i, 0, 0)),
                  pl.BlockSpec((nb, WINDOW, KV_WIDTH), lambda i: (i, 0, 0))],
        out_specs=pl.BlockSpec((nb, N_KV_HEADS, Q_GROUP, HEAD_DIM), lambda i: (i, 0, 0, 0)),
        out_shape=jax.ShapeDtypeStruct((DEC_BATCH, N_KV_HEADS, Q_GROUP, HEAD_DIM), BF16),
        compiler_params=_params(1),
        name="attn_sample",
    )(sinks_b, q4, kn3, vn3, kc, vc)


def _outproj_ffn_kernel(x_ref, ca_ref, ob_ref, g_ref, wo_hbm, wg_hbm, wu_hbm, wd_hbm, o_ref,
                        wo_ref, wg_ref, wu_ref, wd_ref, *, layer):
    @pl.when(pl.program_id(0) == 0)
    def _():
        pltpu.sync_copy(wo_hbm.at[layer], wo_ref)
        pltpu.sync_copy(wg_hbm.at[layer], wg_ref)
        pltpu.sync_copy(wu_hbm.at[layer], wu_ref)
        pltpu.sync_copy(wd_hbm.at[layer], wd_ref)

    o_ref[...] = (x_ref[...]
                  + jnp.dot(ca_ref[...], wo_ref[:A_CH, :], preferred_element_type=F32)
                  + jnp.dot(ob_ref[...], wo_ref[A_CH:, :], preferred_element_type=F32))
    h = _rms(o_ref[...], g_ref[...]).astype(BF16)
    ff = D_FF // FF_SPLIT
    for c in range(FF_SPLIT):
        cols = slice(c * ff, (c + 1) * ff)
        gate = jnp.dot(h, wg_ref[:, cols], preferred_element_type=F32)
        up = jnp.dot(h, wu_ref[:, cols], preferred_element_type=F32)
        act = (gate * jax.nn.sigmoid(gate) * up).astype(BF16)
        o_ref[...] += jnp.dot(act, wd_ref[cols, :], preferred_element_type=F32)


def _outproj_ffn(x, ca, ob, wo, g, wg, wu, wd, tm, layer):
    n = x.shape[0]
    row = lambda w: pl.BlockSpec((tm, w), lambda i: (i, 0))
    hbm = pl.BlockSpec(memory_space=pl.ANY)
    return pl.pallas_call(
        functools.partial(_outproj_ffn_kernel, layer=layer),
        grid=(n // tm,),
        in_specs=[row(D_MODEL), row(A_CH), row(B_WIDTH), _const_spec((1, D_MODEL)), hbm, hbm, hbm, hbm],
        out_specs=row(D_MODEL),
        out_shape=jax.ShapeDtypeStruct((n, D_MODEL), F32),
        scratch_shapes=[pltpu.VMEM((D_MODEL, D_MODEL), BF16), pltpu.VMEM((D_MODEL, D_FF), BF16),
                        pltpu.VMEM((D_MODEL, D_FF), BF16), pltpu.VMEM((D_FF, D_MODEL), BF16)],
        compiler_params=_params(1),
        name="outproj_ffn",
    )(x, ca, ob, g, wo, wg, wu, wd)


def _gmlp_prompt_kernel(x_ref, g_ref, win_ref, lg_ref, lb_ref, ws_ref, bs_ref, wout_ref, o_ref, gate_ref):
    x = x_ref[...]
    h = _rms(x, g_ref[...]).astype(BF16)
    z = jax.nn.gelu(jnp.dot(h, win_ref[...], preferred_element_type=F32))
    u = z[:, :C_WIDTH]
    v = _layer_norm(z[:, C_WIDTH:], lg_ref[...], lb_ref[...]).astype(BF16)
    ri = lax.broadcasted_iota(jnp.int32, (C_CHUNK, C_CHUNK), 0)
    ci = lax.broadcasted_iota(jnp.int32, (C_CHUNK, C_CHUNK), 1)
    tm = x.shape[0]
    gd = C_WIDTH // C_GROUPS
    for grp in range(C_GROUPS):
        w = jnp.where(ri >= ci, ws_ref[grp], 0.0).astype(BF16)
        b = bs_ref[:, grp:grp + 1]
        lanes = slice(grp * gd, (grp + 1) * gd)
        for c in range(tm // C_CHUNK):
            rows = slice(c * C_CHUNK, (c + 1) * C_CHUNK)
            mixed = jnp.dot(w, v[rows, lanes], preferred_element_type=F32) + b
            gate_ref[rows, lanes] = (u[rows, lanes] * mixed).astype(BF16)
    o_ref[...] = x + jnp.dot(gate_ref[...], wout_ref[...], preferred_element_type=F32)


def _gmlp_prompt(x, g, win, lg, lb, ws, bs_t, wout, tm):
    n = x.shape[0]
    row = lambda w: pl.BlockSpec((tm, w), lambda i: (i, 0))
    return pl.pallas_call(
        _gmlp_prompt_kernel,
        grid=(n // tm,),
        in_specs=[row(D_MODEL), _const_spec((1, D_MODEL)), _const_spec((D_MODEL, 2 * C_WIDTH)),
                  _const_spec((1, C_WIDTH)), _const_spec((1, C_WIDTH)),
                  _const_spec((C_GROUPS, C_CHUNK, C_CHUNK)), _const_spec((C_CHUNK, C_GROUPS)),
                  _const_spec((C_WIDTH, D_MODEL))],
        out_specs=row(D_MODEL),
        out_shape=jax.ShapeDtypeStruct((n, D_MODEL), F32),
        scratch_shapes=[pltpu.VMEM((tm, C_WIDTH), BF16)],
        compiler_params=_params(1),
        name="gmlp_prompt",
    )(x, g, win, lg, lb, ws, bs_t, wout)


def _gmlp_sample_kernel(x_ref, g_ref, win_ref, lg_ref, lb_ref, w0_ref, b0_ref, wout_ref, o_ref, v_ref):
    x = x_ref[...]
    h = _rms(x, g_ref[...]).astype(BF16)
    z = jax.nn.gelu(jnp.dot(h, win_ref[...], preferred_element_type=F32))
    v = _layer_norm(z[:, C_WIDTH:], lg_ref[...], lb_ref[...])
    v_ref[...] = v
    gated = z[:, :C_WIDTH] * (w0_ref[...] * v + b0_ref[...])
    o_ref[...] = x + jnp.dot(gated.astype(BF16), wout_ref[...], preferred_element_type=F32)


def _gmlp_sample(x, g, win, lg, lb, w0, b0, wout):
    n = x.shape[0]
    return pl.pallas_call(
        _gmlp_sample_kernel,
        grid=(1,),
        in_specs=[_const_spec((n, D_MODEL)), _const_spec((1, D_MODEL)), _const_spec((D_MODEL, 2 * C_WIDTH)),
                  _const_spec((1, C_WIDTH)), _const_spec((1, C_WIDTH)), _const_spec((1, C_WIDTH)),
                  _const_spec((1, C_WIDTH)), _const_spec((C_WIDTH, D_MODEL))],
        out_specs=[_const_spec((n, D_MODEL)), _const_spec((n, C_WIDTH))],
        out_shape=[jax.ShapeDtypeStruct((n, D_MODEL), F32), jax.ShapeDtypeStruct((n, C_WIDTH), F32)],
        compiler_params=_params(1),
        name="gmlp_sample",
    )(x, g, win, lg, lb, w0, b0, wout)


META_W = 8


def _split_bf16(a):
    hi = a.astype(BF16)
    return hi, (a - hi.astype(F32)).astype(BF16)


def _router_kernel(x_ref, g_ref, wr_ref, *rest, bt):
    hn_ref, cm_ref, lst_ref, cnt_ref = rest[-4:]
    h = _rms(x_ref[...], g_ref[...])
    tm = h.shape[0]
    h_hi, h_lo = _split_bf16(h)
    w_hi, w_lo = _split_bf16(wr_ref[...])
    hn_ref[...] = h_hi
    logits = (jnp.dot(h_hi, w_hi, preferred_element_type=F32) + jnp.dot(h_lo, w_hi, preferred_element_type=F32)
              + jnp.dot(h_hi, w_lo, preferred_element_type=F32))
    lane = lax.broadcasted_iota(jnp.int32, (bt, 128), 1).astype(F32)
    neg = -jnp.inf
    ri = lax.broadcasted_iota(jnp.int32, (bt, bt), 0)
    ci = lax.broadcasted_iota(jnp.int32, (bt, bt), 1)
    earlier = jnp.where(ri > ci, 1.0, 0.0).astype(BF16)
    ei = lax.broadcasted_iota(jnp.int32, (128, 128), 0)
    ej = lax.broadcasted_iota(jnp.int32, (128, 128), 1)
    lower_experts = jnp.where(ei < ej, 1.0, 0.0).astype(BF16)
    for j in range(tm // bt):
        rows = slice(j * bt, (j + 1) * bt)
        lg = jnp.where(lane < N_EXPERTS, logits[rows], neg)
        m1 = jnp.max(lg, axis=-1, keepdims=True)
        i1 = jnp.min(jnp.where(lg == m1, lane, 128.0), axis=-1, keepdims=True)
        lg2 = jnp.where(lane == i1, neg, lg)
        m2 = jnp.max(lg2, axis=-1, keepdims=True)
        i2 = jnp.min(jnp.where(lg2 == m2, lane, 128.0), axis=-1, keepdims=True)
        e = jnp.exp(m2 - m1)
        g1 = 1.0 / (1.0 + e)
        g2 = e / (1.0 + e)
        sel1 = lane == i1
        sel2 = lane == i2
        oh = jnp.where(sel1, 1.0, jnp.where(sel2, 1.0, 0.0))
        before = jnp.dot(earlier, oh.astype(BF16), preferred_element_type=F32)
        cnt = jnp.sum(oh, axis=0, keepdims=True)
        padded = jnp.floor((cnt + (SEG_ALIGN - 1)) * (1.0 / SEG_ALIGN)) * SEG_ALIGN
        start = jnp.dot(jnp.broadcast_to(padded, (8, 128)).astype(BF16), lower_experts,
                        preferred_element_type=F32)[0:1]
        slot = before + start
        l1 = jnp.sum(jnp.where(sel1, slot, 0.0), axis=-1, keepdims=True)
        l2 = jnp.sum(jnp.where(sel2, slot, 0.0), axis=-1, keepdims=True)
        rec = jnp.where(lane == 0, l1, jnp.where(lane == 1, l2, jnp.where(lane == 2, g1, jnp.where(lane == 3, g2, 0.0))))
        cm_ref[rows, :] = rec[:, :META_W]
        lst_ref[j] = rec.T[0:2, :].astype(jnp.int32)
        cnt_ref[j] = cnt


def _router(x, g, wr_pad, tm, bt, row_blk0, prev=None):
    n = x.shape[0]
    nblk = tm // bt
    in_specs = [pl.BlockSpec((tm, D_MODEL), lambda i: (i, 0)), _const_spec((1, D_MODEL)),
                _const_spec((D_MODEL, 128))]
    args = [x, g, wr_pad]
    aliases = {}
    if prev is not None:
        in_specs += [pl.BlockSpec(memory_space=pl.ANY), pl.BlockSpec(memory_space=pl.ANY)]
        args += list(prev)
        aliases = {3: 0, 4: 1}
    return pl.pallas_call(
        functools.partial(_router_kernel, bt=bt),
        grid=(n // tm,),
        in_specs=in_specs,
        out_specs=[pl.BlockSpec((tm, D_MODEL), lambda i: (row_blk0 + i, 0)),
                   pl.BlockSpec((tm, META_W), lambda i: (row_blk0 + i, 0)),
                   pl.BlockSpec((nblk, 2, bt), lambda i: (i, 0, 0)),
                   pl.BlockSpec((nblk, 1, 128), lambda i: (i, 0, 0))],
        out_shape=[jax.ShapeDtypeStruct((N_TOK, D_MODEL), BF16), jax.ShapeDtypeStruct((N_TOK, META_W), F32),
                   jax.ShapeDtypeStruct((n // bt, 2, bt), jnp.int32), jax.ShapeDtypeStruct((n // bt, 1, 128), F32)],
        input_output_aliases=aliases,
        compiler_params=_params(1),
        name="moe_router",
    )(*args)


def _local_rows(bt):
    return 2 * bt + N_EXPERTS * SEG_ALIGN


def _segment_copies(seg_n, seg_src, seg_dst, blk, local_ref, slot_ref, sem, bt, to_slots, action):
    def per_expert(e, c):
        s = blk * N_EXPERTS + e
        n, src, dst = seg_n[s], seg_src[s], seg_dst[s]
        off = jnp.int32(0)
        p = bt
        while p >= SEG_ALIGN:
            hit = (n & p) != 0

            @pl.when(hit)
            def _(off=off, p=p):
                a = local_ref.at[pl.ds(pl.multiple_of(src + off, SEG_ALIGN), p)]
                b = slot_ref.at[pl.ds(pl.multiple_of(dst + off, SEG_ALIGN), p)]
                action(pltpu.make_async_copy(a, b, sem) if to_slots else pltpu.make_async_copy(b, a, sem))

            off = off + (n & p)
            p //= 2
        return c

    lax.fori_loop(0, N_EXPERTS, per_expert, 0)


def _dispatch_kernel(seg_n, seg_src, seg_dst, hn_ref, ls_ref, *rest, bt, blk0, nb):
    slots_ref, buf_ref, sem = rest[-3:]
    i = pl.program_id(0)
    par = i % 2
    start = lambda cp: cp.start()
    wait = lambda cp: cp.wait()
    copies = lambda blk, p, action: _segment_copies(seg_n, seg_src, seg_dst, blk, buf_ref.at[p], slots_ref,
                                                     sem.at[p], bt, True, action)

    @pl.when(i >= 2)
    def _():
        copies(blk0 + i - 2, par, wait)

    rows = lax.broadcasted_iota(jnp.int32, (_local_rows(bt), bt), 0)
    ls = ls_ref[0]
    onehot = jnp.where(rows == ls[0:1, :], 1.0, jnp.where(rows == ls[1:2, :], 1.0, 0.0)).astype(BF16)
    buf_ref[par] = jnp.dot(onehot, hn_ref[...], preferred_element_type=F32)
    copies(blk0 + i, par, start)

    @pl.when(i == nb - 1)
    def _():
        if nb >= 2:
            copies(blk0 + i - 1, 1 - par, wait)
        copies(blk0 + i, par, wait)


def _dispatch(seg, hn, ls_t, bt, blk0, tok0, prev=None):
    nb = ls_t.shape[0]
    in_specs = [pl.BlockSpec((bt, D_MODEL), lambda i, *_: (tok0 // bt + i, 0)),
                pl.BlockSpec((1, 2, bt), lambda i, *_: (i, 0, 0))]
    args = [hn, ls_t]
    aliases = {}
    if prev is not None:
        in_specs.append(pl.BlockSpec(memory_space=pl.ANY))
        args.append(prev)
        aliases = {5: 0}
    return pl.pallas_call(
        functools.partial(_dispatch_kernel, bt=bt, blk0=blk0, nb=nb),
        grid_spec=pltpu.PrefetchScalarGridSpec(
            num_scalar_prefetch=3,
            grid=(nb,),
            in_specs=in_specs,
            out_specs=pl.BlockSpec(memory_space=pl.ANY),
            scratch_shapes=[pltpu.VMEM((2, _local_rows(bt), D_MODEL), F32), pltpu.SemaphoreType.DMA((2,))],
        ),
        out_shape=jax.ShapeDtypeStruct((N_SLOTS, D_MODEL), F32),
        input_output_aliases=aliases,
        compiler_params=_params(1),
        name="moe_dispatch",
    )(*seg, *args)


def _experts_kernel(te_ref, tb_ref, nv_ref, x_ref, wg_ref, wu_ref, wd_ref, o_ref, xb_ref):
    t = pl.program_id(0)
    f = pl.program_id(1)
    nv = nv_ref[t]

    @pl.when(jnp.logical_and(nv > 0, f == 0))
    def _():
        rows = lax.broadcasted_iota(jnp.int32, (MOE_TILE, 1), 0)
        xb_ref[...] = jnp.where(rows < nv, x_ref[...], 0.0).astype(BF16)
        o_ref[...] = jnp.zeros_like(o_ref)

    def swiglu_rows(n_rows):
        rows = slice(0, n_rows)
        xb = xb_ref[rows, :]
        for c in range(MOE_FF_TILE // MOE_FF_SUB):
            cols = slice(c * MOE_FF_SUB, (c + 1) * MOE_FF_SUB)
            gate = jnp.dot(xb, wg_ref[:, cols].astype(BF16), preferred_element_type=F32)
            up = jnp.dot(xb, wu_ref[:, cols].astype(BF16), preferred_element_type=F32)
            act = (gate * jax.nn.sigmoid(gate) * up).astype(BF16)
            o_ref[rows, :] += jnp.dot(act, wd_ref[cols, :].astype(BF16), preferred_element_type=F32)

    lo = 0
    for n_rows in MOE_ROW_PATHS:
        @pl.when(jnp.logical_and(nv > lo, nv <= n_rows))
        def _(n_rows=n_rows):
            swiglu_rows(n_rows)
        lo = n_rows


def _experts(tile_expert, tile_blk, tile_rows, xs, wg, wu, wd, layer):
    def chunk(t, f, nv):
        return jnp.where(nv[t] > 0, f, MOE_NF - 1)
    up_spec = pl.BlockSpec((None, None, D_MODEL, MOE_FF_TILE),
                           lambda t, f, te, tb, nv: (layer, te[t], 0, chunk(t, f, nv)))
    down_spec = pl.BlockSpec((None, None, MOE_FF_TILE, D_MODEL),
                             lambda t, f, te, tb, nv: (layer, te[t], chunk(t, f, nv), 0))
    return pl.pallas_call(
        _experts_kernel,
        grid_spec=pltpu.PrefetchScalarGridSpec(
            num_scalar_prefetch=3,
            grid=(MOE_MAX_TILES, MOE_NF),
            in_specs=[pl.BlockSpec((MOE_TILE, D_MODEL), lambda t, f, te, tb, nv: (tb[t], 0)),
                      up_spec, up_spec, down_spec],
            out_specs=pl.BlockSpec((MOE_TILE, D_MODEL), lambda t, f, te, tb, nv: (tb[t], 0)),
            scratch_shapes=[pltpu.VMEM((MOE_TILE, D_MODEL), BF16)],
        ),
        out_shape=jax.ShapeDtypeStruct((N_SLOTS, D_MODEL), F32),
        compiler_params=_params(2),
        name="moe_experts",
    )(tile_expert, tile_blk, tile_rows, xs, wg, wu, wd)


def _combine_kernel(seg_n, seg_src, seg_dst, x_ref, cm_ref, eo_ref, y_ref, buf_ref, sem, *, bt, blk0, nb):
    i = pl.program_id(0)
    par = i % 2
    start = lambda cp: cp.start()
    wait = lambda cp: cp.wait()
    copies = lambda blk, p, action: _segment_copies(seg_n, seg_src, seg_dst, blk, buf_ref.at[p], eo_ref,
                                                     sem.at[p], bt, False, action)

    @pl.when(i == 0)
    def _():
        copies(blk0, 0, start)

    @pl.when(i + 1 < nb)
    def _():
        copies(blk0 + i + 1, 1 - par, start)

    copies(blk0 + i, par, wait)
    last = (blk0 + i) * N_EXPERTS + (N_EXPERTS - 1)
    used = seg_src[last] + seg_n[last]
    rows = lax.broadcasted_iota(jnp.int32, (_local_rows(bt), 1), 0)
    e_hi, e_lo = _split_bf16(jnp.where(rows < used, buf_ref[par], 0.0))
    cm = cm_ref[...]
    cols = lax.broadcasted_iota(jnp.int32, (bt, _local_rows(bt)), 1).astype(F32)
    comb = jnp.where(cols == cm[:, 0:1], cm[:, 2:3], jnp.where(cols == cm[:, 1:2], cm[:, 3:4], 0.0))
    c_hi, c_lo = _split_bf16(comb)
    y_ref[...] = x_ref[...] + jnp.dot(c_hi, e_hi, preferred_element_type=F32)
    y_ref[...] += jnp.dot(c_lo, e_hi, preferred_element_type=F32)
    y_ref[...] += jnp.dot(c_hi, e_lo, preferred_element_type=F32)


def _combine(seg, x, cm, eo, bt, blk0, tok0):
    n = x.shape[0]
    nb = n // bt
    return pl.pallas_call(
        functools.partial(_combine_kernel, bt=bt, blk0=blk0, nb=nb),
        grid_spec=pltpu.PrefetchScalarGridSpec(
            num_scalar_prefetch=3,
            grid=(nb,),
            in_specs=[pl.BlockSpec((bt, D_MODEL), lambda i, *_: (i, 0)),
                      pl.BlockSpec((bt, META_W), lambda i, *_: (tok0 // bt + i, 0)),
                      pl.BlockSpec(memory_space=pl.ANY)],
            out_specs=pl.BlockSpec((bt, D_MODEL), lambda i, *_: (i, 0)),
            scratch_shapes=[pltpu.VMEM((2, _local_rows(bt), D_MODEL), F32), pltpu.SemaphoreType.DMA((2,))],
        ),
        out_shape=jax.ShapeDtypeStruct((n, D_MODEL), F32),
        compiler_params=_params(1),
        name="moe_combine",
    )(*seg, x, cm, eo)


def _moe(xp, xs, g, w_router, wg, wu, wd, layer):
    g2 = g.reshape(1, D_MODEL)
    wr_pad = jnp.pad(w_router, ((0, 0), (0, 128 - N_EXPERTS)))
    hn, cm, ls_p, cnt_p = _router(xp, g2, wr_pad, ROW_TILE, MOE_BLOCK, 0)
    hn, cm, ls_s, cnt_s = _router(xs, g2, wr_pad, DEC_BATCH, DEC_BATCH, N_PROMPT // DEC_BATCH, prev=(hn, cm))

    nbp = N_PROMPT // MOE_BLOCK
    seg_cnt = jnp.concatenate([cnt_p[:, 0, :N_EXPERTS], cnt_s[:, 0, :N_EXPERTS]], axis=0).astype(jnp.int32)
    seg_n = (seg_cnt + (SEG_ALIGN - 1)) // SEG_ALIGN * SEG_ALIGN
    seg_src = jnp.cumsum(seg_n, axis=1) - seg_n
    slot0 = jnp.cumsum(seg_n, axis=0) - seg_n

    counts = seg_n.sum(axis=0)
    tiles_e = (counts + (MOE_TILE - 1)) // MOE_TILE
    tile_end = jnp.cumsum(tiles_e)
    tile_start = tile_end - tiles_e
    n_tiles = tile_end[-1]
    t = jnp.arange(MOE_MAX_TILES, dtype=jnp.int32)
    tc = jnp.minimum(t, n_tiles - 1)
    tile_expert = jnp.sum((tc[:, None] >= tile_end[None, :]).astype(jnp.int32), axis=1)
    rows_left = counts[tile_expert] - (tc - tile_start[tile_expert]) * MOE_TILE
    tile_rows = jnp.where(t < n_tiles, jnp.minimum(rows_left, MOE_TILE), 0).astype(jnp.int32)
    seg_dst = (tile_start * MOE_TILE)[None, :] + slot0

    cap = jnp.concatenate([jnp.full((nbp, 1), MOE_BLOCK, jnp.int32), jnp.full((1, 1), DEC_BATCH, jnp.int32)])
    seg_n = jnp.clip(seg_n, 0, cap)
    seg = (seg_n.reshape(-1), jnp.clip(seg_src, 0, 2 * cap + N_EXPERTS * SEG_ALIGN - seg_n).reshape(-1),
           jnp.clip(seg_dst, 0, N_SLOTS - MOE_BLOCK).reshape(-1))

    slots = _dispatch(seg, hn, ls_p, MOE_BLOCK, 0, 0)
    slots = _dispatch(seg, hn, ls_s, DEC_BATCH, nbp, N_PROMPT, prev=slots)
    eo = _experts(tile_expert.astype(jnp.int32), tc, tile_rows, slots, wg, wu, wd, layer)
    yp = _combine(seg, xp, cm, eo, MOE_BLOCK, 0, 0)
    ys = _combine(seg, xs, cm, eo, DEC_BATCH, nbp, N_PROMPT)
    return yp, ys


def kernel(x_prompt, x_sample, state_conv, cache_k, cache_v, norm_mix_g, norm_ffn_g, w_in_ab, conv_w, conv_b,
           conv_ln_g, conv_ln_b, q_norm_g, k_norm_g, attn_sinks, w_out_ab, w_gate_dense, w_up_dense, w_down_dense,
           w_in_c, c_ln_g, c_ln_b, w_spatial, b_spatial, w_out_c, w_router, w_gate_exp, w_up_exp, w_down_exp):
    xp = x_prompt.reshape(N_PROMPT, D_MODEL)
    xs = x_sample.reshape(DEC_BATCH, D_MODEL)
    head = jnp.arange(B_WIDTH) // HEAD_DIM
    seg = (head[:, None] == head[None, :]).astype(BF16)
    conv_p, conv_s, kp_out, vp_out, ks_out, vs_out, vc_out = [], [], [], [], [], [], []
    row = lambda a: a.reshape(1, -1)
    wo_all, wg_all = w_out_ab.astype(BF16), w_gate_dense.astype(BF16)
    wu_all, wd_all = w_up_dense.astype(BF16), w_down_dense.astype(BF16)
    for layer in range(DEPTH):
        i = layer // 2
        g_mix = row(norm_mix_g[layer])
        g_ffn = row(norm_ffn_g[layer])
        if layer % 2 == 0:
            w_in = w_in_ab[i].astype(BF16)
            qg = row(jnp.tile(q_norm_g[i], N_HEADS))
            kg = row(jnp.tile(k_norm_g[i], N_KV_HEADS))
            a_p, q_p, k_p, v_p = _inproj(xp, g_mix, w_in, qg, kg, seg, ROW_TILE)
            a_s, q_s, k_s, v_s = _inproj(xs, g_mix, w_in, qg, kg, seg, DEC_BATCH)
            cb, lg, lb = row(conv_b[i]), row(conv_ln_g[i]), row(conv_ln_b[i])
            ca_p = _conv_prompt(a_p, conv_w[i], cb, lg, lb)
            ca_s = _conv_sample(a_s, state_conv[i], conv_w[i], cb, lg, lb)
            ob_p = _attn_prompt(q_p, k_p, v_p, attn_sinks[i])
            sinks_b = jnp.broadcast_to(attn_sinks[i].reshape(N_KV_HEADS, Q_GROUP, 1), (N_KV_HEADS, Q_GROUP, 128))
            ob_s = _attn_sample(q_s.reshape(DEC_BATCH, N_KV_HEADS, Q_GROUP, HEAD_DIM),
                                k_s.reshape(DEC_BATCH, N_KV_HEADS, HEAD_DIM),
                                v_s.reshape(DEC_BATCH, N_KV_HEADS, HEAD_DIM),
                                cache_k[i].reshape(DEC_BATCH, WINDOW, KV_WIDTH),
                                cache_v[i].reshape(DEC_BATCH, WINDOW, KV_WIDTH), sinks_b)
            ob_s = ob_s.reshape(DEC_BATCH, B_WIDTH)
            conv_p.append(a_p.reshape(BATCH, SEQ, A_CH)[:, SEQ - (CONV_W - 1):])
            conv_s.append(jnp.concatenate([state_conv[i][:, 1:], a_s[:, None, :]], axis=1))
            last = lambda t: (t.reshape(BATCH, SEQ, KV_WIDTH)[:, SEQ - WINDOW:]
                              .reshape(BATCH, WINDOW, N_KV_HEADS, HEAD_DIM))
            kp_out.append(last(k_p))
            vp_out.append(last(v_p))
            ks_out.append(jnp.concatenate(
                [cache_k[i][:, 1:], k_s.reshape(DEC_BATCH, 1, N_KV_HEADS, HEAD_DIM)], axis=1))
            vs_out.append(jnp.concatenate(
                [cache_v[i][:, 1:], v_s.reshape(DEC_BATCH, 1, N_KV_HEADS, HEAD_DIM)], axis=1))
            xp = _outproj_ffn(xp, ca_p, ob_p, wo_all, g_ffn, wg_all, wu_all, wd_all, ROW_TILE, i)
            xs = _outproj_ffn(xs, ca_s, ob_s, wo_all, g_ffn, wg_all, wu_all, wd_all, DEC_BATCH, i)
        else:
            win = w_in_c[i].astype(BF16)
            wout = w_out_c[i].astype(BF16)
            lg, lb = row(c_ln_g[i]), row(c_ln_b[i])
            xp = _gmlp_prompt(xp, g_mix, win, lg, lb, w_spatial[i], b_spatial[i].T, wout, ROW_TILE)
            gd = C_WIDTH // C_GROUPS
            w0 = row(jnp.repeat(w_spatial[i][:, 0, 0], gd))
            b0 = row(jnp.repeat(b_spatial[i][:, 0], gd))
            xs, v_s = _gmlp_sample(xs, g_mix, win, lg, lb, w0, b0, wout)
            vc_out.append(v_s.reshape(DEC_BATCH, 1, C_WIDTH))
            xp, xs = _moe(xp, xs, norm_ffn_g[layer], w_router[i], w_gate_exp, w_up_exp, w_down_exp, i)
    return (xp.reshape(BATCH, SEQ, D_MODEL), xs.reshape(DEC_BATCH, 1, D_MODEL),
            jnp.stack(conv_p), jnp.stack(conv_s), jnp.stack(kp_out), jnp.stack(vp_out),
            jnp.stack(ks_out), jnp.stack(vs_out), jnp.stack(vc_out))
```

```python
import functools

import jax
import jax.numpy as jnp
from jax import lax
from jax.experimental import pallas as pl
from jax.experimental.pallas import tpu as pltpu

F32 = jnp.float32
BF16 = jnp.bfloat16

D_MODEL = 1024
BATCH = 8
SEQ = 2048
DEPTH = 4
N_AB = (DEPTH + 1) // 2
DEC_BATCH = 128
A_CH = 512
CONV_W = 31
N_HEADS = 8
N_KV_HEADS = 2
HEAD_DIM = 64
Q_GROUP = 4
WINDOW = 128
B_WIDTH = 512
KV_WIDTH = 128
AB_IN = 2 * A_CH + B_WIDTH + 2 * KV_WIDTH
C_WIDTH = 1024
C_GROUPS = 8
C_CHUNK = 128
D_FF = 2816
N_EXPERTS = 8
D_FF_EXPERT = 3584
EPS = 1e-6
ATTN_SCALE = HEAD_DIM ** -0.5

N_PROMPT = BATCH * SEQ
N_TOK = N_PROMPT + DEC_BATCH
ROW_TILE = 1024
FF_SPLIT = 11
MOE_TILE = 1024
MOE_FF_TILE = 512
MOE_FF_SUB = 256
MOE_ROW_PATHS = (256, 512, MOE_TILE)
MOE_NF = D_FF_EXPERT // MOE_FF_TILE
MOE_BLOCK = 256
MOE_N_BLOCKS = N_PROMPT // MOE_BLOCK + 1
SEG_ALIGN = 8
MOE_MAX_TILES = (2 * N_TOK + MOE_N_BLOCKS * N_EXPERTS * (SEG_ALIGN - 1) + N_EXPERTS * (MOE_TILE - 1)) // MOE_TILE
N_SLOTS = MOE_MAX_TILES * MOE_TILE
VMEM_LIMIT = 56 * 1024 * 1024


def _params(n_axes, vmem=VMEM_LIMIT):
    return pltpu.CompilerParams(dimension_semantics=("arbitrary",) * n_axes, vmem_limit_bytes=vmem)


def _rms(x, g):
    return x * lax.rsqrt(jnp.mean(x * x, axis=-1, keepdims=True) + EPS) * g


def _layer_norm(x, g, b):
    mu = jnp.mean(x, axis=-1, keepdims=True)
    xc = x - mu
    var = jnp.mean(xc * xc, axis=-1, keepdims=True)
    return xc * lax.rsqrt(var + EPS) * g + b


def _const_spec(shape):
    return pl.BlockSpec(shape, lambda *_: (0,) * len(shape))


def _layer_spec(shape, layer):
    return pl.BlockSpec((None,) + tuple(shape), lambda *_: (layer,) + (0,) * len(shape))


def _inproj_kernel(x_ref, g_ref, w_ref, qg_ref, kg_ref, seg_ref, a_ref, q_ref, k_ref, v_ref):
    h = _rms(x_ref[...], g_ref[...]).astype(BF16)
    z = jnp.dot(h, w_ref[...], preferred_element_type=F32)
    a_ref[...] = z[:, :A_CH] * jax.nn.sigmoid(z[:, A_CH:2 * A_CH])
    seg = seg_ref[...]

    def head_norm(t, gain, seg_t):
        sq = t * t
        hi = sq.astype(BF16)
        lo = (sq - hi.astype(F32)).astype(BF16)
        ms = (jnp.dot(hi, seg_t, preferred_element_type=F32)
              + jnp.dot(lo, seg_t, preferred_element_type=F32)) * (1.0 / HEAD_DIM)
        return t * lax.rsqrt(ms + EPS) * gain

    o = 2 * A_CH
    q = head_norm(z[:, o:o + B_WIDTH], qg_ref[...], seg)
    q_ref[...] = (q * ATTN_SCALE).astype(BF16)
    o += B_WIDTH
    k_ref[...] = head_norm(z[:, o:o + KV_WIDTH], kg_ref[...], seg[:KV_WIDTH, :KV_WIDTH])
    o += KV_WIDTH
    v_ref[...] = z[:, o:o + KV_WIDTH]


def _inproj(x, g, w_bf, qg, kg, seg, tm, layer):
    n = x.shape[0]
    row = lambda w: pl.BlockSpec((tm, w), lambda i: (i, 0))
    return pl.pallas_call(
        _inproj_kernel,
        grid=(n // tm,),
        in_specs=[row(D_MODEL), _const_spec((1, D_MODEL)), _layer_spec((D_MODEL, AB_IN), layer),
                  _const_spec((1, B_WIDTH)), _const_spec((1, KV_WIDTH)), _const_spec((B_WIDTH, B_WIDTH))],
        out_specs=[row(A_CH), row(B_WIDTH), row(KV_WIDTH), row(KV_WIDTH)],
        out_shape=[jax.ShapeDtypeStruct((n, A_CH), F32), jax.ShapeDtypeStruct((n, B_WIDTH), BF16),
                   jax.ShapeDtypeStruct((n, KV_WIDTH), F32), jax.ShapeDtypeStruct((n, KV_WIDTH), F32)],
        compiler_params=_params(1),
        name="inproj_ab",
    )(x, g, w_bf, qg, kg, seg)


CONV_PAD = 32
CONV_ROWS = 128
CONV_SUB = 32


def _conv_tail(c, lg, lb):
    y = _layer_norm(c, lg, lb)
    return y * jax.nn.sigmoid(y)


def _conv_prompt_kernel(a_ref, w_ref, b_ref, lg_ref, lb_ref, o_ref, pad_ref, sh_ref, c_ref):
    pad_ref[0:CONV_PAD, :] = jnp.zeros((CONV_PAD, A_CH), F32)
    pad_ref[CONV_PAD:, :] = a_ref[...]
    win = CONV_ROWS + CONV_PAD

    def body(r, carry):
        t0 = pl.multiple_of(r * CONV_ROWS, CONV_ROWS)
        nc = A_CH // 128
        for c in range(nc):
            sh_ref[0, c] = pad_ref[pl.ds(t0, win), c * 128:(c + 1) * 128]
        for s in range(1, 8):
            for c in range(nc):
                sh_ref[s, c, 0:win - 8, :] = sh_ref[0, c, s:s + win - 8, :]
        first = CONV_PAD - (CONV_W - 1)

        def chunk(c, carry2):
            for r0 in range(0, CONV_ROWS, CONV_SUB):
                acc = jnp.zeros((CONV_SUB, 128), F32)
                for s in range(8):
                    taps = [k for k in range(CONV_W) if (first + k) % 8 == s]
                    span = max(first + k - s for k in taps) + CONV_SUB
                    rows = sh_ref[s, c, r0:r0 + span, :]
                    for k in taps:
                        base = first + k - s
                        acc = acc + rows[base:base + CONV_SUB] * w_ref[c, k:k + 1, :]
                c_ref[c, r0:r0 + CONV_SUB, :] = acc
            return carry2

        lax.fori_loop(0, nc, chunk, 0)
        conv = jnp.concatenate([c_ref[c] for c in range(nc)], axis=-1)
        y = _conv_tail(conv + b_ref[...], lg_ref[...], lb_ref[...])
        o_ref[pl.ds(t0, CONV_ROWS), :] = y.astype(BF16)
        return carry

    lax.fori_loop(0, SEQ // CONV_ROWS, body, 0)


def _conv_prompt(a, w, b, lg, lb):
    nc = A_CH // 128
    w_chunks = w.reshape(CONV_W, nc, 128).transpose(1, 0, 2)
    return pl.pallas_call(
        _conv_prompt_kernel,
        grid=(BATCH,),
        in_specs=[pl.BlockSpec((SEQ, A_CH), lambda i: (i, 0)), _const_spec((nc, CONV_W, 128)),
                  _const_spec((1, A_CH)), _const_spec((1, A_CH)), _const_spec((1, A_CH))],
        out_specs=pl.BlockSpec((SEQ, A_CH), lambda i: (i, 0)),
        out_shape=jax.ShapeDtypeStruct((N_PROMPT, A_CH), BF16),
        scratch_shapes=[pltpu.VMEM((CONV_PAD + SEQ, A_CH), F32),
                        pltpu.VMEM((8, nc, CONV_ROWS + CONV_PAD, 128), F32),
                        pltpu.VMEM((nc, CONV_ROWS, 128), F32)],
        compiler_params=_params(1),
        name="conv_prompt",
    )(a, w_chunks, b, lg, lb)


def _conv_sample_kernel(a_ref, st_ref, w_ref, b_ref, lg_ref, lb_ref, o_ref):
    acc = a_ref[...] * w_ref[CONV_W - 1:CONV_W, :]
    acc = acc + jnp.sum(st_ref[...] * w_ref[0:CONV_W - 1, :][None], axis=1)
    o_ref[...] = _conv_tail(acc + b_ref[...], lg_ref[...], lb_ref[...]).astype(BF16)


def _conv_sample(a, st_t, w, b, lg, lb, layer):
    return pl.pallas_call(
        _conv_sample_kernel,
        grid=(1,),
        in_specs=[_const_spec((DEC_BATCH, A_CH)), _layer_spec((DEC_BATCH, CONV_W - 1, A_CH), layer),
                  _const_spec((CONV_W, A_CH)), _const_spec((1, A_CH)), _const_spec((1, A_CH)),
                  _const_spec((1, A_CH))],
        out_specs=_const_spec((DEC_BATCH, A_CH)),
        out_shape=jax.ShapeDtypeStruct((DEC_BATCH, A_CH), BF16),
        compiler_params=_params(1),
        name="conv_sample",
    )(a, st_t, w, b, lg, lb)


def _alibi_slope(h):
    return 2.0 ** (-8.0 * (h + 1) / N_HEADS)


ATTN_P_WINDOWS = 4


def _attn_prompt_kernel(sink_ref, q_ref, kp_ref, kc_ref, vp_ref, vc_ref, o_ref):
    step = pl.program_id(1)
    kk = jnp.concatenate([kp_ref[...], kc_ref[...]], axis=0).astype(BF16)
    vv = jnp.concatenate([vp_ref[...], vc_ref[...]], axis=0).astype(BF16)
    qi = lax.broadcasted_iota(jnp.int32, (WINDOW, 2 * WINDOW), 0)
    kj = lax.broadcasted_iota(jnp.int32, (WINDOW, 2 * WINDOW), 1)
    dist_i = WINDOW + qi - kj
    in_window = (dist_i >= 0) & (dist_i < WINDOW)
    lo = jnp.where(step > 0, 0, WINDOW)
    dist = dist_i.astype(F32)
    for w in range(ATTN_P_WINDOWS):
        valid = in_window & (kj >= lo) if w == 0 else in_window
        rows = slice(w * WINDOW, (w + 1) * WINDOW)
        keys = slice(w * WINDOW, (w + 2) * WINDOW)
        outs = []
        for h in range(N_HEADS):
            kv = h // Q_GROUP
            lanes = slice(kv * HEAD_DIM, (kv + 1) * HEAD_DIM)
            qh = q_ref[rows, h * HEAD_DIM:(h + 1) * HEAD_DIM]
            s = lax.dot_general(qh, kk[keys, lanes], (((1,), (1,)), ((), ())), preferred_element_type=F32)
            s = jnp.where(valid, s - _alibi_slope(h) * dist, -jnp.inf)
            sink = sink_ref[h]
            m = jnp.maximum(jnp.max(s, axis=-1, keepdims=True), sink)
            p = jnp.exp(s - m)
            denom = jnp.sum(p, axis=-1, keepdims=True) + jnp.exp(sink - m)
            p = (p / denom).astype(BF16)
            outs.append(jnp.dot(p, vv[keys, lanes], preferred_element_type=F32))
        o_ref[rows, :] = jnp.concatenate(outs, axis=-1).astype(BF16)


def _attn_prompt(q, k, v, sinks):
    rows = ATTN_P_WINDOWS * WINDOW
    ns = SEQ // rows
    own = lambda b, j: (b * ns + j, 0)
    prev = lambda b, j: (b * (SEQ // WINDOW) + jnp.maximum(j * ATTN_P_WINDOWS - 1, 0), 0)
    return pl.pallas_call(
        _attn_prompt_kernel,
        grid=(BATCH, ns),
        in_specs=[pl.BlockSpec(memory_space=pltpu.SMEM),
                  pl.BlockSpec((rows, B_WIDTH), own),
                  pl.BlockSpec((WINDOW, KV_WIDTH), prev), pl.BlockSpec((rows, KV_WIDTH), own),
                  pl.BlockSpec((WINDOW, KV_WIDTH), prev), pl.BlockSpec((rows, KV_WIDTH), own)],
        out_specs=pl.BlockSpec((rows, B_WIDTH), own),
        out_shape=jax.ShapeDtypeStruct((N_PROMPT, B_WIDTH), BF16),
        compiler_params=_params(2),
        name="attn_prompt",
    )(sinks, q, k, k, v, v)


ATTN_S_BLK = 32


def _attn_sample_kernel(sink_ref, q_ref, kn_ref, vn_ref, kc_ref, vc_ref, o_ref):
    kc = kc_ref[...].astype(BF16)
    vc = vc_ref[...].astype(BF16)
    j = lax.broadcasted_iota(jnp.int32, (1, 1, WINDOW), 2)
    dist = (WINDOW - j).astype(F32)
    g_idx = lax.broadcasted_iota(jnp.int32, (1, Q_GROUP, 1), 1)
    for kv in range(N_KV_HEADS):
        lanes = slice(kv * HEAD_DIM, (kv + 1) * HEAD_DIM)
        qg = q_ref[:, kv]
        slope = jnp.exp2(-(g_idx + (kv * Q_GROUP + 1)).astype(F32) * (8.0 / N_HEADS))
        s = jnp.einsum('ngd,njd->ngj', qg, kc[:, :, lanes], preferred_element_type=F32)
        s = jnp.where(j >= 1, s - slope * dist, -jnp.inf)
        kn = kn_ref[:, kv].astype(BF16).astype(F32)
        s_new = jnp.sum(qg.astype(F32) * kn[:, None, :], axis=-1, keepdims=True)
        sink = sink_ref[kv][:, :1][None]
        m = jnp.maximum(jnp.maximum(jnp.max(s, axis=-1, keepdims=True), s_new), sink)
        p = jnp.exp(s - m)
        p_new = jnp.exp(s_new - m)
        denom = jnp.sum(p, axis=-1, keepdims=True) + p_new + jnp.exp(sink - m)
        o = jnp.einsum('ngj,njd->ngd', (p / denom).astype(BF16), vc[:, :, lanes], preferred_element_type=F32)
        vn = vn_ref[:, kv].astype(BF16).astype(F32)
        o = o + (p_new / denom).astype(BF16).astype(F32) * vn[:, None, :]
        o_ref[:, kv] = o.astype(BF16)


def _attn_sample(q4, kn3, vn3, kc, vc, sinks_b, layer):
    nb = ATTN_S_BLK
    return pl.pallas_call(
        _attn_sample_kernel,
        grid=(DEC_BATCH // nb,),
        in_specs=[_const_spec((N_KV_HEADS, Q_GROUP, 128)),
                  pl.BlockSpec((nb, N_KV_HEADS, Q_GROUP, HEAD_DIM), lambda i: (i, 0, 0, 0)),
                  pl.BlockSpec((nb, N_KV_HEADS, HEAD_DIM), lambda i: (i, 0, 0)),
                  pl.BlockSpec((nb, N_KV_HEADS, HEAD_DIM), lambda i: (i, 0, 0)),
                  pl.BlockSpec((None, nb, WINDOW, KV_WIDTH), lambda i: (layer, i, 0, 0)),
                  pl.BlockSpec((None, nb, WINDOW, KV_WIDTH), lambda i: (layer, i, 0, 0))],
        out_specs=pl.BlockSpec((nb, N_KV_HEADS, Q_GROUP, HEAD_DIM), lambda i: (i, 0, 0, 0)),
        out_shape=jax.ShapeDtypeStruct((DEC_BATCH, N_KV_HEADS, Q_GROUP, HEAD_DIM), BF16),
        compiler_params=_params(1),
        name="attn_sample",
    )(sinks_b, q4, kn3, vn3, kc, vc)


def _outproj_ffn_kernel(x_ref, ca_ref, ob_ref, g_ref, wo_hbm, wg_hbm, wu_hbm, wd_hbm, o_ref,
                        wo_ref, wg_ref, wu_ref, wd_ref, *, layer):
    @pl.when(pl.program_id(0) == 0)
    def _():
        pltpu.sync_copy(wo_hbm.at[layer], wo_ref)
        pltpu.sync_copy(wg_hbm.at[layer], wg_ref)
        pltpu.sync_copy(wu_hbm.at[layer], wu_ref)
        pltpu.sync_copy(wd_hbm.at[layer], wd_ref)

    o_ref[...] = (x_ref[...]
                  + jnp.dot(ca_ref[...], wo_ref[:A_CH, :], preferred_element_type=F32)
                  + jnp.dot(ob_ref[...], wo_ref[A_CH:, :], preferred_element_type=F32))
    h = _rms(o_ref[...], g_ref[...]).astype(BF16)
    ff = D_FF // FF_SPLIT
    for c in range(FF_SPLIT):
        cols = slice(c * ff, (c + 1) * ff)
        gate = jnp.dot(h, wg_ref[:, cols], preferred_element_type=F32)
        up = jnp.dot(h, wu_ref[:, cols], preferred_element_type=F32)
        act = (gate * jax.nn.sigmoid(gate) * up).astype(BF16)
        o_ref[...] += jnp.dot(act, wd_ref[cols, :], preferred_element_type=F32)


def _outproj_ffn(x, ca, ob, wo, g, wg, wu, wd, tm, layer):
    n = x.shape[0]
    row = lambda w: pl.BlockSpec((tm, w), lambda i: (i, 0))
    hbm = pl.BlockSpec(memory_space=pl.ANY)
    return pl.pallas_call(
        functools.partial(_outproj_ffn_kernel, layer=layer),
        grid=(n // tm,),
        in_specs=[row(D_MODEL), row(A_CH), row(B_WIDTH), _const_spec((1, D_MODEL)), hbm, hbm, hbm, hbm],
        out_specs=row(D_MODEL),
        out_shape=jax.ShapeDtypeStruct((n, D_MODEL), F32),
        scratch_shapes=[pltpu.VMEM((D_MODEL, D_MODEL), BF16), pltpu.VMEM((D_MODEL, D_FF), BF16),
                        pltpu.VMEM((D_MODEL, D_FF), BF16), pltpu.VMEM((D_FF, D_MODEL), BF16)],
        compiler_params=_params(1),
        name="outproj_ffn",
    )(x, ca, ob, g, wo, wg, wu, wd)


def _gmlp_prompt_kernel(x_ref, g_ref, win_ref, lg_ref, lb_ref, ws_ref, bs_ref, wout_ref, o_ref, gate_ref):
    x = x_ref[...]
    h = _rms(x, g_ref[...]).astype(BF16)
    z = jax.nn.gelu(jnp.dot(h, win_ref[...], preferred_element_type=F32))
    u = z[:, :C_WIDTH]
    v = _layer_norm(z[:, C_WIDTH:], lg_ref[...], lb_ref[...]).astype(BF16)
    ri = lax.broadcasted_iota(jnp.int32, (C_CHUNK, C_CHUNK), 0)
    ci = lax.broadcasted_iota(jnp.int32, (C_CHUNK, C_CHUNK), 1)
    tm = x.shape[0]
    gd = C_WIDTH // C_GROUPS
    for grp in range(C_GROUPS):
        w = jnp.where(ri >= ci, ws_ref[grp], 0.0).astype(BF16)
        b = bs_ref[:, grp:grp + 1]
        lanes = slice(grp * gd, (grp + 1) * gd)
        for c in range(tm // C_CHUNK):
            rows = slice(c * C_CHUNK, (c + 1) * C_CHUNK)
            mixed = jnp.dot(w, v[rows, lanes], preferred_element_type=F32) + b
            gate_ref[rows, lanes] = (u[rows, lanes] * mixed).astype(BF16)
    o_ref[...] = x + jnp.dot(gate_ref[...], wout_ref[...], preferred_element_type=F32)


def _gmlp_prompt(x, g, win, lg, lb, ws, bs_t, wout, tm, layer):
    n = x.shape[0]
    row = lambda w: pl.BlockSpec((tm, w), lambda i: (i, 0))
    return pl.pallas_call(
        _gmlp_prompt_kernel,
        grid=(n // tm,),
        in_specs=[row(D_MODEL), _const_spec((1, D_MODEL)), _layer_spec((D_MODEL, 2 * C_WIDTH), layer),
                  _const_spec((1, C_WIDTH)), _const_spec((1, C_WIDTH)),
                  _layer_spec((C_GROUPS, C_CHUNK, C_CHUNK), layer), _const_spec((C_CHUNK, C_GROUPS)),
                  _layer_spec((C_WIDTH, D_MODEL), layer)],
        out_specs=row(D_MODEL),
        out_shape=jax.ShapeDtypeStruct((n, D_MODEL), F32),
        scratch_shapes=[pltpu.VMEM((tm, C_WIDTH), BF16)],
        compiler_params=_params(1),
        name="gmlp_prompt",
    )(x, g, win, lg, lb, ws, bs_t, wout)


def _gmlp_sample_kernel(x_ref, g_ref, win_ref, lg_ref, lb_ref, w0_ref, b0_ref, wout_ref, o_ref, v_ref):
    x = x_ref[...]
    h = _rms(x, g_ref[...]).astype(BF16)
    z = jax.nn.gelu(jnp.dot(h, win_ref[...], preferred_element_type=F32))
    v = _layer_norm(z[:, C_WIDTH:], lg_ref[...], lb_ref[...])
    v_ref[...] = v
    gated = z[:, :C_WIDTH] * (w0_ref[...] * v + b0_ref[...])
    o_ref[...] = x + jnp.dot(gated.astype(BF16), wout_ref[...], preferred_element_type=F32)


def _gmlp_sample(x, g, win, lg, lb, w0, b0, wout, layer):
    n = x.shape[0]
    return pl.pallas_call(
        _gmlp_sample_kernel,
        grid=(1,),
        in_specs=[_const_spec((n, D_MODEL)), _const_spec((1, D_MODEL)), _layer_spec((D_MODEL, 2 * C_WIDTH), layer),
                  _const_spec((1, C_WIDTH)), _const_spec((1, C_WIDTH)), _const_spec((1, C_WIDTH)),
                  _const_spec((1, C_WIDTH)), _layer_spec((C_WIDTH, D_MODEL), layer)],
        out_specs=[_const_spec((n, D_MODEL)), _const_spec((n, C_WIDTH))],
        out_shape=[jax.ShapeDtypeStruct((n, D_MODEL), F32), jax.ShapeDtypeStruct((n, C_WIDTH), F32)],
        compiler_params=_params(1),
        name="gmlp_sample",
    )(x, g, win, lg, lb, w0, b0, wout)


META_W = 8


def _split_bf16(a):
    hi = a.astype(BF16)
    return hi, (a - hi.astype(F32)).astype(BF16)


def _router_kernel(x_ref, g_ref, wr_ref, *rest, bt):
    hn_ref, cm_ref, lst_ref, cnt_ref = rest[-4:]
    h = _rms(x_ref[...], g_ref[...])
    tm = h.shape[0]
    h_hi, h_lo = _split_bf16(h)
    w_hi, w_lo = _split_bf16(wr_ref[...])
    hn_ref[...] = h_hi
    logits = (jnp.dot(h_hi, w_hi, preferred_element_type=F32) + jnp.dot(h_lo, w_hi, preferred_element_type=F32)
              + jnp.dot(h_hi, w_lo, preferred_element_type=F32))
    lane = lax.broadcasted_iota(jnp.int32, (bt, 128), 1).astype(F32)
    neg = -jnp.inf
    ri = lax.broadcasted_iota(jnp.int32, (bt, bt), 0)
    ci = lax.broadcasted_iota(jnp.int32, (bt, bt), 1)
    earlier = jnp.where(ri > ci, 1.0, 0.0).astype(BF16)
    ei = lax.broadcasted_iota(jnp.int32, (128, 128), 0)
    ej = lax.broadcasted_iota(jnp.int32, (128, 128), 1)
    lower_experts = jnp.where(ei < ej, 1.0, 0.0).astype(BF16)
    for j in range(tm // bt):
        rows = slice(j * bt, (j + 1) * bt)
        lg = jnp.where(lane < N_EXPERTS, logits[rows], neg)
        m1 = jnp.max(lg, axis=-1, keepdims=True)
        i1 = jnp.min(jnp.where(lg == m1, lane, 128.0), axis=-1, keepdims=True)
        lg2 = jnp.where(lane == i1, neg, lg)
        m2 = jnp.max(lg2, axis=-1, keepdims=True)
        i2 = jnp.min(jnp.where(lg2 == m2, lane, 128.0), axis=-1, keepdims=True)
        e = jnp.exp(m2 - m1)
        g1 = 1.0 / (1.0 + e)
        g2 = e / (1.0 + e)
        sel1 = lane == i1
        sel2 = lane == i2
        oh = jnp.where(sel1, 1.0, jnp.where(sel2, 1.0, 0.0))
        before = jnp.dot(earlier, oh.astype(BF16), preferred_element_type=F32)
        cnt = jnp.sum(oh, axis=0, keepdims=True)
        padded = jnp.floor((cnt + (SEG_ALIGN - 1)) * (1.0 / SEG_ALIGN)) * SEG_ALIGN
        start = jnp.dot(jnp.broadcast_to(padded, (8, 128)).astype(BF16), lower_experts,
                        preferred_element_type=F32)[0:1]
        slot = before + start
        l1 = jnp.sum(jnp.where(sel1, slot, 0.0), axis=-1, keepdims=True)
        l2 = jnp.sum(jnp.where(sel2, slot, 0.0), axis=-1, keepdims=True)
        rec = jnp.where(lane == 0, l1, jnp.where(lane == 1, l2, jnp.where(lane == 2, g1, jnp.where(lane == 3, g2, 0.0))))
        cm_ref[rows, :] = rec[:, :META_W]
        lst_ref[j] = rec.T[0:2, :].astype(jnp.int32)
        cnt_ref[j] = cnt


def _router(x, g, wr_pad, tm, bt, row_blk0, prev=None):
    n = x.shape[0]
    nblk = tm // bt
    in_specs = [pl.BlockSpec((tm, D_MODEL), lambda i: (i, 0)), _const_spec((1, D_MODEL)),
                _const_spec((D_MODEL, 128))]
    args = [x, g, wr_pad]
    aliases = {}
    if prev is not None:
        in_specs += [pl.BlockSpec(memory_space=pl.ANY), pl.BlockSpec(memory_space=pl.ANY)]
        args += list(prev)
        aliases = {3: 0, 4: 1}
    return pl.pallas_call(
        functools.partial(_router_kernel, bt=bt),
        grid=(n // tm,),
        in_specs=in_specs,
        out_specs=[pl.BlockSpec((tm, D_MODEL), lambda i: (row_blk0 + i, 0)),
                   pl.BlockSpec((tm, META_W), lambda i: (row_blk0 + i, 0)),
                   pl.BlockSpec((nblk, 2, bt), lambda i: (i, 0, 0)),
                   pl.BlockSpec((nblk, 1, 128), lambda i: (i, 0, 0))],
        out_shape=[jax.ShapeDtypeStruct((N_TOK, D_MODEL), BF16), jax.ShapeDtypeStruct((N_TOK, META_W), F32),
                   jax.ShapeDtypeStruct((n // bt, 2, bt), jnp.int32), jax.ShapeDtypeStruct((n // bt, 1, 128), F32)],
        input_output_aliases=aliases,
        compiler_params=_params(1),
        name="moe_router",
    )(*args)


def _local_rows(bt):
    return 2 * bt + N_EXPERTS * SEG_ALIGN


def _segment_copies(seg_n, seg_src, seg_dst, blk, local_ref, slot_ref, sem, bt, to_slots, action):
    def per_expert(e, c):
        s = blk * N_EXPERTS + e
        n, src, dst = seg_n[s], seg_src[s], seg_dst[s]
        off = jnp.int32(0)
        p = bt
        while p >= SEG_ALIGN:
            hit = (n & p) != 0

            @pl.when(hit)
            def _(off=off, p=p):
                a = local_ref.at[pl.ds(pl.multiple_of(src + off, SEG_ALIGN), p)]
                b = slot_ref.at[pl.ds(pl.multiple_of(dst + off, SEG_ALIGN), p)]
                action(pltpu.make_async_copy(a, b, sem) if to_slots else pltpu.make_async_copy(b, a, sem))

            off = off + (n & p)
            p //= 2
        return c

    lax.fori_loop(0, N_EXPERTS, per_expert, 0)


def _dispatch_kernel(seg_n, seg_src, seg_dst, hn_ref, ls_ref, *rest, bt, blk0, nb):
    slots_ref, buf_ref, sem = rest[-3:]
    i = pl.program_id(0)
    par = i % 2
    start = lambda cp: cp.start()
    wait = lambda cp: cp.wait()
    copies = lambda blk, p, action: _segment_copies(seg_n, seg_src, seg_dst, blk, buf_ref.at[p], slots_ref,
                                                     sem.at[p], bt, True, action)

    @pl.when(i >= 2)
    def _():
        copies(blk0 + i - 2, par, wait)

    rows = lax.broadcasted_iota(jnp.int32, (_local_rows(bt), bt), 0)
    ls = ls_ref[0]
    onehot = jnp.where(rows == ls[0:1, :], 1.0, jnp.where(rows == ls[1:2, :], 1.0, 0.0)).astype(BF16)
    buf_ref[par] = jnp.dot(onehot, hn_ref[...], preferred_element_type=F32)
    copies(blk0 + i, par, start)

    @pl.when(i == nb - 1)
    def _():
        if nb >= 2:
            copies(blk0 + i - 1, 1 - par, wait)
        copies(blk0 + i, par, wait)


def _dispatch(seg, hn, ls_t, bt, blk0, tok0, prev=None):
    nb = ls_t.shape[0]
    in_specs = [pl.BlockSpec((bt, D_MODEL), lambda i, *_: (tok0 // bt + i, 0)),
                pl.BlockSpec((1, 2, bt), lambda i, *_: (i, 0, 0))]
    args = [hn, ls_t]
    aliases = {}
    if prev is not None:
        in_specs.append(pl.BlockSpec(memory_space=pl.ANY))
        args.append(prev)
        aliases = {5: 0}
    return pl.pallas_call(
        functools.partial(_dispatch_kernel, bt=bt, blk0=blk0, nb=nb),
        grid_spec=pltpu.PrefetchScalarGridSpec(
            num_scalar_prefetch=3,
            grid=(nb,),
            in_specs=in_specs,
            out_specs=pl.BlockSpec(memory_space=pl.ANY),
            scratch_shapes=[pltpu.VMEM((2, _local_rows(bt), D_MODEL), F32), pltpu.SemaphoreType.DMA((2,))],
        ),
        out_shape=jax.ShapeDtypeStruct((N_SLOTS, D_MODEL), F32),
        input_output_aliases=aliases,
        compiler_params=_params(1),
        name="moe_dispatch",
    )(*seg, *args)


def _experts_kernel(te_ref, tb_ref, nv_ref, x_ref, wg_ref, wu_ref, wd_ref, o_ref, xb_ref):
    t = pl.program_id(0)
    f = pl.program_id(1)
    nv = nv_ref[t]

    @pl.when(jnp.logical_and(nv > 0, f == 0))
    def _():
        rows = lax.broadcasted_iota(jnp.int32, (MOE_TILE, 1), 0)
        xb_ref[...] = jnp.where(rows < nv, x_ref[...], 0.0).astype(BF16)
        o_ref[...] = jnp.zeros_like(o_ref)

    def swiglu_rows(n_rows):
        rows = slice(0, n_rows)
        xb = xb_ref[rows, :]
        for c in range(MOE_FF_TILE // MOE_FF_SUB):
            cols = slice(c * MOE_FF_SUB, (c + 1) * MOE_FF_SUB)
            gate = jnp.dot(xb, wg_ref[:, cols].astype(BF16), preferred_element_type=F32)
            up = jnp.dot(xb, wu_ref[:, cols].astype(BF16), preferred_element_type=F32)
            act = (gate * jax.nn.sigmoid(gate) * up).astype(BF16)
            o_ref[rows, :] += jnp.dot(act, wd_ref[cols, :].astype(BF16), preferred_element_type=F32)

    lo = 0
    for n_rows in MOE_ROW_PATHS:
        @pl.when(jnp.logical_and(nv > lo, nv <= n_rows))
        def _(n_rows=n_rows):
            swiglu_rows(n_rows)
        lo = n_rows


def _experts(tile_expert, tile_blk, tile_rows, xs, wg, wu, wd, layer):
    def chunk(t, f, nv):
        return jnp.where(nv[t] > 0, f, MOE_NF - 1)
    up_spec = pl.BlockSpec((None, None, D_MODEL, MOE_FF_TILE),
                           lambda t, f, te, tb, nv: (layer, te[t], 0, chunk(t, f, nv)))
    down_spec = pl.BlockSpec((None, None, MOE_FF_TILE, D_MODEL),
                             lambda t, f, te, tb, nv: (layer, te[t], chunk(t, f, nv), 0))
    return pl.pallas_call(
        _experts_kernel,
        grid_spec=pltpu.PrefetchScalarGridSpec(
            num_scalar_prefetch=3,
            grid=(MOE_MAX_TILES, MOE_NF),
            in_specs=[pl.BlockSpec((MOE_TILE, D_MODEL), lambda t, f, te, tb, nv: (tb[t], 0)),
                      up_spec, up_spec, down_spec],
            out_specs=pl.BlockSpec((MOE_TILE, D_MODEL), lambda t, f, te, tb, nv: (tb[t], 0)),
            scratch_shapes=[pltpu.VMEM((MOE_TILE, D_MODEL), BF16)],
        ),
        out_shape=jax.ShapeDtypeStruct((N_SLOTS, D_MODEL), F32),
        compiler_params=_params(2),
        name="moe_experts",
    )(tile_expert, tile_blk, tile_rows, xs, wg, wu, wd)


def _combine_kernel(seg_n, seg_src, seg_dst, x_ref, cm_ref, eo_ref, y_ref, buf_ref, sem, *, bt, blk0, nb):
    i = pl.program_id(0)
    par = i % 2
    start = lambda cp: cp.start()
    wait = lambda cp: cp.wait()
    copies = lambda blk, p, action: _segment_copies(seg_n, seg_src, seg_dst, blk, buf_ref.at[p], eo_ref,
                                                     sem.at[p], bt, False, action)

    @pl.when(i == 0)
    def _():
        copies(blk0, 0, start)

    @pl.when(i + 1 < nb)
    def _():
        copies(blk0 + i + 1, 1 - par, start)

    copies(blk0 + i, par, wait)
    last = (blk0 + i) * N_EXPERTS + (N_EXPERTS - 1)
    used = seg_src[last] + seg_n[last]
    rows = lax.broadcasted_iota(jnp.int32, (_local_rows(bt), 1), 0)
    e_hi, e_lo = _split_bf16(jnp.where(rows < used, buf_ref[par], 0.0))
    cm = cm_ref[...]
    cols = lax.broadcasted_iota(jnp.int32, (bt, _local_rows(bt)), 1).astype(F32)
    comb = jnp.where(cols == cm[:, 0:1], cm[:, 2:3], jnp.where(cols == cm[:, 1:2], cm[:, 3:4], 0.0))
    c_hi, c_lo = _split_bf16(comb)
    y_ref[...] = x_ref[...] + jnp.dot(c_hi, e_hi, preferred_element_type=F32)
    y_ref[...] += jnp.dot(c_lo, e_hi, preferred_element_type=F32)
    y_ref[...] += jnp.dot(c_hi, e_lo, preferred_element_type=F32)


def _combine(seg, x, cm, eo, bt, blk0, tok0):
    n = x.shape[0]
    nb = n // bt
    return pl.pallas_call(
        functools.partial(_combine_kernel, bt=bt, blk0=blk0, nb=nb),
        grid_spec=pltpu.PrefetchScalarGridSpec(
            num_scalar_prefetch=3,
            grid=(nb,),
            in_specs=[pl.BlockSpec((bt, D_MODEL), lambda i, *_: (i, 0)),
                      pl.BlockSpec((bt, META_W), lambda i, *_: (tok0 // bt + i, 0)),
                      pl.BlockSpec(memory_space=pl.ANY)],
            out_specs=pl.BlockSpec((bt, D_MODEL), lambda i, *_: (i, 0)),
            scratch_shapes=[pltpu.VMEM((2, _local_rows(bt), D_MODEL), F32), pltpu.SemaphoreType.DMA((2,))],
        ),
        out_shape=jax.ShapeDtypeStruct((n, D_MODEL), F32),
        compiler_params=_params(1),
        name="moe_combine",
    )(*seg, x, cm, eo)


def _moe(xp, xs, g, w_router, wg, wu, wd, layer):
    g2 = g.reshape(1, D_MODEL)
    wr_pad = jnp.pad(w_router, ((0, 0), (0, 128 - N_EXPERTS)))
    hn, cm, ls_p, cnt_p = _router(xp, g2, wr_pad, ROW_TILE, MOE_BLOCK, 0)
    hn, cm, ls_s, cnt_s = _router(xs, g2, wr_pad, DEC_BATCH, DEC_BATCH, N_PROMPT // DEC_BATCH, prev=(hn, cm))

    nbp = N_PROMPT // MOE_BLOCK
    seg_cnt = jnp.concatenate([cnt_p[:, 0, :N_EXPERTS], cnt_s[:, 0, :N_EXPERTS]], axis=0).astype(jnp.int32)
    seg_n = (seg_cnt + (SEG_ALIGN - 1)) // SEG_ALIGN * SEG_ALIGN
    seg_src = jnp.cumsum(seg_n, axis=1) - seg_n
    slot0 = jnp.cumsum(seg_n, axis=0) - seg_n

    counts = seg_n.sum(axis=0)
    tiles_e = (counts + (MOE_TILE - 1)) // MOE_TILE
    tile_end = jnp.cumsum(tiles_e)
    tile_start = tile_end - tiles_e
    n_tiles = tile_end[-1]
    t = jnp.arange(MOE_MAX_TILES, dtype=jnp.int32)
    tc = jnp.minimum(t, n_tiles - 1)
    tile_expert = jnp.sum((tc[:, None] >= tile_end[None, :]).astype(jnp.int32), axis=1)
    rows_left = counts[tile_expert] - (tc - tile_start[tile_expert]) * MOE_TILE
    tile_rows = jnp.where(t < n_tiles, jnp.minimum(rows_left, MOE_TILE), 0).astype(jnp.int32)
    seg_dst = (tile_start * MOE_TILE)[None, :] + slot0

    cap = jnp.concatenate([jnp.full((nbp, 1), MOE_BLOCK, jnp.int32), jnp.full((1, 1), DEC_BATCH, jnp.int32)])
    seg_n = jnp.clip(seg_n, 0, cap)
    seg = (seg_n.reshape(-1), jnp.clip(seg_src, 0, 2 * cap + N_EXPERTS * SEG_ALIGN - seg_n).reshape(-1),
           jnp.clip(seg_dst, 0, N_SLOTS - MOE_BLOCK).reshape(-1))

    slots = _dispatch(seg, hn, ls_p, MOE_BLOCK, 0, 0)
    slots = _dispatch(seg, hn, ls_s, DEC_BATCH, nbp, N_PROMPT, prev=slots)
    eo = _experts(tile_expert.astype(jnp.int32), tc, tile_rows, slots, wg, wu, wd, layer)
    yp = _combine(seg, xp, cm, eo, MOE_BLOCK, 0, 0)
    ys = _combine(seg, xs, cm, eo, DEC_BATCH, nbp, N_PROMPT)
    return yp, ys


def kernel(x_prompt, x_sample, state_conv, cache_k, cache_v, norm_mix_g, norm_ffn_g, w_in_ab, conv_w, conv_b,
           conv_ln_g, conv_ln_b, q_norm_g, k_norm_g, attn_sinks, w_out_ab, w_gate_dense, w_up_dense, w_down_dense,
           w_in_c, c_ln_g, c_ln_b, w_spatial, b_spatial, w_out_c, w_router, w_gate_exp, w_up_exp, w_down_exp):
    xp = x_prompt.reshape(N_PROMPT, D_MODEL)
    xs = x_sample.reshape(DEC_BATCH, D_MODEL)
    head = jnp.arange(B_WIDTH) // HEAD_DIM
    seg = (head[:, None] == head[None, :]).astype(BF16)
    conv_p, conv_s, kp_out, vp_out, ks_out, vs_out, vc_out = [], [], [], [], [], [], []
    row = lambda a: a.reshape(1, -1)
    wo_all, wg_all = w_out_ab.astype(BF16), w_gate_dense.astype(BF16)
    wu_all, wd_all = w_up_dense.astype(BF16), w_down_dense.astype(BF16)
    w_in_all, win_c_all, wout_c_all = w_in_ab.astype(BF16), w_in_c.astype(BF16), w_out_c.astype(BF16)
    kc_all = cache_k.reshape(N_AB, DEC_BATCH, WINDOW, KV_WIDTH)
    vc_all = cache_v.reshape(N_AB, DEC_BATCH, WINDOW, KV_WIDTH)
    for layer in range(DEPTH):
        i = layer // 2
        g_mix = row(norm_mix_g[layer])
        g_ffn = row(norm_ffn_g[layer])
        if layer % 2 == 0:
            qg = row(jnp.tile(q_norm_g[i], N_HEADS))
            kg = row(jnp.tile(k_norm_g[i], N_KV_HEADS))
            a_p, q_p, k_p, v_p = _inproj(xp, g_mix, w_in_all, qg, kg, seg, ROW_TILE, i)
            a_s, q_s, k_s, v_s = _inproj(xs, g_mix, w_in_all, qg, kg, seg, DEC_BATCH, i)
            cb, lg, lb = row(conv_b[i]), row(conv_ln_g[i]), row(conv_ln_b[i])
            ca_p = _conv_prompt(a_p, conv_w[i], cb, lg, lb)
            ca_s = _conv_sample(a_s, state_conv, conv_w[i], cb, lg, lb, i)
            ob_p = _attn_prompt(q_p, k_p, v_p, attn_sinks[i])
            sinks_b = jnp.broadcast_to(attn_sinks[i].reshape(N_KV_HEADS, Q_GROUP, 1), (N_KV_HEADS, Q_GROUP, 128))
            ob_s = _attn_sample(q_s.reshape(DEC_BATCH, N_KV_HEADS, Q_GROUP, HEAD_DIM),
                                k_s.reshape(DEC_BATCH, N_KV_HEADS, HEAD_DIM),
                                v_s.reshape(DEC_BATCH, N_KV_HEADS, HEAD_DIM),
                                kc_all, vc_all, sinks_b, i)
            ob_s = ob_s.reshape(DEC_BATCH, B_WIDTH)
            conv_p.append(a_p.reshape(BATCH, SEQ, A_CH)[:, SEQ - (CONV_W - 1):])
            conv_s.append(jnp.concatenate([state_conv[i][:, 1:], a_s[:, None, :]], axis=1))
            last = lambda t: (t.reshape(BATCH, SEQ, KV_WIDTH)[:, SEQ - WINDOW:]
                              .reshape(BATCH, WINDOW, N_KV_HEADS, HEAD_DIM))
            kp_out.append(last(k_p))
            vp_out.append(last(v_p))
            ks_out.append(jnp.concatenate(
                [cache_k[i][:, 1:], k_s.reshape(DEC_BATCH, 1, N_KV_HEADS, HEAD_DIM)], axis=1))
            vs_out.append(jnp.concatenate(
                [cache_v[i][:, 1:], v_s.reshape(DEC_BATCH, 1, N_KV_HEADS, HEAD_DIM)], axis=1))
            xp = _outproj_ffn(xp, ca_p, ob_p, wo_all, g_ffn, wg_all, wu_all, wd_all, ROW_TILE, i)
            xs = _outproj_ffn(xs, ca_s, ob_s, wo_all, g_ffn, wg_all, wu_all, wd_all, DEC_BATCH, i)
        else:
            lg, lb = row(c_ln_g[i]), row(c_ln_b[i])
            xp = _gmlp_prompt(xp, g_mix, win_c_all, lg, lb, w_spatial, b_spatial[i].T, wout_c_all, ROW_TILE, i)
            gd = C_WIDTH // C_GROUPS
            w0 = row(jnp.repeat(w_spatial[i][:, 0, 0], gd))
            b0 = row(jnp.repeat(b_spatial[i][:, 0], gd))
            xs, v_s = _gmlp_sample(xs, g_mix, win_c_all, lg, lb, w0, b0, wout_c_all, i)
            vc_out.append(v_s.reshape(DEC_BATCH, 1, C_WIDTH))
            xp, xs = _moe(xp, xs, norm_ffn_g[layer], w_router[i], w_gate_exp, w_up_exp, w_down_exp, i)
    return (xp.reshape(BATCH, SEQ, D_MODEL), xs.reshape(DEC_BATCH, 1, D_MODEL),
            jnp.stack(conv_p), jnp.stack(conv_s), jnp.stack(kp_out), jnp.stack(vp_out),
            jnp.stack(ks_out), jnp.stack(vs_out), jnp.stack(vc_out))
```

```python
import functools

import jax
import jax.numpy as jnp
from jax import lax
from jax.experimental import pallas as pl
from jax.experimental.pallas import tpu as pltpu

F32 = jnp.float32
BF16 = jnp.bfloat16

D_MODEL = 1024
BATCH = 8
SEQ = 2048
DEPTH = 4
N_AB = (DEPTH + 1) // 2
DEC_BATCH = 128
A_CH = 512
CONV_W = 31
N_HEADS = 8
N_KV_HEADS = 2
HEAD_DIM = 64
Q_GROUP = 4
WINDOW = 128
B_WIDTH = 512
KV_WIDTH = 128
AB_IN = 2 * A_CH + B_WIDTH + 2 * KV_WIDTH
C_WIDTH = 1024
C_GROUPS = 8
C_CHUNK = 128
D_FF = 2816
N_EXPERTS = 8
D_FF_EXPERT = 3584
EPS = 1e-6
ATTN_SCALE = HEAD_DIM ** -0.5

N_PROMPT = BATCH * SEQ
N_TOK = N_PROMPT + DEC_BATCH
ROW_TILE = 1024
FF_SPLIT = 11
MOE_TILE = 1024
MOE_FF_TILE = 512
MOE_FF_SUB = 256
MOE_ROW_PATHS = (256, 512, MOE_TILE)
MOE_NF = D_FF_EXPERT // MOE_FF_TILE
MOE_BLOCK = 256
MOE_N_BLOCKS = N_PROMPT // MOE_BLOCK + 1
SEG_ALIGN = 8
MOE_MAX_TILES = (2 * N_TOK + MOE_N_BLOCKS * N_EXPERTS * (SEG_ALIGN - 1) + N_EXPERTS * (MOE_TILE - 1)) // MOE_TILE
N_SLOTS = MOE_MAX_TILES * MOE_TILE
VMEM_LIMIT = 56 * 1024 * 1024


def _params(n_axes, vmem=VMEM_LIMIT):
    return pltpu.CompilerParams(dimension_semantics=("arbitrary",) * n_axes, vmem_limit_bytes=vmem)


def _rms(x, g):
    return x * lax.rsqrt(jnp.mean(x * x, axis=-1, keepdims=True) + EPS) * g


def _layer_norm(x, g, b):
    mu = jnp.mean(x, axis=-1, keepdims=True)
    xc = x - mu
    var = jnp.mean(xc * xc, axis=-1, keepdims=True)
    return xc * lax.rsqrt(var + EPS) * g + b


def _const_spec(shape):
    return pl.BlockSpec(shape, lambda *_: (0,) * len(shape))


def _layer_spec(shape, layer):
    return pl.BlockSpec((None,) + tuple(shape), lambda *_: (layer,) + (0,) * len(shape))


def _inproj_kernel(x_ref, g_ref, w_ref, qg_ref, kg_ref, seg_ref, a_ref, q_ref, k_ref, v_ref):
    h = _rms(x_ref[...], g_ref[...]).astype(BF16)
    z = jnp.dot(h, w_ref[...], preferred_element_type=F32)
    a_ref[...] = z[:, :A_CH] * jax.nn.sigmoid(z[:, A_CH:2 * A_CH])
    seg = seg_ref[...]

    def head_norm(t, gain, seg_t):
        sq = t * t
        hi = sq.astype(BF16)
        lo = (sq - hi.astype(F32)).astype(BF16)
        ms = (jnp.dot(hi, seg_t, preferred_element_type=F32)
              + jnp.dot(lo, seg_t, preferred_element_type=F32)) * (1.0 / HEAD_DIM)
        return t * lax.rsqrt(ms + EPS) * gain

    o = 2 * A_CH
    q = head_norm(z[:, o:o + B_WIDTH], qg_ref[...], seg)
    q_ref[...] = (q * ATTN_SCALE).astype(BF16)
    o += B_WIDTH
    k_ref[...] = head_norm(z[:, o:o + KV_WIDTH], kg_ref[...], seg[:KV_WIDTH, :KV_WIDTH])
    o += KV_WIDTH
    v_ref[...] = z[:, o:o + KV_WIDTH]


def _inproj(x, g, w_bf, qg, kg, seg, tm, layer):
    n = x.shape[0]
    row = lambda w: pl.BlockSpec((tm, w), lambda i: (i, 0))
    return pl.pallas_call(
        _inproj_kernel,
        grid=(n // tm,),
        in_specs=[row(D_MODEL), _const_spec((1, D_MODEL)), _layer_spec((D_MODEL, AB_IN), layer),
                  _const_spec((1, B_WIDTH)), _const_spec((1, KV_WIDTH)), _const_spec((B_WIDTH, B_WIDTH))],
        out_specs=[row(A_CH), row(B_WIDTH), row(KV_WIDTH), row(KV_WIDTH)],
        out_shape=[jax.ShapeDtypeStruct((n, A_CH), F32), jax.ShapeDtypeStruct((n, B_WIDTH), BF16),
                   jax.ShapeDtypeStruct((n, KV_WIDTH), F32), jax.ShapeDtypeStruct((n, KV_WIDTH), F32)],
        compiler_params=_params(1),
        name="inproj_ab",
    )(x, g, w_bf, qg, kg, seg)


CONV_PAD = 32
CONV_ROWS = 128
CONV_SUB = 32


def _conv_tail(c, lg, lb):
    y = _layer_norm(c, lg, lb)
    return y * jax.nn.sigmoid(y)


def _conv_prompt_kernel(a_ref, w_ref, b_ref, lg_ref, lb_ref, o_ref, pad_ref, sh_ref, c_ref):
    pad_ref[0:CONV_PAD, :] = jnp.zeros((CONV_PAD, A_CH), F32)
    pad_ref[CONV_PAD:, :] = a_ref[...]
    win = CONV_ROWS + CONV_PAD

    def body(r, carry):
        t0 = pl.multiple_of(r * CONV_ROWS, CONV_ROWS)
        nc = A_CH // 128
        for c in range(nc):
            sh_ref[0, c] = pad_ref[pl.ds(t0, win), c * 128:(c + 1) * 128]
        for s in range(1, 8):
            for c in range(nc):
                sh_ref[s, c, 0:win - 8, :] = sh_ref[0, c, s:s + win - 8, :]
        first = CONV_PAD - (CONV_W - 1)

        def chunk(c, carry2):
            for r0 in range(0, CONV_ROWS, CONV_SUB):
                acc = jnp.zeros((CONV_SUB, 128), F32)
                for s in range(8):
                    taps = [k for k in range(CONV_W) if (first + k) % 8 == s]
                    span = max(first + k - s for k in taps) + CONV_SUB
                    rows = sh_ref[s, c, r0:r0 + span, :]
                    for k in taps:
                        base = first + k - s
                        acc = acc + rows[base:base + CONV_SUB] * w_ref[c, k:k + 1, :]
                c_ref[c, r0:r0 + CONV_SUB, :] = acc
            return carry2

        lax.fori_loop(0, nc, chunk, 0)
        conv = jnp.concatenate([c_ref[c] for c in range(nc)], axis=-1)
        y = _conv_tail(conv + b_ref[...], lg_ref[...], lb_ref[...])
        o_ref[pl.ds(t0, CONV_ROWS), :] = y.astype(BF16)
        return carry

    lax.fori_loop(0, SEQ // CONV_ROWS, body, 0)


def _conv_prompt(a, w, b, lg, lb):
    nc = A_CH // 128
    w_chunks = w.reshape(CONV_W, nc, 128).transpose(1, 0, 2)
    return pl.pallas_call(
        _conv_prompt_kernel,
        grid=(BATCH,),
        in_specs=[pl.BlockSpec((SEQ, A_CH), lambda i: (i, 0)), _const_spec((nc, CONV_W, 128)),
                  _const_spec((1, A_CH)), _const_spec((1, A_CH)), _const_spec((1, A_CH))],
        out_specs=pl.BlockSpec((SEQ, A_CH), lambda i: (i, 0)),
        out_shape=jax.ShapeDtypeStruct((N_PROMPT, A_CH), BF16),
        scratch_shapes=[pltpu.VMEM((CONV_PAD + SEQ, A_CH), F32),
                        pltpu.VMEM((8, nc, CONV_ROWS + CONV_PAD, 128), F32),
                        pltpu.VMEM((nc, CONV_ROWS, 128), F32)],
        compiler_params=_params(1),
        name="conv_prompt",
    )(a, w_chunks, b, lg, lb)


def _conv_sample_kernel(a_ref, st_ref, w_ref, b_ref, lg_ref, lb_ref, o_ref):
    acc = a_ref[...] * w_ref[CONV_W - 1:CONV_W, :]
    acc = acc + jnp.sum(st_ref[...] * w_ref[0:CONV_W - 1, :][None], axis=1)
    o_ref[...] = _conv_tail(acc + b_ref[...], lg_ref[...], lb_ref[...]).astype(BF16)


def _conv_sample(a, st_t, w, b, lg, lb, layer):
    return pl.pallas_call(
        _conv_sample_kernel,
        grid=(1,),
        in_specs=[_const_spec((DEC_BATCH, A_CH)), _layer_spec((DEC_BATCH, CONV_W - 1, A_CH), layer),
                  _const_spec((CONV_W, A_CH)), _const_spec((1, A_CH)), _const_spec((1, A_CH)),
                  _const_spec((1, A_CH))],
        out_specs=_const_spec((DEC_BATCH, A_CH)),
        out_shape=jax.ShapeDtypeStruct((DEC_BATCH, A_CH), BF16),
        compiler_params=_params(1),
        name="conv_sample",
    )(a, st_t, w, b, lg, lb)


def _alibi_slope(h):
    return 2.0 ** (-8.0 * (h + 1) / N_HEADS)


ATTN_P_WINDOWS = 4


def _attn_prompt_kernel(sink_ref, q_ref, kp_ref, kc_ref, vp_ref, vc_ref, o_ref):
    step = pl.program_id(1)
    kk = jnp.concatenate([kp_ref[...], kc_ref[...]], axis=0).astype(BF16)
    vv = jnp.concatenate([vp_ref[...], vc_ref[...]], axis=0).astype(BF16)
    qi = lax.broadcasted_iota(jnp.int32, (WINDOW, 2 * WINDOW), 0)
    kj = lax.broadcasted_iota(jnp.int32, (WINDOW, 2 * WINDOW), 1)
    dist_i = WINDOW + qi - kj
    in_window = (dist_i >= 0) & (dist_i < WINDOW)
    lo = jnp.where(step > 0, 0, WINDOW)
    dist = dist_i.astype(F32)
    for w in range(ATTN_P_WINDOWS):
        valid = in_window & (kj >= lo) if w == 0 else in_window
        rows = slice(w * WINDOW, (w + 1) * WINDOW)
        keys = slice(w * WINDOW, (w + 2) * WINDOW)
        outs = []
        for h in range(N_HEADS):
            kv = h // Q_GROUP
            lanes = slice(kv * HEAD_DIM, (kv + 1) * HEAD_DIM)
            qh = q_ref[rows, h * HEAD_DIM:(h + 1) * HEAD_DIM]
            s = lax.dot_general(qh, kk[keys, lanes], (((1,), (1,)), ((), ())), preferred_element_type=F32)
            s = jnp.where(valid, s - _alibi_slope(h) * dist, -jnp.inf)
            sink = sink_ref[h]
            m = jnp.maximum(jnp.max(s, axis=-1, keepdims=True), sink)
            p = jnp.exp(s - m)
            denom = jnp.sum(p, axis=-1, keepdims=True) + jnp.exp(sink - m)
            p = (p / denom).astype(BF16)
            outs.append(jnp.dot(p, vv[keys, lanes], preferred_element_type=F32))
        o_ref[rows, :] = jnp.concatenate(outs, axis=-1).astype(BF16)


def _attn_prompt(q, k, v, sinks):
    rows = ATTN_P_WINDOWS * WINDOW
    ns = SEQ // rows
    own = lambda b, j: (b * ns + j, 0)
    prev = lambda b, j: (b * (SEQ // WINDOW) + jnp.maximum(j * ATTN_P_WINDOWS - 1, 0), 0)
    return pl.pallas_call(
        _attn_prompt_kernel,
        grid=(BATCH, ns),
        in_specs=[pl.BlockSpec(memory_space=pltpu.SMEM),
                  pl.BlockSpec((rows, B_WIDTH), own),
                  pl.BlockSpec((WINDOW, KV_WIDTH), prev), pl.BlockSpec((rows, KV_WIDTH), own),
                  pl.BlockSpec((WINDOW, KV_WIDTH), prev), pl.BlockSpec((rows, KV_WIDTH), own)],
        out_specs=pl.BlockSpec((rows, B_WIDTH), own),
        out_shape=jax.ShapeDtypeStruct((N_PROMPT, B_WIDTH), BF16),
        compiler_params=_params(2),
        name="attn_prompt",
    )(sinks, q, k, k, v, v)


ATTN_S_BLK = 32


def _attn_sample_kernel(sink_ref, q_ref, kn_ref, vn_ref, kc_ref, vc_ref, o_ref):
    kc = kc_ref[...].astype(BF16)
    vc = vc_ref[...].astype(BF16)
    j = lax.broadcasted_iota(jnp.int32, (1, 1, WINDOW), 2)
    dist = (WINDOW - j).astype(F32)
    g_idx = lax.broadcasted_iota(jnp.int32, (1, Q_GROUP, 1), 1)
    for kv in range(N_KV_HEADS):
        lanes = slice(kv * HEAD_DIM, (kv + 1) * HEAD_DIM)
        qg = q_ref[:, kv]
        slope = jnp.exp2(-(g_idx + (kv * Q_GROUP + 1)).astype(F32) * (8.0 / N_HEADS))
        s = jnp.einsum('ngd,njd->ngj', qg, kc[:, :, lanes], preferred_element_type=F32)
        s = jnp.where(j >= 1, s - slope * dist, -jnp.inf)
        kn = kn_ref[:, kv].astype(BF16).astype(F32)
        s_new = jnp.sum(qg.astype(F32) * kn[:, None, :], axis=-1, keepdims=True)
        sink = sink_ref[kv][:, :1][None]
        m = jnp.maximum(jnp.maximum(jnp.max(s, axis=-1, keepdims=True), s_new), sink)
        p = jnp.exp(s - m)
        p_new = jnp.exp(s_new - m)
        denom = jnp.sum(p, axis=-1, keepdims=True) + p_new + jnp.exp(sink - m)
        o = jnp.einsum('ngj,njd->ngd', (p / denom).astype(BF16), vc[:, :, lanes], preferred_element_type=F32)
        vn = vn_ref[:, kv].astype(BF16).astype(F32)
        o = o + (p_new / denom).astype(BF16).astype(F32) * vn[:, None, :]
        o_ref[:, kv] = o.astype(BF16)


def _attn_sample(q4, kn3, vn3, kc, vc, sinks_b, layer):
    nb = ATTN_S_BLK
    return pl.pallas_call(
        _attn_sample_kernel,
        grid=(DEC_BATCH // nb,),
        in_specs=[_const_spec((N_KV_HEADS, Q_GROUP, 128)),
                  pl.BlockSpec((nb, N_KV_HEADS, Q_GROUP, HEAD_DIM), lambda i: (i, 0, 0, 0)),
                  pl.BlockSpec((nb, N_KV_HEADS, HEAD_DIM), lambda i: (i, 0, 0)),
                  pl.BlockSpec((nb, N_KV_HEADS, HEAD_DIM), lambda i: (i, 0, 0)),
                  pl.BlockSpec((None, nb, WINDOW, KV_WIDTH), lambda i: (layer, i, 0, 0)),
                  pl.BlockSpec((None, nb, WINDOW, KV_WIDTH), lambda i: (layer, i, 0, 0))],
        out_specs=pl.BlockSpec((nb, N_KV_HEADS, Q_GROUP, HEAD_DIM), lambda i: (i, 0, 0, 0)),
        out_shape=jax.ShapeDtypeStruct((DEC_BATCH, N_KV_HEADS, Q_GROUP, HEAD_DIM), BF16),
        compiler_params=_params(1),
        name="attn_sample",
    )(sinks_b, q4, kn3, vn3, kc, vc)


def _outproj_ffn_kernel(x_ref, ca_ref, ob_ref, g_ref, wo_hbm, wg_hbm, wu_hbm, wd_hbm, o_ref,
                        wo_ref, wg_ref, wu_ref, wd_ref, *, layer):
    @pl.when(pl.program_id(0) == 0)
    def _():
        pltpu.sync_copy(wo_hbm.at[layer], wo_ref)
        pltpu.sync_copy(wg_hbm.at[layer], wg_ref)
        pltpu.sync_copy(wu_hbm.at[layer], wu_ref)
        pltpu.sync_copy(wd_hbm.at[layer], wd_ref)

    o_ref[...] = (x_ref[...]
                  + jnp.dot(ca_ref[...], wo_ref[:A_CH, :], preferred_element_type=F32)
                  + jnp.dot(ob_ref[...], wo_ref[A_CH:, :], preferred_element_type=F32))
    h = _rms(o_ref[...], g_ref[...]).astype(BF16)
    ff = D_FF // FF_SPLIT
    for c in range(FF_SPLIT):
        cols = slice(c * ff, (c + 1) * ff)
        gate = jnp.dot(h, wg_ref[:, cols], preferred_element_type=F32)
        up = jnp.dot(h, wu_ref[:, cols], preferred_element_type=F32)
        act = (gate * jax.nn.sigmoid(gate) * up).astype(BF16)
        o_ref[...] += jnp.dot(act, wd_ref[cols, :], preferred_element_type=F32)


def _outproj_ffn(x, ca, ob, wo, g, wg, wu, wd, tm, layer):
    n = x.shape[0]
    row = lambda w: pl.BlockSpec((tm, w), lambda i: (i, 0))
    hbm = pl.BlockSpec(memory_space=pl.ANY)
    return pl.pallas_call(
        functools.partial(_outproj_ffn_kernel, layer=layer),
        grid=(n // tm,),
        in_specs=[row(D_MODEL), row(A_CH), row(B_WIDTH), _const_spec((1, D_MODEL)), hbm, hbm, hbm, hbm],
        out_specs=row(D_MODEL),
        out_shape=jax.ShapeDtypeStruct((n, D_MODEL), F32),
        scratch_shapes=[pltpu.VMEM((D_MODEL, D_MODEL), BF16), pltpu.VMEM((D_MODEL, D_FF), BF16),
                        pltpu.VMEM((D_MODEL, D_FF), BF16), pltpu.VMEM((D_FF, D_MODEL), BF16)],
        compiler_params=_params(1),
        name="outproj_ffn",
    )(x, ca, ob, g, wo, wg, wu, wd)


def _gmlp_prompt_kernel(x_ref, g_ref, win_ref, lg_ref, lb_ref, ws_ref, bs_ref, wout_ref, o_ref, gate_ref):
    x = x_ref[...]
    h = _rms(x, g_ref[...]).astype(BF16)
    z = jax.nn.gelu(jnp.dot(h, win_ref[...], preferred_element_type=F32))
    u = z[:, :C_WIDTH]
    v = _layer_norm(z[:, C_WIDTH:], lg_ref[...], lb_ref[...]).astype(BF16)
    ri = lax.broadcasted_iota(jnp.int32, (C_CHUNK, C_CHUNK), 0)
    ci = lax.broadcasted_iota(jnp.int32, (C_CHUNK, C_CHUNK), 1)
    tm = x.shape[0]
    gd = C_WIDTH // C_GROUPS
    for grp in range(C_GROUPS):
        w = jnp.where(ri >= ci, ws_ref[grp], 0.0).astype(BF16)
        b = bs_ref[:, grp:grp + 1]
        lanes = slice(grp * gd, (grp + 1) * gd)
        for c in range(tm // C_CHUNK):
            rows = slice(c * C_CHUNK, (c + 1) * C_CHUNK)
            mixed = jnp.dot(w, v[rows, lanes], preferred_element_type=F32) + b
            gate_ref[rows, lanes] = (u[rows, lanes] * mixed).astype(BF16)
    o_ref[...] = x + jnp.dot(gate_ref[...], wout_ref[...], preferred_element_type=F32)


def _gmlp_prompt(x, g, win, lg, lb, ws, bs_t, wout, tm, layer):
    n = x.shape[0]
    row = lambda w: pl.BlockSpec((tm, w), lambda i: (i, 0))
    return pl.pallas_call(
        _gmlp_prompt_kernel,
        grid=(n // tm,),
        in_specs=[row(D_MODEL), _const_spec((1, D_MODEL)), _layer_spec((D_MODEL, 2 * C_WIDTH), layer),
                  _const_spec((1, C_WIDTH)), _const_spec((1, C_WIDTH)),
                  _layer_spec((C_GROUPS, C_CHUNK, C_CHUNK), layer), _const_spec((C_CHUNK, C_GROUPS)),
                  _layer_spec((C_WIDTH, D_MODEL), layer)],
        out_specs=row(D_MODEL),
        out_shape=jax.ShapeDtypeStruct((n, D_MODEL), F32),
        scratch_shapes=[pltpu.VMEM((tm, C_WIDTH), BF16)],
        compiler_params=_params(1),
        name="gmlp_prompt",
    )(x, g, win, lg, lb, ws, bs_t, wout)


def _gmlp_sample_kernel(x_ref, g_ref, win_ref, lg_ref, lb_ref, w0_ref, b0_ref, wout_ref, o_ref, v_ref):
    x = x_ref[...]
    h = _rms(x, g_ref[...]).astype(BF16)
    z = jax.nn.gelu(jnp.dot(h, win_ref[...], preferred_element_type=F32))
    v = _layer_norm(z[:, C_WIDTH:], lg_ref[...], lb_ref[...])
    v_ref[...] = v
    gated = z[:, :C_WIDTH] * (w0_ref[...] * v + b0_ref[...])
    o_ref[...] = x + jnp.dot(gated.astype(BF16), wout_ref[...], preferred_element_type=F32)


def _gmlp_sample(x, g, win, lg, lb, w0, b0, wout, layer):
    n = x.shape[0]
    return pl.pallas_call(
        _gmlp_sample_kernel,
        grid=(1,),
        in_specs=[_const_spec((n, D_MODEL)), _const_spec((1, D_MODEL)), _layer_spec((D_MODEL, 2 * C_WIDTH), layer),
                  _const_spec((1, C_WIDTH)), _const_spec((1, C_WIDTH)), _const_spec((1, C_WIDTH)),
                  _const_spec((1, C_WIDTH)), _layer_spec((C_WIDTH, D_MODEL), layer)],
        out_specs=[_const_spec((n, D_MODEL)), _const_spec((n, C_WIDTH))],
        out_shape=[jax.ShapeDtypeStruct((n, D_MODEL), F32), jax.ShapeDtypeStruct((n, C_WIDTH), F32)],
        compiler_params=_params(1),
        name="gmlp_sample",
    )(x, g, win, lg, lb, w0, b0, wout)


META_W = 8


def _split_bf16(a):
    hi = a.astype(BF16)
    return hi, (a - hi.astype(F32)).astype(BF16)


def _router_kernel(x_ref, g_ref, wr_ref, *rest, bt):
    hn_ref, cm_ref, lst_ref, cnt_ref = rest[-4:]
    h = _rms(x_ref[...], g_ref[...])
    tm = h.shape[0]
    h_hi, h_lo = _split_bf16(h)
    w_hi, w_lo = _split_bf16(wr_ref[...])
    hn_ref[...] = h_hi
    logits = (jnp.dot(h_hi, w_hi, preferred_element_type=F32) + jnp.dot(h_lo, w_hi, preferred_element_type=F32)
              + jnp.dot(h_hi, w_lo, preferred_element_type=F32))
    lane = lax.broadcasted_iota(jnp.int32, (bt, 128), 1).astype(F32)
    neg = -jnp.inf
    ri = lax.broadcasted_iota(jnp.int32, (bt, bt), 0)
    ci = lax.broadcasted_iota(jnp.int32, (bt, bt), 1)
    earlier = jnp.where(ri > ci, 1.0, 0.0).astype(BF16)
    ei = lax.broadcasted_iota(jnp.int32, (128, 128), 0)
    ej = lax.broadcasted_iota(jnp.int32, (128, 128), 1)
    lower_experts = jnp.where(ei < ej, 1.0, 0.0).astype(BF16)
    for j in range(tm // bt):
        rows = slice(j * bt, (j + 1) * bt)
        lg = jnp.where(lane < N_EXPERTS, logits[rows], neg)
        m1 = jnp.max(lg, axis=-1, keepdims=True)
        i1 = jnp.min(jnp.where(lg == m1, lane, 128.0), axis=-1, keepdims=True)
        lg2 = jnp.where(lane == i1, neg, lg)
        m2 = jnp.max(lg2, axis=-1, keepdims=True)
        i2 = jnp.min(jnp.where(lg2 == m2, lane, 128.0), axis=-1, keepdims=True)
        e = jnp.exp(m2 - m1)
        g1 = 1.0 / (1.0 + e)
        g2 = e / (1.0 + e)
        sel1 = lane == i1
        sel2 = lane == i2
        oh = jnp.where(sel1, 1.0, jnp.where(sel2, 1.0, 0.0))
        before = jnp.dot(earlier, oh.astype(BF16), preferred_element_type=F32)
        cnt = jnp.sum(oh, axis=0, keepdims=True)
        padded = jnp.floor((cnt + (SEG_ALIGN - 1)) * (1.0 / SEG_ALIGN)) * SEG_ALIGN
        start = jnp.dot(jnp.broadcast_to(padded, (8, 128)).astype(BF16), lower_experts,
                        preferred_element_type=F32)[0:1]
        slot = before + start
        l1 = jnp.sum(jnp.where(sel1, slot, 0.0), axis=-1, keepdims=True)
        l2 = jnp.sum(jnp.where(sel2, slot, 0.0), axis=-1, keepdims=True)
        rec = jnp.where(lane == 0, l1, jnp.where(lane == 1, l2, jnp.where(lane == 2, g1, jnp.where(lane == 3, g2, 0.0))))
        cm_ref[rows, :] = rec[:, :META_W]
        lst_ref[j] = rec.T[0:2, :].astype(jnp.int32)
        cnt_ref[j] = cnt


def _router(x, g, wr_pad, tm, bt, row_blk0, prev=None):
    n = x.shape[0]
    nblk = tm // bt
    in_specs = [pl.BlockSpec((tm, D_MODEL), lambda i: (i, 0)), _const_spec((1, D_MODEL)),
                _const_spec((D_MODEL, 128))]
    args = [x, g, wr_pad]
    aliases = {}
    if prev is not None:
        in_specs += [pl.BlockSpec(memory_space=pl.ANY), pl.BlockSpec(memory_space=pl.ANY)]
        args += list(prev)
        aliases = {3: 0, 4: 1}
    return pl.pallas_call(
        functools.partial(_router_kernel, bt=bt),
        grid=(n // tm,),
        in_specs=in_specs,
        out_specs=[pl.BlockSpec((tm, D_MODEL), lambda i: (row_blk0 + i, 0)),
                   pl.BlockSpec((tm, META_W), lambda i: (row_blk0 + i, 0)),
                   pl.BlockSpec((nblk, 2, bt), lambda i: (i, 0, 0)),
                   pl.BlockSpec((nblk, 1, 128), lambda i: (i, 0, 0))],
        out_shape=[jax.ShapeDtypeStruct((N_TOK, D_MODEL), BF16), jax.ShapeDtypeStruct((N_TOK, META_W), F32),
                   jax.ShapeDtypeStruct((n // bt, 2, bt), jnp.int32), jax.ShapeDtypeStruct((n // bt, 1, 128), F32)],
        input_output_aliases=aliases,
        compiler_params=_params(1),
        name="moe_router",
    )(*args)


def _local_rows(bt):
    return 2 * bt + N_EXPERTS * SEG_ALIGN


def _segment_copies(seg_n, seg_src, seg_dst, blk, local_ref, slot_ref, sem, bt, to_slots, action):
    def per_expert(e, c):
        s = blk * N_EXPERTS + e
        n, src, dst = seg_n[s], seg_src[s], seg_dst[s]
        off = jnp.int32(0)
        p = bt
        while p >= SEG_ALIGN:
            hit = (n & p) != 0

            @pl.when(hit)
            def _(off=off, p=p):
                a = local_ref.at[pl.ds(pl.multiple_of(src + off, SEG_ALIGN), p)]
                b = slot_ref.at[pl.ds(pl.multiple_of(dst + off, SEG_ALIGN), p)]
                action(pltpu.make_async_copy(a, b, sem) if to_slots else pltpu.make_async_copy(b, a, sem))

            off = off + (n & p)
            p //= 2
        return c

    lax.fori_loop(0, N_EXPERTS, per_expert, 0)


def _dispatch_kernel(seg_n, seg_src, seg_dst, hn_ref, ls_ref, *rest, bt, blk0, nb):
    slots_ref, buf_ref, sem = rest[-3:]
    i = pl.program_id(0)
    par = i % 2
    start = lambda cp: cp.start()
    wait = lambda cp: cp.wait()
    copies = lambda blk, p, action: _segment_copies(seg_n, seg_src, seg_dst, blk, buf_ref.at[p], slots_ref,
                                                     sem.at[p], bt, True, action)

    @pl.when(i >= 2)
    def _():
        copies(blk0 + i - 2, par, wait)

    rows = lax.broadcasted_iota(jnp.int32, (_local_rows(bt), bt), 0)
    ls = ls_ref[0]
    onehot = jnp.where(rows == ls[0:1, :], 1.0, jnp.where(rows == ls[1:2, :], 1.0, 0.0)).astype(BF16)
    buf_ref[par] = jnp.dot(onehot, hn_ref[...], preferred_element_type=F32)
    copies(blk0 + i, par, start)

    @pl.when(i == nb - 1)
    def _():
        if nb >= 2:
            copies(blk0 + i - 1, 1 - par, wait)
        copies(blk0 + i, par, wait)


def _dispatch(seg, hn, ls_t, bt, blk0, tok0, prev=None):
    nb = ls_t.shape[0]
    in_specs = [pl.BlockSpec((bt, D_MODEL), lambda i, *_: (tok0 // bt + i, 0)),
                pl.BlockSpec((1, 2, bt), lambda i, *_: (i, 0, 0))]
    args = [hn, ls_t]
    aliases = {}
    if prev is not None:
        in_specs.append(pl.BlockSpec(memory_space=pl.ANY))
        args.append(prev)
        aliases = {5: 0}
    return pl.pallas_call(
        functools.partial(_dispatch_kernel, bt=bt, blk0=blk0, nb=nb),
        grid_spec=pltpu.PrefetchScalarGridSpec(
            num_scalar_prefetch=3,
            grid=(nb,),
            in_specs=in_specs,
            out_specs=pl.BlockSpec(memory_space=pl.ANY),
            scratch_shapes=[pltpu.VMEM((2, _local_rows(bt), D_MODEL), F32), pltpu.SemaphoreType.DMA((2,))],
        ),
        out_shape=jax.ShapeDtypeStruct((N_SLOTS, D_MODEL), F32),
        input_output_aliases=aliases,
        compiler_params=_params(1),
        name="moe_dispatch",
    )(*seg, *args)


def _experts_kernel(te_ref, tb_ref, nv_ref, x_ref, wg_ref, wu_ref, wd_ref, o_ref, xb_ref):
    t = pl.program_id(0)
    f = pl.program_id(1)
    nv = nv_ref[t]

    @pl.when(jnp.logical_and(nv > 0, f == 0))
    def _():
        rows = lax.broadcasted_iota(jnp.int32, (MOE_TILE, 1), 0)
        xb_ref[...] = jnp.where(rows < nv, x_ref[...], 0.0).astype(BF16)
        o_ref[...] = jnp.zeros_like(o_ref)

    def swiglu_rows(n_rows):
        rows = slice(0, n_rows)
        xb = xb_ref[rows, :]
        for c in range(MOE_FF_TILE // MOE_FF_SUB):
            cols = slice(c * MOE_FF_SUB, (c + 1) * MOE_FF_SUB)
            gate = jnp.dot(xb, wg_ref[:, cols].astype(BF16), preferred_element_type=F32)
            up = jnp.dot(xb, wu_ref[:, cols].astype(BF16), preferred_element_type=F32)
            act = (gate * jax.nn.sigmoid(gate) * up).astype(BF16)
            o_ref[rows, :] += jnp.dot(act, wd_ref[cols, :].astype(BF16), preferred_element_type=F32)

    lo = 0
    for n_rows in MOE_ROW_PATHS:
        @pl.when(jnp.logical_and(nv > lo, nv <= n_rows))
        def _(n_rows=n_rows):
            swiglu_rows(n_rows)
        lo = n_rows


def _experts(tile_expert, tile_blk, tile_rows, xs, wg, wu, wd, layer):
    def chunk(t, f, nv):
        return jnp.where(nv[t] > 0, f, MOE_NF - 1)
    up_spec = pl.BlockSpec((None, None, D_MODEL, MOE_FF_TILE),
                           lambda t, f, te, tb, nv: (layer, te[t], 0, chunk(t, f, nv)))
    down_spec = pl.BlockSpec((None, None, MOE_FF_TILE, D_MODEL),
                             lambda t, f, te, tb, nv: (layer, te[t], chunk(t, f, nv), 0))
    return pl.pallas_call(
        _experts_kernel,
        grid_spec=pltpu.PrefetchScalarGridSpec(
            num_scalar_prefetch=3,
            grid=(MOE_MAX_TILES, MOE_NF),
            in_specs=[pl.BlockSpec((MOE_TILE, D_MODEL), lambda t, f, te, tb, nv: (tb[t], 0)),
                      up_spec, up_spec, down_spec],
            out_specs=pl.BlockSpec((MOE_TILE, D_MODEL), lambda t, f, te, tb, nv: (tb[t], 0)),
            scratch_shapes=[pltpu.VMEM((MOE_TILE, D_MODEL), BF16)],
        ),
        out_shape=jax.ShapeDtypeStruct((N_SLOTS, D_MODEL), F32),
        compiler_params=_params(2),
        name="moe_experts",
    )(tile_expert, tile_blk, tile_rows, xs, wg, wu, wd)


def _combine_kernel(seg_n, seg_src, seg_dst, x_ref, cm_ref, eo_ref, y_ref, buf_ref, sem, *, bt, blk0, nb):
    i = pl.program_id(0)
    par = i % 2
    start = lambda cp: cp.start()
    wait = lambda cp: cp.wait()
    copies = lambda blk, p, action: _segment_copies(seg_n, seg_src, seg_dst, blk, buf_ref.at[p], eo_ref,
                                                     sem.at[p], bt, False, action)

    @pl.when(i == 0)
    def _():
        copies(blk0, 0, start)

    @pl.when(i + 1 < nb)
    def _():
        copies(blk0 + i + 1, 1 - par, start)

    copies(blk0 + i, par, wait)
    last = (blk0 + i) * N_EXPERTS + (N_EXPERTS - 1)
    used = seg_src[last] + seg_n[last]
    rows = lax.broadcasted_iota(jnp.int32, (_local_rows(bt), 1), 0)
    e_hi, e_lo = _split_bf16(jnp.where(rows < used, buf_ref[par], 0.0))
    cm = cm_ref[...]
    cols = lax.broadcasted_iota(jnp.int32, (bt, _local_rows(bt)), 1).astype(F32)
    comb = jnp.where(cols == cm[:, 0:1], cm[:, 2:3], jnp.where(cols == cm[:, 1:2], cm[:, 3:4], 0.0))
    c_hi, c_lo = _split_bf16(comb)
    y_ref[...] = x_ref[...] + jnp.dot(c_hi, e_hi, preferred_element_type=F32)
    y_ref[...] += jnp.dot(c_lo, e_hi, preferred_element_type=F32)
    y_ref[...] += jnp.dot(c_hi, e_lo, preferred_element_type=F32)


def _combine(seg, x, cm, eo, bt, blk0, tok0):
    n = x.shape[0]
    nb = n // bt
    return pl.pallas_call(
        functools.partial(_combine_kernel, bt=bt, blk0=blk0, nb=nb),
        grid_spec=pltpu.PrefetchScalarGridSpec(
            num_scalar_prefetch=3,
            grid=(nb,),
            in_specs=[pl.BlockSpec((bt, D_MODEL), lambda i, *_: (i, 0)),
                      pl.BlockSpec((bt, META_W), lambda i, *_: (tok0 // bt + i, 0)),
                      pl.BlockSpec(memory_space=pl.ANY)],
            out_specs=pl.BlockSpec((bt, D_MODEL), lambda i, *_: (i, 0)),
            scratch_shapes=[pltpu.VMEM((2, _local_rows(bt), D_MODEL), F32), pltpu.SemaphoreType.DMA((2,))],
        ),
        out_shape=jax.ShapeDtypeStruct((n, D_MODEL), F32),
        compiler_params=_params(1),
        name="moe_combine",
    )(*seg, x, cm, eo)


def _moe(xp, xs, g, w_router, wg, wu, wd, layer):
    g2 = g.reshape(1, D_MODEL)
    wr_pad = jnp.pad(w_router, ((0, 0), (0, 128 - N_EXPERTS)))
    hn, cm, ls_p, cnt_p = _router(xp, g2, wr_pad, ROW_TILE, MOE_BLOCK, 0)
    hn, cm, ls_s, cnt_s = _router(xs, g2, wr_pad, DEC_BATCH, DEC_BATCH, N_PROMPT // DEC_BATCH, prev=(hn, cm))

    nbp = N_PROMPT // MOE_BLOCK
    seg_cnt = jnp.concatenate([cnt_p[:, 0, :N_EXPERTS], cnt_s[:, 0, :N_EXPERTS]], axis=0).astype(jnp.int32)
    seg_n = (seg_cnt + (SEG_ALIGN - 1)) // SEG_ALIGN * SEG_ALIGN
    seg_src = jnp.cumsum(seg_n, axis=1) - seg_n
    slot0 = jnp.cumsum(seg_n, axis=0) - seg_n

    counts = seg_n.sum(axis=0)
    tiles_e = (counts + (MOE_TILE - 1)) // MOE_TILE
    tile_end = jnp.cumsum(tiles_e)
    tile_start = tile_end - tiles_e
    n_tiles = tile_end[-1]
    t = jnp.arange(MOE_MAX_TILES, dtype=jnp.int32)
    tc = jnp.minimum(t, n_tiles - 1)
    tile_expert = jnp.sum((tc[:, None] >= tile_end[None, :]).astype(jnp.int32), axis=1)
    rows_left = counts[tile_expert] - (tc - tile_start[tile_expert]) * MOE_TILE
    tile_rows = jnp.where(t < n_tiles, jnp.minimum(rows_left, MOE_TILE), 0).astype(jnp.int32)
    seg_dst = (tile_start * MOE_TILE)[None, :] + slot0

    cap = jnp.concatenate([jnp.full((nbp, 1), MOE_BLOCK, jnp.int32), jnp.full((1, 1), DEC_BATCH, jnp.int32)])
    seg_n = jnp.clip(seg_n, 0, cap)
    seg = (seg_n.reshape(-1), jnp.clip(seg_src, 0, 2 * cap + N_EXPERTS * SEG_ALIGN - seg_n).reshape(-1),
           jnp.clip(seg_dst, 0, N_SLOTS - MOE_BLOCK).reshape(-1))

    slots = _dispatch(seg, hn, ls_p, MOE_BLOCK, 0, 0)
    slots = _dispatch(seg, hn, ls_s, DEC_BATCH, nbp, N_PROMPT, prev=slots)
    eo = _experts(tile_expert.astype(jnp.int32), tc, tile_rows, slots, wg, wu, wd, layer)
    yp = _combine(seg, xp, cm, eo, MOE_BLOCK, 0, 0)
    ys = _combine(seg, xs, cm, eo, DEC_BATCH, nbp, N_PROMPT)
    return yp, ys


def kernel(x_prompt, x_sample, state_conv, cache_k, cache_v, norm_mix_g, norm_ffn_g, w_in_ab, conv_w, conv_b,
           conv_ln_g, conv_ln_b, q_norm_g, k_norm_g, attn_sinks, w_out_ab, w_gate_dense, w_up_dense, w_down_dense,
           w_in_c, c_ln_g, c_ln_b, w_spatial, b_spatial, w_out_c, w_router, w_gate_exp, w_up_exp, w_down_exp):
    xp = x_prompt.reshape(N_PROMPT, D_MODEL)
    xs = x_sample.reshape(DEC_BATCH, D_MODEL)
    head = jnp.arange(B_WIDTH) // HEAD_DIM
    seg = (head[:, None] == head[None, :]).astype(BF16)
    conv_p, conv_s, kp_out, vp_out, ks_out, vs_out, vc_out = [], [], [], [], [], [], []
    row = lambda a: a.reshape(1, -1)
    wo_all, wg_all = w_out_ab.astype(BF16), w_gate_dense.astype(BF16)
    wu_all, wd_all = w_up_dense.astype(BF16), w_down_dense.astype(BF16)
    w_in_all, win_c_all, wout_c_all = w_in_ab.astype(BF16), w_in_c.astype(BF16), w_out_c.astype(BF16)
    kc_all = cache_k.reshape(N_AB, DEC_BATCH, WINDOW, KV_WIDTH)
    vc_all = cache_v.reshape(N_AB, DEC_BATCH, WINDOW, KV_WIDTH)
    for layer in range(DEPTH):
        i = layer // 2
        g_mix = row(norm_mix_g[layer])
        g_ffn = row(norm_ffn_g[layer])
        if layer % 2 == 0:
            qg = row(jnp.tile(q_norm_g[i], N_HEADS))
            kg = row(jnp.tile(k_norm_g[i], N_KV_HEADS))
            a_p, q_p, k_p, v_p = _inproj(xp, g_mix, w_in_all, qg, kg, seg, ROW_TILE, i)
            a_s, q_s, k_s, v_s = _inproj(xs, g_mix, w_in_all, qg, kg, seg, DEC_BATCH, i)
            cb, lg, lb = row(conv_b[i]), row(conv_ln_g[i]), row(conv_ln_b[i])
            ca_p = _conv_prompt(a_p, conv_w[i], cb, lg, lb)
            ca_s = _conv_sample(a_s, state_conv, conv_w[i], cb, lg, lb, i)
            ob_p = _attn_prompt(q_p, k_p, v_p, attn_sinks[i])
            sinks_b = jnp.broadcast_to(attn_sinks[i].reshape(N_KV_HEADS, Q_GROUP, 1), (N_KV_HEADS, Q_GROUP, 128))
            ob_s = _attn_sample(q_s.reshape(DEC_BATCH, N_KV_HEADS, Q_GROUP, HEAD_DIM),
                                k_s.reshape(DEC_BATCH, N_KV_HEADS, HEAD_DIM),
                                v_s.reshape(DEC_BATCH, N_KV_HEADS, HEAD_DIM),
                                kc_all, vc_all, sinks_b, i)
            ob_s = ob_s.reshape(DEC_BATCH, B_WIDTH)
            conv_p.append(a_p.reshape(BATCH, SEQ, A_CH)[:, SEQ - (CONV_W - 1):])
            conv_s.append(a_s)
            last = lambda t: (t.reshape(BATCH, SEQ, KV_WIDTH)[:, SEQ - WINDOW:]
                              .reshape(BATCH, WINDOW, N_KV_HEADS, HEAD_DIM))
            kp_out.append(last(k_p))
            vp_out.append(last(v_p))
            ks_out.append(k_s.reshape(DEC_BATCH, 1, N_KV_HEADS, HEAD_DIM))
            vs_out.append(v_s.reshape(DEC_BATCH, 1, N_KV_HEADS, HEAD_DIM))
            xp = _outproj_ffn(xp, ca_p, ob_p, wo_all, g_ffn, wg_all, wu_all, wd_all, ROW_TILE, i)
            xs = _outproj_ffn(xs, ca_s, ob_s, wo_all, g_ffn, wg_all, wu_all, wd_all, DEC_BATCH, i)
        else:
            lg, lb = row(c_ln_g[i]), row(c_ln_b[i])
            xp = _gmlp_prompt(xp, g_mix, win_c_all, lg, lb, w_spatial, b_spatial[i].T, wout_c_all, ROW_TILE, i)
            gd = C_WIDTH // C_GROUPS
            w0 = row(jnp.repeat(w_spatial[i][:, 0, 0], gd))
            b0 = row(jnp.repeat(b_spatial[i][:, 0], gd))
            xs, v_s = _gmlp_sample(xs, g_mix, win_c_all, lg, lb, w0, b0, wout_c_all, i)
            vc_out.append(v_s.reshape(DEC_BATCH, 1, C_WIDTH))
            xp, xs = _moe(xp, xs, norm_ffn_g[layer], w_router[i], w_gate_exp, w_up_exp, w_down_exp, i)
    shifted = lambda old, new: jnp.concatenate([old[:, :, 1:], new], axis=2)
    return (xp.reshape(BATCH, SEQ, D_MODEL), xs.reshape(DEC_BATCH, 1, D_MODEL),
            jnp.stack(conv_p), shifted(state_conv, jnp.stack(conv_s)[:, :, None, :]),
            jnp.stack(kp_out), jnp.stack(vp_out),
            shifted(cache_k, jnp.stack(ks_out)), shifted(cache_v, jnp.stack(vs_out)), jnp.stack(vc_out))
```

```python
import functools

import jax
import jax.numpy as jnp
from jax import lax
from jax.experimental import pallas as pl
from jax.experimental.pallas import tpu as pltpu

F32 = jnp.float32
BF16 = jnp.bfloat16

D_MODEL = 1024
BATCH = 8
SEQ = 2048
DEPTH = 4
N_AB = (DEPTH + 1) // 2
DEC_BATCH = 128
A_CH = 512
CONV_W = 31
N_HEADS = 8
N_KV_HEADS = 2
HEAD_DIM = 64
Q_GROUP = 4
WINDOW = 128
B_WIDTH = 512
KV_WIDTH = 128
AB_IN = 2 * A_CH + B_WIDTH + 2 * KV_WIDTH
C_WIDTH = 1024
C_GROUPS = 8
C_CHUNK = 128
D_FF = 2816
N_EXPERTS = 8
D_FF_EXPERT = 3584
EPS = 1e-6
ATTN_SCALE = HEAD_DIM ** -0.5

N_PROMPT = BATCH * SEQ
N_TOK = N_PROMPT + DEC_BATCH
ROW_TILE = 1024
FF_SPLIT = 11
MOE_TILE = 1024
MOE_FF_TILE = 512
MOE_FF_SUB = 256
MOE_ROW_PATHS = (256, 512, MOE_TILE)
MOE_NF = D_FF_EXPERT // MOE_FF_TILE
MOE_BLOCK = 256
MOE_N_BLOCKS = N_PROMPT // MOE_BLOCK + 1
SEG_ALIGN = 8
MOE_MAX_TILES = (2 * N_TOK + MOE_N_BLOCKS * N_EXPERTS * (SEG_ALIGN - 1) + N_EXPERTS * (MOE_TILE - 1)) // MOE_TILE
N_SLOTS = MOE_MAX_TILES * MOE_TILE
VMEM_LIMIT = 56 * 1024 * 1024


def _params(n_axes, vmem=VMEM_LIMIT):
    return pltpu.CompilerParams(dimension_semantics=("arbitrary",) * n_axes, vmem_limit_bytes=vmem)


def _rms(x, g):
    return x * lax.rsqrt(jnp.mean(x * x, axis=-1, keepdims=True) + EPS) * g


def _layer_norm(x, g, b):
    mu = jnp.mean(x, axis=-1, keepdims=True)
    xc = x - mu
    var = jnp.mean(xc * xc, axis=-1, keepdims=True)
    return xc * lax.rsqrt(var + EPS) * g + b


def _const_spec(shape):
    return pl.BlockSpec(shape, lambda *_: (0,) * len(shape))


def _layer_spec(shape, layer):
    return pl.BlockSpec((None,) + tuple(shape), lambda *_: (layer,) + (0,) * len(shape))


def _inproj_kernel(x_ref, g_ref, w_ref, qg_ref, kg_ref, seg_ref, a_ref, q_ref, k_ref, v_ref):
    h = _rms(x_ref[...], g_ref[...]).astype(BF16)
    z = jnp.dot(h, w_ref[...], preferred_element_type=F32)
    a_ref[...] = z[:, :A_CH] * jax.nn.sigmoid(z[:, A_CH:2 * A_CH])
    seg = seg_ref[...]

    def head_norm(t, gain, seg_t):
        sq = t * t
        hi = sq.astype(BF16)
        lo = (sq - hi.astype(F32)).astype(BF16)
        ms = (jnp.dot(hi, seg_t, preferred_element_type=F32)
              + jnp.dot(lo, seg_t, preferred_element_type=F32)) * (1.0 / HEAD_DIM)
        return t * lax.rsqrt(ms + EPS) * gain

    o = 2 * A_CH
    q = head_norm(z[:, o:o + B_WIDTH], qg_ref[...], seg)
    q_ref[...] = (q * ATTN_SCALE).astype(BF16)
    o += B_WIDTH
    k_ref[...] = head_norm(z[:, o:o + KV_WIDTH], kg_ref[...], seg[:KV_WIDTH, :KV_WIDTH])
    o += KV_WIDTH
    v_ref[...] = z[:, o:o + KV_WIDTH]


def _inproj(x, g, w_bf, qg, kg, seg, tm, layer):
    n = x.shape[0]
    row = lambda w: pl.BlockSpec((tm, w), lambda i: (i, 0))
    return pl.pallas_call(
        _inproj_kernel,
        grid=(n // tm,),
        in_specs=[row(D_MODEL), _const_spec((1, D_MODEL)), _layer_spec((D_MODEL, AB_IN), layer),
                  _const_spec((1, B_WIDTH)), _const_spec((1, KV_WIDTH)), _const_spec((B_WIDTH, B_WIDTH))],
        out_specs=[row(A_CH), row(B_WIDTH), row(KV_WIDTH), row(KV_WIDTH)],
        out_shape=[jax.ShapeDtypeStruct((n, A_CH), F32), jax.ShapeDtypeStruct((n, B_WIDTH), BF16),
                   jax.ShapeDtypeStruct((n, KV_WIDTH), F32), jax.ShapeDtypeStruct((n, KV_WIDTH), F32)],
        compiler_params=_params(1),
        name="inproj_ab",
    )(x, g, w_bf, qg, kg, seg)


CONV_PAD = 32
CONV_ROWS = 128
CONV_SUB = 32


def _conv_tail(c, lg, lb):
    y = _layer_norm(c, lg, lb)
    return y * jax.nn.sigmoid(y)


def _conv_prompt_kernel(a_ref, w_ref, b_ref, lg_ref, lb_ref, o_ref, pad_ref, sh_ref, c_ref):
    pad_ref[0:CONV_PAD, :] = jnp.zeros((CONV_PAD, A_CH), F32)
    pad_ref[CONV_PAD:, :] = a_ref[...]
    win = CONV_ROWS + CONV_PAD

    def body(r, carry):
        t0 = pl.multiple_of(r * CONV_ROWS, CONV_ROWS)
        nc = A_CH // 128
        for c in range(nc):
            sh_ref[0, c] = pad_ref[pl.ds(t0, win), c * 128:(c + 1) * 128]
        for s in range(1, 8):
            for c in range(nc):
                sh_ref[s, c, 0:win - 8, :] = sh_ref[0, c, s:s + win - 8, :]
        first = CONV_PAD - (CONV_W - 1)

        def chunk(c, carry2):
            for r0 in range(0, CONV_ROWS, CONV_SUB):
                acc = jnp.zeros((CONV_SUB, 128), F32)
                for s in range(8):
                    taps = [k for k in range(CONV_W) if (first + k) % 8 == s]
                    span = max(first + k - s for k in taps) + CONV_SUB
                    rows = sh_ref[s, c, r0:r0 + span, :]
                    for k in taps:
                        base = first + k - s
                        acc = acc + rows[base:base + CONV_SUB] * w_ref[c, k:k + 1, :]
                c_ref[c, r0:r0 + CONV_SUB, :] = acc
            return carry2

        lax.fori_loop(0, nc, chunk, 0)
        conv = jnp.concatenate([c_ref[c] for c in range(nc)], axis=-1)
        y = _conv_tail(conv + b_ref[...], lg_ref[...], lb_ref[...])
        o_ref[pl.ds(t0, CONV_ROWS), :] = y.astype(BF16)
        return carry

    lax.fori_loop(0, SEQ // CONV_ROWS, body, 0)


def _conv_prompt(a, w, b, lg, lb):
    nc = A_CH // 128
    w_chunks = w.reshape(CONV_W, nc, 128).transpose(1, 0, 2)
    return pl.pallas_call(
        _conv_prompt_kernel,
        grid=(BATCH,),
        in_specs=[pl.BlockSpec((SEQ, A_CH), lambda i: (i, 0)), _const_spec((nc, CONV_W, 128)),
                  _const_spec((1, A_CH)), _const_spec((1, A_CH)), _const_spec((1, A_CH))],
        out_specs=pl.BlockSpec((SEQ, A_CH), lambda i: (i, 0)),
        out_shape=jax.ShapeDtypeStruct((N_PROMPT, A_CH), BF16),
        scratch_shapes=[pltpu.VMEM((CONV_PAD + SEQ, A_CH), F32),
                        pltpu.VMEM((8, nc, CONV_ROWS + CONV_PAD, 128), F32),
                        pltpu.VMEM((nc, CONV_ROWS, 128), F32)],
        compiler_params=_params(1),
        name="conv_prompt",
    )(a, w_chunks, b, lg, lb)


def _conv_sample_kernel(a_ref, st_ref, w_ref, b_ref, lg_ref, lb_ref, o_ref):
    acc = a_ref[...] * w_ref[CONV_W - 1:CONV_W, :]
    acc = acc + jnp.sum(st_ref[...] * w_ref[0:CONV_W - 1, :][None], axis=1)
    o_ref[...] = _conv_tail(acc + b_ref[...], lg_ref[...], lb_ref[...]).astype(BF16)


def _conv_sample(a, st_t, w, b, lg, lb, layer):
    return pl.pallas_call(
        _conv_sample_kernel,
        grid=(1,),
        in_specs=[_const_spec((DEC_BATCH, A_CH)), _layer_spec((DEC_BATCH, CONV_W - 1, A_CH), layer),
                  _const_spec((CONV_W, A_CH)), _const_spec((1, A_CH)), _const_spec((1, A_CH)),
                  _const_spec((1, A_CH))],
        out_specs=_const_spec((DEC_BATCH, A_CH)),
        out_shape=jax.ShapeDtypeStruct((DEC_BATCH, A_CH), BF16),
        compiler_params=_params(1),
        name="conv_sample",
    )(a, st_t, w, b, lg, lb)


def _alibi_slope(h):
    return 2.0 ** (-8.0 * (h + 1) / N_HEADS)


ATTN_P_WINDOWS = 4


def _attn_prompt_kernel(sink_ref, q_ref, kp_ref, kc_ref, vp_ref, vc_ref, o_ref):
    step = pl.program_id(1)
    kk = jnp.concatenate([kp_ref[...], kc_ref[...]], axis=0).astype(BF16)
    vv = jnp.concatenate([vp_ref[...], vc_ref[...]], axis=0).astype(BF16)
    qi = lax.broadcasted_iota(jnp.int32, (WINDOW, 2 * WINDOW), 0)
    kj = lax.broadcasted_iota(jnp.int32, (WINDOW, 2 * WINDOW), 1)
    dist_i = WINDOW + qi - kj
    in_window = (dist_i >= 0) & (dist_i < WINDOW)
    lo = jnp.where(step > 0, 0, WINDOW)
    dist = dist_i.astype(F32)
    for w in range(ATTN_P_WINDOWS):
        valid = in_window & (kj >= lo) if w == 0 else in_window
        rows = slice(w * WINDOW, (w + 1) * WINDOW)
        keys = slice(w * WINDOW, (w + 2) * WINDOW)
        outs = []
        for h in range(N_HEADS):
            kv = h // Q_GROUP
            lanes = slice(kv * HEAD_DIM, (kv + 1) * HEAD_DIM)
            qh = q_ref[rows, h * HEAD_DIM:(h + 1) * HEAD_DIM]
            s = lax.dot_general(qh, kk[keys, lanes], (((1,), (1,)), ((), ())), preferred_element_type=F32)
            s = jnp.where(valid, s - _alibi_slope(h) * dist, -jnp.inf)
            sink = sink_ref[h]
            m = jnp.maximum(jnp.max(s, axis=-1, keepdims=True), sink)
            p = jnp.exp(s - m)
            denom = jnp.sum(p, axis=-1, keepdims=True) + jnp.exp(sink - m)
            p = (p / denom).astype(BF16)
            outs.append(jnp.dot(p, vv[keys, lanes], preferred_element_type=F32))
        o_ref[rows, :] = jnp.concatenate(outs, axis=-1).astype(BF16)


def _attn_prompt(q, k, v, sinks):
    rows = ATTN_P_WINDOWS * WINDOW
    ns = SEQ // rows
    own = lambda b, j: (b * ns + j, 0)
    prev = lambda b, j: (b * (SEQ // WINDOW) + jnp.maximum(j * ATTN_P_WINDOWS - 1, 0), 0)
    return pl.pallas_call(
        _attn_prompt_kernel,
        grid=(BATCH, ns),
        in_specs=[pl.BlockSpec(memory_space=pltpu.SMEM),
                  pl.BlockSpec((rows, B_WIDTH), own),
                  pl.BlockSpec((WINDOW, KV_WIDTH), prev), pl.BlockSpec((rows, KV_WIDTH), own),
                  pl.BlockSpec((WINDOW, KV_WIDTH), prev), pl.BlockSpec((rows, KV_WIDTH), own)],
        out_specs=pl.BlockSpec((rows, B_WIDTH), own),
        out_shape=jax.ShapeDtypeStruct((N_PROMPT, B_WIDTH), BF16),
        compiler_params=_params(2),
        name="attn_prompt",
    )(sinks, q, k, k, v, v)


ATTN_S_BLK = 32


def _attn_sample_kernel(sink_ref, q_ref, kn_ref, vn_ref, kc_ref, vc_ref, o_ref):
    kc = kc_ref[...].astype(BF16)
    vc = vc_ref[...].astype(BF16)
    j = lax.broadcasted_iota(jnp.int32, (1, 1, WINDOW), 2)
    dist = (WINDOW - j).astype(F32)
    g_idx = lax.broadcasted_iota(jnp.int32, (1, Q_GROUP, 1), 1)
    for kv in range(N_KV_HEADS):
        lanes = slice(kv * HEAD_DIM, (kv + 1) * HEAD_DIM)
        qg = q_ref[:, kv]
        slope = jnp.exp2(-(g_idx + (kv * Q_GROUP + 1)).astype(F32) * (8.0 / N_HEADS))
        s = jnp.einsum('ngd,njd->ngj', qg, kc[:, :, lanes], preferred_element_type=F32)
        s = jnp.where(j >= 1, s - slope * dist, -jnp.inf)
        kn = kn_ref[:, kv].astype(BF16).astype(F32)
        s_new = jnp.sum(qg.astype(F32) * kn[:, None, :], axis=-1, keepdims=True)
        sink = sink_ref[kv][:, :1][None]
        m = jnp.maximum(jnp.maximum(jnp.max(s, axis=-1, keepdims=True), s_new), sink)
        p = jnp.exp(s - m)
        p_new = jnp.exp(s_new - m)
        denom = jnp.sum(p, axis=-1, keepdims=True) + p_new + jnp.exp(sink - m)
        o = jnp.einsum('ngj,njd->ngd', (p / denom).astype(BF16), vc[:, :, lanes], preferred_element_type=F32)
        vn = vn_ref[:, kv].astype(BF16).astype(F32)
        o = o + (p_new / denom).astype(BF16).astype(F32) * vn[:, None, :]
        o_ref[:, kv] = o.astype(BF16)


def _attn_sample(q4, kn3, vn3, kc, vc, sinks_b, layer):
    nb = ATTN_S_BLK
    return pl.pallas_call(
        _attn_sample_kernel,
        grid=(DEC_BATCH // nb,),
        in_specs=[_const_spec((N_KV_HEADS, Q_GROUP, 128)),
                  pl.BlockSpec((nb, N_KV_HEADS, Q_GROUP, HEAD_DIM), lambda i: (i, 0, 0, 0)),
                  pl.BlockSpec((nb, N_KV_HEADS, HEAD_DIM), lambda i: (i, 0, 0)),
                  pl.BlockSpec((nb, N_KV_HEADS, HEAD_DIM), lambda i: (i, 0, 0)),
                  pl.BlockSpec((None, nb, WINDOW, KV_WIDTH), lambda i: (layer, i, 0, 0)),
                  pl.BlockSpec((None, nb, WINDOW, KV_WIDTH), lambda i: (layer, i, 0, 0))],
        out_specs=pl.BlockSpec((nb, N_KV_HEADS, Q_GROUP, HEAD_DIM), lambda i: (i, 0, 0, 0)),
        out_shape=jax.ShapeDtypeStruct((DEC_BATCH, N_KV_HEADS, Q_GROUP, HEAD_DIM), BF16),
        compiler_params=_params(1),
        name="attn_sample",
    )(sinks_b, q4, kn3, vn3, kc, vc)


def _outproj_ffn_kernel(x_ref, ca_ref, ob_ref, g_ref, wo_hbm, wg_hbm, wu_hbm, wd_hbm, o_ref,
                        wo_ref, wg_ref, wu_ref, wd_ref, *, layer):
    @pl.when(pl.program_id(0) == 0)
    def _():
        pltpu.sync_copy(wo_hbm.at[layer], wo_ref)
        pltpu.sync_copy(wg_hbm.at[layer], wg_ref)
        pltpu.sync_copy(wu_hbm.at[layer], wu_ref)
        pltpu.sync_copy(wd_hbm.at[layer], wd_ref)

    o_ref[...] = (x_ref[...]
                  + jnp.dot(ca_ref[...], wo_ref[:A_CH, :], preferred_element_type=F32)
                  + jnp.dot(ob_ref[...], wo_ref[A_CH:, :], preferred_element_type=F32))
    h = _rms(o_ref[...], g_ref[...]).astype(BF16)
    ff = D_FF // FF_SPLIT
    for c in range(FF_SPLIT):
        cols = slice(c * ff, (c + 1) * ff)
        gate = jnp.dot(h, wg_ref[:, cols], preferred_element_type=F32)
        up = jnp.dot(h, wu_ref[:, cols], preferred_element_type=F32)
        act = (gate * jax.nn.sigmoid(gate) * up).astype(BF16)
        o_ref[...] += jnp.dot(act, wd_ref[cols, :], preferred_element_type=F32)


def _outproj_ffn(x, ca, ob, wo, g, wg, wu, wd, tm, layer):
    n = x.shape[0]
    row = lambda w: pl.BlockSpec((tm, w), lambda i: (i, 0))
    hbm = pl.BlockSpec(memory_space=pl.ANY)
    return pl.pallas_call(
        functools.partial(_outproj_ffn_kernel, layer=layer),
        grid=(n // tm,),
        in_specs=[row(D_MODEL), row(A_CH), row(B_WIDTH), _const_spec((1, D_MODEL)), hbm, hbm, hbm, hbm],
        out_specs=row(D_MODEL),
        out_shape=jax.ShapeDtypeStruct((n, D_MODEL), F32),
        scratch_shapes=[pltpu.VMEM((D_MODEL, D_MODEL), BF16), pltpu.VMEM((D_MODEL, D_FF), BF16),
                        pltpu.VMEM((D_MODEL, D_FF), BF16), pltpu.VMEM((D_FF, D_MODEL), BF16)],
        compiler_params=_params(1),
        name="outproj_ffn",
    )(x, ca, ob, g, wo, wg, wu, wd)


def _gmlp_prompt_kernel(x_ref, g_ref, win_ref, lg_ref, lb_ref, ws_ref, bs_ref, wout_ref, o_ref, gate_ref):
    x = x_ref[...]
    h = _rms(x, g_ref[...]).astype(BF16)
    z = jax.nn.gelu(jnp.dot(h, win_ref[...], preferred_element_type=F32))
    u = z[:, :C_WIDTH]
    v = _layer_norm(z[:, C_WIDTH:], lg_ref[...], lb_ref[...]).astype(BF16)
    ri = lax.broadcasted_iota(jnp.int32, (C_CHUNK, C_CHUNK), 0)
    ci = lax.broadcasted_iota(jnp.int32, (C_CHUNK, C_CHUNK), 1)
    tm = x.shape[0]
    gd = C_WIDTH // C_GROUPS
    for grp in range(C_GROUPS):
        w = jnp.where(ri >= ci, ws_ref[grp], 0.0).astype(BF16)
        b = bs_ref[:, grp:grp + 1]
        lanes = slice(grp * gd, (grp + 1) * gd)
        for c in range(tm // C_CHUNK):
            rows = slice(c * C_CHUNK, (c + 1) * C_CHUNK)
            mixed = jnp.dot(w, v[rows, lanes], preferred_element_type=F32) + b
            gate_ref[rows, lanes] = (u[rows, lanes] * mixed).astype(BF16)
    o_ref[...] = x + jnp.dot(gate_ref[...], wout_ref[...], preferred_element_type=F32)


def _gmlp_prompt(x, g, win, lg, lb, ws, bs_t, wout, tm, layer):
    n = x.shape[0]
    row = lambda w: pl.BlockSpec((tm, w), lambda i: (i, 0))
    return pl.pallas_call(
        _gmlp_prompt_kernel,
        grid=(n // tm,),
        in_specs=[row(D_MODEL), _const_spec((1, D_MODEL)), _layer_spec((D_MODEL, 2 * C_WIDTH), layer),
                  _const_spec((1, C_WIDTH)), _const_spec((1, C_WIDTH)),
                  _layer_spec((C_GROUPS, C_CHUNK, C_CHUNK), layer), _const_spec((C_CHUNK, C_GROUPS)),
                  _layer_spec((C_WIDTH, D_MODEL), layer)],
        out_specs=row(D_MODEL),
        out_shape=jax.ShapeDtypeStruct((n, D_MODEL), F32),
        scratch_shapes=[pltpu.VMEM((tm, C_WIDTH), BF16)],
        compiler_params=_params(1),
        name="gmlp_prompt",
    )(x, g, win, lg, lb, ws, bs_t, wout)


def _gmlp_sample_kernel(x_ref, g_ref, win_ref, lg_ref, lb_ref, w0_ref, b0_ref, wout_ref, o_ref, v_ref):
    x = x_ref[...]
    h = _rms(x, g_ref[...]).astype(BF16)
    z = jax.nn.gelu(jnp.dot(h, win_ref[...], preferred_element_type=F32))
    v = _layer_norm(z[:, C_WIDTH:], lg_ref[...], lb_ref[...])
    v_ref[...] = v
    gated = z[:, :C_WIDTH] * (w0_ref[...] * v + b0_ref[...])
    o_ref[...] = x + jnp.dot(gated.astype(BF16), wout_ref[...], preferred_element_type=F32)


def _gmlp_sample(x, g, win, lg, lb, w0, b0, wout, layer):
    n = x.shape[0]
    return pl.pallas_call(
        _gmlp_sample_kernel,
        grid=(1,),
        in_specs=[_const_spec((n, D_MODEL)), _const_spec((1, D_MODEL)), _layer_spec((D_MODEL, 2 * C_WIDTH), layer),
                  _const_spec((1, C_WIDTH)), _const_spec((1, C_WIDTH)), _const_spec((1, C_WIDTH)),
                  _const_spec((1, C_WIDTH)), _layer_spec((C_WIDTH, D_MODEL), layer)],
        out_specs=[_const_spec((n, D_MODEL)), _const_spec((n, C_WIDTH))],
        out_shape=[jax.ShapeDtypeStruct((n, D_MODEL), F32), jax.ShapeDtypeStruct((n, C_WIDTH), F32)],
        compiler_params=_params(1),
        name="gmlp_sample",
    )(x, g, win, lg, lb, w0, b0, wout)


META_W = 8


def _split_bf16(a):
    hi = a.astype(BF16)
    return hi, (a - hi.astype(F32)).astype(BF16)


def _router_kernel(x_ref, g_ref, wr_ref, *rest, bt):
    hn_ref, cm_ref, lst_ref, cnt_ref = rest[-4:]
    h = _rms(x_ref[...], g_ref[...])
    tm = h.shape[0]
    h_hi, h_lo = _split_bf16(h)
    w_hi, w_lo = _split_bf16(wr_ref[...])
    hn_ref[...] = h_hi
    logits = (jnp.dot(h_hi, w_hi, preferred_element_type=F32) + jnp.dot(h_lo, w_hi, preferred_element_type=F32)
              + jnp.dot(h_hi, w_lo, preferred_element_type=F32))
    lane = lax.broadcasted_iota(jnp.int32, (bt, 128), 1).astype(F32)
    neg = -jnp.inf
    ri = lax.broadcasted_iota(jnp.int32, (bt, bt), 0)
    ci = lax.broadcasted_iota(jnp.int32, (bt, bt), 1)
    earlier = jnp.where(ri > ci, 1.0, 0.0).astype(BF16)
    ei = lax.broadcasted_iota(jnp.int32, (128, 128), 0)
    ej = lax.broadcasted_iota(jnp.int32, (128, 128), 1)
    lower_experts = jnp.where(ei < ej, 1.0, 0.0).astype(BF16)
    for j in range(tm // bt):
        rows = slice(j * bt, (j + 1) * bt)
        lg = jnp.where(lane < N_EXPERTS, logits[rows], neg)
        m1 = jnp.max(lg, axis=-1, keepdims=True)
        i1 = jnp.min(jnp.where(lg == m1, lane, 128.0), axis=-1, keepdims=True)
        lg2 = jnp.where(lane == i1, neg, lg)
        m2 = jnp.max(lg2, axis=-1, keepdims=True)
        i2 = jnp.min(jnp.where(lg2 == m2, lane, 128.0), axis=-1, keepdims=True)
        e = jnp.exp(m2 - m1)
        g1 = 1.0 / (1.0 + e)
        g2 = e / (1.0 + e)
        sel1 = lane == i1
        sel2 = lane == i2
        oh = jnp.where(sel1, 1.0, jnp.where(sel2, 1.0, 0.0))
        before = jnp.dot(earlier, oh.astype(BF16), preferred_element_type=F32)
        cnt = jnp.sum(oh, axis=0, keepdims=True)
        padded = jnp.floor((cnt + (SEG_ALIGN - 1)) * (1.0 / SEG_ALIGN)) * SEG_ALIGN
        start = jnp.dot(jnp.broadcast_to(padded, (8, 128)).astype(BF16), lower_experts,
                        preferred_element_type=F32)[0:1]
        slot = before + start
        l1 = jnp.sum(jnp.where(sel1, slot, 0.0), axis=-1, keepdims=True)
        l2 = jnp.sum(jnp.where(sel2, slot, 0.0), axis=-1, keepdims=True)
        rec = jnp.where(lane == 0, l1, jnp.where(lane == 1, l2, jnp.where(lane == 2, g1, jnp.where(lane == 3, g2, 0.0))))
        cm_ref[rows, :] = rec[:, :META_W]
        lst_ref[j] = rec.T[0:2, :].astype(jnp.int32)
        cnt_ref[j] = cnt


def _router(x, g, wr_pad, tm, bt, row_blk0, prev=None):
    n = x.shape[0]
    nblk = tm // bt
    in_specs = [pl.BlockSpec((tm, D_MODEL), lambda i: (i, 0)), _const_spec((1, D_MODEL)),
                _const_spec((D_MODEL, 128))]
    args = [x, g, wr_pad]
    aliases = {}
    if prev is not None:
        in_specs += [pl.BlockSpec(memory_space=pl.ANY), pl.BlockSpec(memory_space=pl.ANY)]
        args += list(prev)
        aliases = {3: 0, 4: 1}
    return pl.pallas_call(
        functools.partial(_router_kernel, bt=bt),
        grid=(n // tm,),
        in_specs=in_specs,
        out_specs=[pl.BlockSpec((tm, D_MODEL), lambda i: (row_blk0 + i, 0)),
                   pl.BlockSpec((tm, META_W), lambda i: (row_blk0 + i, 0)),
                   pl.BlockSpec((nblk, 2, bt), lambda i: (i, 0, 0)),
                   pl.BlockSpec((nblk, 1, 128), lambda i: (i, 0, 0))],
        out_shape=[jax.ShapeDtypeStruct((N_TOK, D_MODEL), BF16), jax.ShapeDtypeStruct((N_TOK, META_W), F32),
                   jax.ShapeDtypeStruct((n // bt, 2, bt), jnp.int32), jax.ShapeDtypeStruct((n // bt, 1, 128), F32)],
        input_output_aliases=aliases,
        compiler_params=_params(1),
        name="moe_router",
    )(*args)


def _local_rows(bt):
    return 2 * bt + N_EXPERTS * SEG_ALIGN


def _segment_copies(seg_n, seg_src, seg_dst, blk, local_ref, slot_ref, sem, bt, to_slots, action):
    def per_expert(e, c):
        s = blk * N_EXPERTS + e
        n, src, dst = seg_n[s], seg_src[s], seg_dst[s]
        off = jnp.int32(0)
        p = bt
        while p >= SEG_ALIGN:
            hit = (n & p) != 0

            @pl.when(hit)
            def _(off=off, p=p):
                a = local_ref.at[pl.ds(pl.multiple_of(src + off, SEG_ALIGN), p)]
                b = slot_ref.at[pl.ds(pl.multiple_of(dst + off, SEG_ALIGN), p)]
                action(pltpu.make_async_copy(a, b, sem) if to_slots else pltpu.make_async_copy(b, a, sem))

            off = off + (n & p)
            p //= 2
        return c

    lax.fori_loop(0, N_EXPERTS, per_expert, 0)


def _dispatch_kernel(seg_n, seg_src, seg_dst, hn_ref, ls_ref, *rest, bt, blk0, nb):
    slots_ref, buf_ref, sem = rest[-3:]
    i = pl.program_id(0)
    par = i % 2
    start = lambda cp: cp.start()
    wait = lambda cp: cp.wait()
    copies = lambda blk, p, action: _segment_copies(seg_n, seg_src, seg_dst, blk, buf_ref.at[p], slots_ref,
                                                     sem.at[p], bt, True, action)

    @pl.when(i >= 2)
    def _():
        copies(blk0 + i - 2, par, wait)

    rows = lax.broadcasted_iota(jnp.int32, (_local_rows(bt), bt), 0)
    ls = ls_ref[0]
    onehot = jnp.where(rows == ls[0:1, :], 1.0, jnp.where(rows == ls[1:2, :], 1.0, 0.0)).astype(BF16)
    buf_ref[par] = jnp.dot(onehot, hn_ref[...], preferred_element_type=F32)
    copies(blk0 + i, par, start)

    @pl.when(i == nb - 1)
    def _():
        if nb >= 2:
            copies(blk0 + i - 1, 1 - par, wait)
        copies(blk0 + i, par, wait)


def _dispatch(seg, hn, ls_t, bt, blk0, tok0, prev=None):
    nb = ls_t.shape[0]
    in_specs = [pl.BlockSpec((bt, D_MODEL), lambda i, *_: (tok0 // bt + i, 0)),
                pl.BlockSpec((1, 2, bt), lambda i, *_: (i, 0, 0))]
    args = [hn, ls_t]
    aliases = {}
    if prev is not None:
        in_specs.append(pl.BlockSpec(memory_space=pl.ANY))
        args.append(prev)
        aliases = {5: 0}
    return pl.pallas_call(
        functools.partial(_dispatch_kernel, bt=bt, blk0=blk0, nb=nb),
        grid_spec=pltpu.PrefetchScalarGridSpec(
            num_scalar_prefetch=3,
            grid=(nb,),
            in_specs=in_specs,
            out_specs=pl.BlockSpec(memory_space=pl.ANY),
            scratch_shapes=[pltpu.VMEM((2, _local_rows(bt), D_MODEL), F32), pltpu.SemaphoreType.DMA((2,))],
        ),
        out_shape=jax.ShapeDtypeStruct((N_SLOTS, D_MODEL), F32),
        input_output_aliases=aliases,
        compiler_params=_params(1),
        name="moe_dispatch",
    )(*seg, *args)


def _experts_kernel(te_ref, tb_ref, nv_ref, x_ref, wg_ref, wu_ref, wd_ref, o_ref, xb_ref):
    t = pl.program_id(0)
    f = pl.program_id(1)
    nv = nv_ref[t]

    @pl.when(jnp.logical_and(nv > 0, f == 0))
    def _():
        rows = lax.broadcasted_iota(jnp.int32, (MOE_TILE, 1), 0)
        xb_ref[...] = jnp.where(rows < nv, x_ref[...], 0.0).astype(BF16)
        o_ref[...] = jnp.zeros_like(o_ref)

    def swiglu_rows(n_rows):
        rows = slice(0, n_rows)
        xb = xb_ref[rows, :]
        for c in range(MOE_FF_TILE // MOE_FF_SUB):
            cols = slice(c * MOE_FF_SUB, (c + 1) * MOE_FF_SUB)
            gate = jnp.dot(xb, wg_ref[:, cols].astype(BF16), preferred_element_type=F32)
            up = jnp.dot(xb, wu_ref[:, cols].astype(BF16), preferred_element_type=F32)
            act = (gate * jax.nn.sigmoid(gate) * up).astype(BF16)
            o_ref[rows, :] += jnp.dot(act, wd_ref[cols, :].astype(BF16), preferred_element_type=F32)

    lo = 0
    for n_rows in MOE_ROW_PATHS:
        @pl.when(jnp.logical_and(nv > lo, nv <= n_rows))
        def _(n_rows=n_rows):
            swiglu_rows(n_rows)
        lo = n_rows


def _experts(tile_expert, tile_blk, tile_rows, xs, wg, wu, wd, layer):
    def chunk(t, f, nv):
        return jnp.where(nv[t] > 0, f, MOE_NF - 1)
    up_spec = pl.BlockSpec((None, None, D_MODEL, MOE_FF_TILE),
                           lambda t, f, te, tb, nv: (layer, te[t], 0, chunk(t, f, nv)))
    down_spec = pl.BlockSpec((None, None, MOE_FF_TILE, D_MODEL),
                             lambda t, f, te, tb, nv: (layer, te[t], chunk(t, f, nv), 0))
    return pl.pallas_call(
        _experts_kernel,
        grid_spec=pltpu.PrefetchScalarGridSpec(
            num_scalar_prefetch=3,
            grid=(MOE_MAX_TILES, MOE_NF),
            in_specs=[pl.BlockSpec((MOE_TILE, D_MODEL), lambda t, f, te, tb, nv: (tb[t], 0)),
                      up_spec, up_spec, down_spec],
            out_specs=pl.BlockSpec((MOE_TILE, D_MODEL), lambda t, f, te, tb, nv: (tb[t], 0)),
            scratch_shapes=[pltpu.VMEM((MOE_TILE, D_MODEL), BF16)],
        ),
        out_shape=jax.ShapeDtypeStruct((N_SLOTS, D_MODEL), F32),
        compiler_params=_params(2),
        name="moe_experts",
    )(tile_expert, tile_blk, tile_rows, xs, wg, wu, wd)


def _combine_kernel(seg_n, seg_src, seg_dst, x_ref, cm_ref, eo_ref, y_ref, buf_ref, sem, *, bt, blk0, nb):
    i = pl.program_id(0)
    par = i % 2
    start = lambda cp: cp.start()
    wait = lambda cp: cp.wait()
    copies = lambda blk, p, action: _segment_copies(seg_n, seg_src, seg_dst, blk, buf_ref.at[p], eo_ref,
                                                     sem.at[p], bt, False, action)

    @pl.when(i == 0)
    def _():
        copies(blk0, 0, start)

    @pl.when(i + 1 < nb)
    def _():
        copies(blk0 + i + 1, 1 - par, start)

    copies(blk0 + i, par, wait)
    last = (blk0 + i) * N_EXPERTS + (N_EXPERTS - 1)
    used = seg_src[last] + seg_n[last]
    rows = lax.broadcasted_iota(jnp.int32, (_local_rows(bt), 1), 0)
    e_hi, e_lo = _split_bf16(jnp.where(rows < used, buf_ref[par], 0.0))
    cm = cm_ref[...]
    cols = lax.broadcasted_iota(jnp.int32, (bt, _local_rows(bt)), 1).astype(F32)
    comb = jnp.where(cols == cm[:, 0:1], cm[:, 2:3], jnp.where(cols == cm[:, 1:2], cm[:, 3:4], 0.0))
    c_hi, c_lo = _split_bf16(comb)
    y_ref[...] = x_ref[...] + jnp.dot(c_hi, e_hi, preferred_element_type=F32)
    y_ref[...] += jnp.dot(c_lo, e_hi, preferred_element_type=F32)
    y_ref[...] += jnp.dot(c_hi, e_lo, preferred_element_type=F32)


def _combine(seg, x, cm, eo, bt, blk0, tok0):
    n = x.shape[0]
    nb = n // bt
    return pl.pallas_call(
        functools.partial(_combine_kernel, bt=bt, blk0=blk0, nb=nb),
        grid_spec=pltpu.PrefetchScalarGridSpec(
            num_scalar_prefetch=3,
            grid=(nb,),
            in_specs=[pl.BlockSpec((bt, D_MODEL), lambda i, *_: (i, 0)),
                      pl.BlockSpec((bt, META_W), lambda i, *_: (tok0 // bt + i, 0)),
                      pl.BlockSpec(memory_space=pl.ANY)],
            out_specs=pl.BlockSpec((bt, D_MODEL), lambda i, *_: (i, 0)),
            scratch_shapes=[pltpu.VMEM((2, _local_rows(bt), D_MODEL), F32), pltpu.SemaphoreType.DMA((2,))],
        ),
        out_shape=jax.ShapeDtypeStruct((n, D_MODEL), F32),
        compiler_params=_params(1),
        name="moe_combine",
    )(*seg, x, cm, eo)


def _moe(xp, xs, g, w_router, wg, wu, wd, layer):
    g2 = g.reshape(1, D_MODEL)
    wr_pad = jnp.pad(w_router, ((0, 0), (0, 128 - N_EXPERTS)))
    hn, cm, ls_p, cnt_p = _router(xp, g2, wr_pad, ROW_TILE, MOE_BLOCK, 0)
    hn, cm, ls_s, cnt_s = _router(xs, g2, wr_pad, DEC_BATCH, DEC_BATCH, N_PROMPT // DEC_BATCH, prev=(hn, cm))

    nbp = N_PROMPT // MOE_BLOCK
    seg_cnt = jnp.concatenate([cnt_p[:, 0, :N_EXPERTS], cnt_s[:, 0, :N_EXPERTS]], axis=0).astype(jnp.int32)
    seg_n = (seg_cnt + (SEG_ALIGN - 1)) // SEG_ALIGN * SEG_ALIGN
    seg_src = jnp.cumsum(seg_n, axis=1) - seg_n
    slot0 = jnp.cumsum(seg_n, axis=0) - seg_n

    counts = seg_n.sum(axis=0)
    tiles_e = (counts + (MOE_TILE - 1)) // MOE_TILE
    tile_end = jnp.cumsum(tiles_e)
    tile_start = tile_end - tiles_e
    n_tiles = tile_end[-1]
    t = jnp.arange(MOE_MAX_TILES, dtype=jnp.int32)
    tc = jnp.minimum(t, n_tiles - 1)
    tile_expert = jnp.sum((tc[:, None] >= tile_end[None, :]).astype(jnp.int32), axis=1)
    rows_left = counts[tile_expert] - (tc - tile_start[tile_expert]) * MOE_TILE
    tile_rows = jnp.where(t < n_tiles, jnp.minimum(rows_left, MOE_TILE), 0).astype(jnp.int32)
    seg_dst = (tile_start * MOE_TILE)[None, :] + slot0

    cap = jnp.concatenate([jnp.full((nbp, 1), MOE_BLOCK, jnp.int32), jnp.full((1, 1), DEC_BATCH, jnp.int32)])
    seg_n = jnp.clip(seg_n, 0, cap)
    seg = (seg_n.reshape(-1), jnp.clip(seg_src, 0, 2 * cap + N_EXPERTS * SEG_ALIGN - seg_n).reshape(-1),
           jnp.clip(seg_dst, 0, N_SLOTS - MOE_BLOCK).reshape(-1))

    slots = _dispatch(seg, hn, ls_p, MOE_BLOCK, 0, 0)
    slots = _dispatch(seg, hn, ls_s, DEC_BATCH, nbp, N_PROMPT, prev=slots)
    eo = _experts(tile_expert.astype(jnp.int32), tc, tile_rows, slots, wg, wu, wd, layer)
    yp = _combine(seg, xp, cm, eo, MOE_BLOCK, 0, 0)
    ys = _combine(seg, xs, cm, eo, DEC_BATCH, nbp, N_PROMPT)
    return yp, ys


def kernel(x_prompt, x_sample, state_conv, cache_k, cache_v, norm_mix_g, norm_ffn_g, w_in_ab, conv_w, conv_b,
           conv_ln_g, conv_ln_b, q_norm_g, k_norm_g, attn_sinks, w_out_ab, w_gate_dense, w_up_dense, w_down_dense,
           w_in_c, c_ln_g, c_ln_b, w_spatial, b_spatial, w_out_c, w_router, w_gate_exp, w_up_exp, w_down_exp):
    xp = x_prompt.reshape(N_PROMPT, D_MODEL)
    xs = x_sample.reshape(DEC_BATCH, D_MODEL)
    head = jnp.arange(B_WIDTH) // HEAD_DIM
    seg = (head[:, None] == head[None, :]).astype(BF16)
    conv_p, conv_s, kp_out, vp_out, ks_out, vs_out, vc_out = [], [], [], [], [], [], []
    row = lambda a: a.reshape(1, -1)
    wo_all, wg_all = w_out_ab.astype(BF16), w_gate_dense.astype(BF16)
    wu_all, wd_all = w_up_dense.astype(BF16), w_down_dense.astype(BF16)
    w_in_all, win_c_all, wout_c_all = w_in_ab.astype(BF16), w_in_c.astype(BF16), w_out_c.astype(BF16)
    kc_all = cache_k.reshape(N_AB, DEC_BATCH, WINDOW, KV_WIDTH)
    vc_all = cache_v.reshape(N_AB, DEC_BATCH, WINDOW, KV_WIDTH)
    for layer in range(DEPTH):
        i = layer // 2
        g_mix = row(norm_mix_g[layer])
        g_ffn = row(norm_ffn_g[layer])
        if layer % 2 == 0:
            qg = row(jnp.tile(q_norm_g[i], N_HEADS))
            kg = row(jnp.tile(k_norm_g[i], N_KV_HEADS))
            a_p, q_p, k_p, v_p = _inproj(xp, g_mix, w_in_all, qg, kg, seg, ROW_TILE, i)
            a_s, q_s, k_s, v_s = _inproj(xs, g_mix, w_in_all, qg, kg, seg, DEC_BATCH, i)
            cb, lg, lb = row(conv_b[i]), row(conv_ln_g[i]), row(conv_ln_b[i])
            ca_p = _conv_prompt(a_p, conv_w[i], cb, lg, lb)
            ca_s = _conv_sample(a_s, state_conv, conv_w[i], cb, lg, lb, i)
            ob_p = _attn_prompt(q_p, k_p, v_p, attn_sinks[i])
            sinks_b = jnp.broadcast_to(attn_sinks[i].reshape(N_KV_HEADS, Q_GROUP, 1), (N_KV_HEADS, Q_GROUP, 128))
            ob_s = _attn_sample(q_s.reshape(DEC_BATCH, N_KV_HEADS, Q_GROUP, HEAD_DIM),
                                k_s.reshape(DEC_BATCH, N_KV_HEADS, HEAD_DIM),
                                v_s.reshape(DEC_BATCH, N_KV_HEADS, HEAD_DIM),
                                kc_all, vc_all, sinks_b, i)
            ob_s = ob_s.reshape(DEC_BATCH, B_WIDTH)
            conv_p.append(a_p.reshape(BATCH, SEQ, A_CH)[:, SEQ - (CONV_W - 1):])
            conv_s.append(a_s)
            last = lambda t: (t.reshape(BATCH, SEQ, KV_WIDTH)[:, SEQ - WINDOW:]
                              .reshape(BATCH, WINDOW, N_KV_HEADS, HEAD_DIM))
            kp_out.append(last(k_p))
            vp_out.append(last(v_p))
            ks_out.append(k_s.reshape(DEC_BATCH, 1, N_KV_HEADS, HEAD_DIM))
            vs_out.append(v_s.reshape(DEC_BATCH, 1, N_KV_HEADS, HEAD_DIM))
            xp = _outproj_ffn(xp, ca_p, ob_p, wo_all, g_ffn, wg_all, wu_all, wd_all, ROW_TILE, i)
            xs = _outproj_ffn(xs, ca_s, ob_s, wo_all, g_ffn, wg_all, wu_all, wd_all, DEC_BATCH, i)
        else:
            lg, lb = row(c_ln_g[i]), row(c_ln_b[i])
            xp = _gmlp_prompt(xp, g_mix, win_c_all, lg, lb, w_spatial, b_spatial[i].T, wout_c_all, ROW_TILE, i)
            gd = C_WIDTH // C_GROUPS
            w0 = row(jnp.repeat(w_spatial[i][:, 0, 0], gd))
            b0 = row(jnp.repeat(b_spatial[i][:, 0], gd))
            xs, v_s = _gmlp_sample(xs, g_mix, win_c_all, lg, lb, w0, b0, wout_c_all, i)
            vc_out.append(v_s.reshape(DEC_BATCH, 1, C_WIDTH))
            xp, xs = _moe(xp, xs, norm_ffn_g[layer], w_router[i], w_gate_exp, w_up_exp, w_down_exp, i)
    def shifted(old, new):
        crop = [(0, 0, 0)] * old.ndim
        crop[2] = (-1, 1, 0)
        moved = lax.pad(old, jnp.zeros((), old.dtype), crop)
        row = lax.broadcasted_iota(jnp.int32, old.shape, 2)
        return jnp.where(row == old.shape[2] - 1, new, moved)
    return (xp.reshape(BATCH, SEQ, D_MODEL), xs.reshape(DEC_BATCH, 1, D_MODEL),
            jnp.stack(conv_p), shifted(state_conv, jnp.stack(conv_s)[:, :, None, :]),
            jnp.stack(kp_out), jnp.stack(vp_out),
            shifted(cache_k, jnp.stack(ks_out)), shifted(cache_v, jnp.stack(vs_out)), jnp.stack(vc_out))
```

```python
import functools

import jax
import jax.numpy as jnp
from jax import lax
from jax.experimental import pallas as pl
from jax.experimental.pallas import tpu as pltpu

F32 = jnp.float32
BF16 = jnp.bfloat16

D_MODEL = 1024
BATCH = 8
SEQ = 2048
DEPTH = 4
N_AB = (DEPTH + 1) // 2
DEC_BATCH = 128
A_CH = 512
CONV_W = 31
N_HEADS = 8
N_KV_HEADS = 2
HEAD_DIM = 64
Q_GROUP = 4
WINDOW = 128
B_WIDTH = 512
KV_WIDTH = 128
AB_IN = 2 * A_CH + B_WIDTH + 2 * KV_WIDTH
C_WIDTH = 1024
C_GROUPS = 8
C_CHUNK = 128
D_FF = 2816
N_EXPERTS = 8
D_FF_EXPERT = 3584
EPS = 1e-6
ATTN_SCALE = HEAD_DIM ** -0.5

N_PROMPT = BATCH * SEQ
N_TOK = N_PROMPT + DEC_BATCH
ROW_TILE = 1024
FF_SPLIT = 11
MOE_TILE = 1024
MOE_FF_TILE = 512
MOE_FF_SUB = 256
MOE_ROW_PATHS = (256, 512, MOE_TILE)
MOE_NF = D_FF_EXPERT // MOE_FF_TILE
MOE_BLOCK = 256
MOE_N_BLOCKS = N_PROMPT // MOE_BLOCK + 1
SEG_ALIGN = 8
MOE_MAX_TILES = (2 * N_TOK + MOE_N_BLOCKS * N_EXPERTS * (SEG_ALIGN - 1) + N_EXPERTS * (MOE_TILE - 1)) // MOE_TILE
N_SLOTS = MOE_MAX_TILES * MOE_TILE
VMEM_LIMIT = 56 * 1024 * 1024


def _params(n_axes, vmem=VMEM_LIMIT):
    return pltpu.CompilerParams(dimension_semantics=("arbitrary",) * n_axes, vmem_limit_bytes=vmem)


def _rms(x, g):
    return x * lax.rsqrt(jnp.mean(x * x, axis=-1, keepdims=True) + EPS) * g


def _layer_norm(x, g, b):
    mu = jnp.mean(x, axis=-1, keepdims=True)
    xc = x - mu
    var = jnp.mean(xc * xc, axis=-1, keepdims=True)
    return xc * lax.rsqrt(var + EPS) * g + b


def _const_spec(shape):
    return pl.BlockSpec(shape, lambda *_: (0,) * len(shape))


def _layer_spec(shape, layer):
    return pl.BlockSpec((None,) + tuple(shape), lambda *_: (layer,) + (0,) * len(shape))


def _inproj_kernel(x_ref, g_ref, w_ref, qg_ref, kg_ref, seg_ref, a_ref, q_ref, k_ref, v_ref):
    h = _rms(x_ref[...], g_ref[...]).astype(BF16)
    z = jnp.dot(h, w_ref[...], preferred_element_type=F32)
    a_ref[...] = z[:, :A_CH] * jax.nn.sigmoid(z[:, A_CH:2 * A_CH])
    seg = seg_ref[...]

    def head_norm(t, gain, seg_t):
        sq = t * t
        hi = sq.astype(BF16)
        lo = (sq - hi.astype(F32)).astype(BF16)
        ms = (jnp.dot(hi, seg_t, preferred_element_type=F32)
              + jnp.dot(lo, seg_t, preferred_element_type=F32)) * (1.0 / HEAD_DIM)
        return t * lax.rsqrt(ms + EPS) * gain

    o = 2 * A_CH
    q = head_norm(z[:, o:o + B_WIDTH], qg_ref[...], seg)
    q_ref[...] = (q * ATTN_SCALE).astype(BF16)
    o += B_WIDTH
    k_ref[...] = head_norm(z[:, o:o + KV_WIDTH], kg_ref[...], seg[:KV_WIDTH, :KV_WIDTH])
    o += KV_WIDTH
    v_ref[...] = z[:, o:o + KV_WIDTH]


def _inproj(x, g, w_bf, qg, kg, seg, tm, layer):
    n = x.shape[0]
    row = lambda w: pl.BlockSpec((tm, w), lambda i: (i, 0))
    return pl.pallas_call(
        _inproj_kernel,
        grid=(n // tm,),
        in_specs=[row(D_MODEL), _const_spec((1, D_MODEL)), _layer_spec((D_MODEL, AB_IN), layer),
                  _const_spec((1, B_WIDTH)), _const_spec((1, KV_WIDTH)), _const_spec((B_WIDTH, B_WIDTH))],
        out_specs=[row(A_CH), row(B_WIDTH), row(KV_WIDTH), row(KV_WIDTH)],
        out_shape=[jax.ShapeDtypeStruct((n, A_CH), F32), jax.ShapeDtypeStruct((n, B_WIDTH), BF16),
                   jax.ShapeDtypeStruct((n, KV_WIDTH), F32), jax.ShapeDtypeStruct((n, KV_WIDTH), F32)],
        compiler_params=_params(1),
        name="inproj_ab",
    )(x, g, w_bf, qg, kg, seg)


CONV_PAD = 32
CONV_ROWS = 128
CONV_SUB = 32


def _conv_tail(c, lg, lb):
    y = _layer_norm(c, lg, lb)
    return y * jax.nn.sigmoid(y)


def _conv_prompt_kernel(a_ref, w_ref, b_ref, lg_ref, lb_ref, o_ref, pad_ref, sh_ref, c_ref):
    pad_ref[0:CONV_PAD, :] = jnp.zeros((CONV_PAD, A_CH), F32)
    pad_ref[CONV_PAD:, :] = a_ref[...]
    win = CONV_ROWS + CONV_PAD

    def body(r, carry):
        t0 = pl.multiple_of(r * CONV_ROWS, CONV_ROWS)
        nc = A_CH // 128
        for c in range(nc):
            sh_ref[0, c] = pad_ref[pl.ds(t0, win), c * 128:(c + 1) * 128]
        for s in range(1, 8):
            for c in range(nc):
                sh_ref[s, c, 0:win - 8, :] = sh_ref[0, c, s:s + win - 8, :]
        first = CONV_PAD - (CONV_W - 1)

        def chunk(c, carry2):
            for r0 in range(0, CONV_ROWS, CONV_SUB):
                acc = jnp.zeros((CONV_SUB, 128), F32)
                for s in range(8):
                    taps = [k for k in range(CONV_W) if (first + k) % 8 == s]
                    span = max(first + k - s for k in taps) + CONV_SUB
                    rows = sh_ref[s, c, r0:r0 + span, :]
                    for k in taps:
                        base = first + k - s
                        acc = acc + rows[base:base + CONV_SUB] * w_ref[c, k:k + 1, :]
                c_ref[c, r0:r0 + CONV_SUB, :] = acc
            return carry2

        lax.fori_loop(0, nc, chunk, 0)
        conv = jnp.concatenate([c_ref[c] for c in range(nc)], axis=-1)
        y = _conv_tail(conv + b_ref[...], lg_ref[...], lb_ref[...])
        o_ref[pl.ds(t0, CONV_ROWS), :] = y.astype(BF16)
        return carry

    lax.fori_loop(0, SEQ // CONV_ROWS, body, 0)


def _conv_prompt(a, w, b, lg, lb):
    nc = A_CH // 128
    w_chunks = w.reshape(CONV_W, nc, 128).transpose(1, 0, 2)
    return pl.pallas_call(
        _conv_prompt_kernel,
        grid=(BATCH,),
        in_specs=[pl.BlockSpec((SEQ, A_CH), lambda i: (i, 0)), _const_spec((nc, CONV_W, 128)),
                  _const_spec((1, A_CH)), _const_spec((1, A_CH)), _const_spec((1, A_CH))],
        out_specs=pl.BlockSpec((SEQ, A_CH), lambda i: (i, 0)),
        out_shape=jax.ShapeDtypeStruct((N_PROMPT, A_CH), BF16),
        scratch_shapes=[pltpu.VMEM((CONV_PAD + SEQ, A_CH), F32),
                        pltpu.VMEM((8, nc, CONV_ROWS + CONV_PAD, 128), F32),
                        pltpu.VMEM((nc, CONV_ROWS, 128), F32)],
        compiler_params=_params(1),
        name="conv_prompt",
    )(a, w_chunks, b, lg, lb)


def _conv_sample_kernel(a_ref, st_ref, w_ref, b_ref, lg_ref, lb_ref, o_ref):
    acc = a_ref[...] * w_ref[CONV_W - 1:CONV_W, :]
    acc = acc + jnp.sum(st_ref[...] * w_ref[0:CONV_W - 1, :][None], axis=1)
    o_ref[...] = _conv_tail(acc + b_ref[...], lg_ref[...], lb_ref[...]).astype(BF16)


def _conv_sample(a, st_t, w, b, lg, lb, layer):
    return pl.pallas_call(
        _conv_sample_kernel,
        grid=(1,),
        in_specs=[_const_spec((DEC_BATCH, A_CH)), _layer_spec((DEC_BATCH, CONV_W - 1, A_CH), layer),
                  _const_spec((CONV_W, A_CH)), _const_spec((1, A_CH)), _const_spec((1, A_CH)),
                  _const_spec((1, A_CH))],
        out_specs=_const_spec((DEC_BATCH, A_CH)),
        out_shape=jax.ShapeDtypeStruct((DEC_BATCH, A_CH), BF16),
        compiler_params=_params(1),
        name="conv_sample",
    )(a, st_t, w, b, lg, lb)


def _alibi_slope(h):
    return 2.0 ** (-8.0 * (h + 1) / N_HEADS)


ATTN_P_WINDOWS = 4


def _attn_prompt_kernel(sink_ref, q_ref, kp_ref, kc_ref, vp_ref, vc_ref, o_ref):
    step = pl.program_id(1)
    kk = jnp.concatenate([kp_ref[...], kc_ref[...]], axis=0).astype(BF16)
    vv = jnp.concatenate([vp_ref[...], vc_ref[...]], axis=0).astype(BF16)
    qi = lax.broadcasted_iota(jnp.int32, (WINDOW, 2 * WINDOW), 0)
    kj = lax.broadcasted_iota(jnp.int32, (WINDOW, 2 * WINDOW), 1)
    dist_i = WINDOW + qi - kj
    in_window = (dist_i >= 0) & (dist_i < WINDOW)
    lo = jnp.where(step > 0, 0, WINDOW)
    dist = dist_i.astype(F32)
    for w in range(ATTN_P_WINDOWS):
        valid = in_window & (kj >= lo) if w == 0 else in_window
        rows = slice(w * WINDOW, (w + 1) * WINDOW)
        keys = slice(w * WINDOW, (w + 2) * WINDOW)
        outs = []
        for h in range(N_HEADS):
            kv = h // Q_GROUP
            lanes = slice(kv * HEAD_DIM, (kv + 1) * HEAD_DIM)
            qh = q_ref[rows, h * HEAD_DIM:(h + 1) * HEAD_DIM]
            s = lax.dot_general(qh, kk[keys, lanes], (((1,), (1,)), ((), ())), preferred_element_type=F32)
            s = jnp.where(valid, s - _alibi_slope(h) * dist, -jnp.inf)
            sink = sink_ref[h]
            m = jnp.maximum(jnp.max(s, axis=-1, keepdims=True), sink)
            p = jnp.exp(s - m)
            denom = jnp.sum(p, axis=-1, keepdims=True) + jnp.exp(sink - m)
            p = (p / denom).astype(BF16)
            outs.append(jnp.dot(p, vv[keys, lanes], preferred_element_type=F32))
        o_ref[rows, :] = jnp.concatenate(outs, axis=-1).astype(BF16)


def _attn_prompt(q, k, v, sinks):
    rows = ATTN_P_WINDOWS * WINDOW
    ns = SEQ // rows
    own = lambda b, j: (b * ns + j, 0)
    prev = lambda b, j: (b * (SEQ // WINDOW) + jnp.maximum(j * ATTN_P_WINDOWS - 1, 0), 0)
    return pl.pallas_call(
        _attn_prompt_kernel,
        grid=(BATCH, ns),
        in_specs=[pl.BlockSpec(memory_space=pltpu.SMEM),
                  pl.BlockSpec((rows, B_WIDTH), own),
                  pl.BlockSpec((WINDOW, KV_WIDTH), prev), pl.BlockSpec((rows, KV_WIDTH), own),
                  pl.BlockSpec((WINDOW, KV_WIDTH), prev), pl.BlockSpec((rows, KV_WIDTH), own)],
        out_specs=pl.BlockSpec((rows, B_WIDTH), own),
        out_shape=jax.ShapeDtypeStruct((N_PROMPT, B_WIDTH), BF16),
        compiler_params=_params(2),
        name="attn_prompt",
    )(sinks, q, k, k, v, v)


ATTN_S_BLK = 32


def _attn_sample_kernel(sink_ref, q_ref, kn_ref, vn_ref, kc_ref, vc_ref, o_ref):
    kc = kc_ref[...].astype(BF16)
    vc = vc_ref[...].astype(BF16)
    j = lax.broadcasted_iota(jnp.int32, (1, 1, WINDOW), 2)
    dist = (WINDOW - j).astype(F32)
    g_idx = lax.broadcasted_iota(jnp.int32, (1, Q_GROUP, 1), 1)
    for kv in range(N_KV_HEADS):
        lanes = slice(kv * HEAD_DIM, (kv + 1) * HEAD_DIM)
        qg = q_ref[:, kv]
        slope = jnp.exp2(-(g_idx + (kv * Q_GROUP + 1)).astype(F32) * (8.0 / N_HEADS))
        s = jnp.einsum('ngd,njd->ngj', qg, kc[:, :, lanes], preferred_element_type=F32)
        s = jnp.where(j >= 1, s - slope * dist, -jnp.inf)
        kn = kn_ref[:, kv].astype(BF16).astype(F32)
        s_new = jnp.sum(qg.astype(F32) * kn[:, None, :], axis=-1, keepdims=True)
        sink = sink_ref[kv][:, :1][None]
        m = jnp.maximum(jnp.maximum(jnp.max(s, axis=-1, keepdims=True), s_new), sink)
        p = jnp.exp(s - m)
        p_new = jnp.exp(s_new - m)
        denom = jnp.sum(p, axis=-1, keepdims=True) + p_new + jnp.exp(sink - m)
        o = jnp.einsum('ngj,njd->ngd', (p / denom).astype(BF16), vc[:, :, lanes], preferred_element_type=F32)
        vn = vn_ref[:, kv].astype(BF16).astype(F32)
        o = o + (p_new / denom).astype(BF16).astype(F32) * vn[:, None, :]
        o_ref[:, kv] = o.astype(BF16)


def _attn_sample(q4, kn3, vn3, kc, vc, sinks_b, layer):
    nb = ATTN_S_BLK
    return pl.pallas_call(
        _attn_sample_kernel,
        grid=(DEC_BATCH // nb,),
        in_specs=[_const_spec((N_KV_HEADS, Q_GROUP, 128)),
                  pl.BlockSpec((nb, N_KV_HEADS, Q_GROUP, HEAD_DIM), lambda i: (i, 0, 0, 0)),
                  pl.BlockSpec((nb, N_KV_HEADS, HEAD_DIM), lambda i: (i, 0, 0)),
                  pl.BlockSpec((nb, N_KV_HEADS, HEAD_DIM), lambda i: (i, 0, 0)),
                  pl.BlockSpec((None, nb, WINDOW, KV_WIDTH), lambda i: (layer, i, 0, 0)),
                  pl.BlockSpec((None, nb, WINDOW, KV_WIDTH), lambda i: (layer, i, 0, 0))],
        out_specs=pl.BlockSpec((nb, N_KV_HEADS, Q_GROUP, HEAD_DIM), lambda i: (i, 0, 0, 0)),
        out_shape=jax.ShapeDtypeStruct((DEC_BATCH, N_KV_HEADS, Q_GROUP, HEAD_DIM), BF16),
        compiler_params=_params(1),
        name="attn_sample",
    )(sinks_b, q4, kn3, vn3, kc, vc)


def _outproj_ffn_kernel(x_ref, ca_ref, ob_ref, g_ref, wo_hbm, wg_hbm, wu_hbm, wd_hbm, o_ref,
                        wo_ref, wg_ref, wu_ref, wd_ref, *, layer):
    @pl.when(pl.program_id(0) == 0)
    def _():
        pltpu.sync_copy(wo_hbm.at[layer], wo_ref)
        pltpu.sync_copy(wg_hbm.at[layer], wg_ref)
        pltpu.sync_copy(wu_hbm.at[layer], wu_ref)
        pltpu.sync_copy(wd_hbm.at[layer], wd_ref)

    o_ref[...] = (x_ref[...]
                  + jnp.dot(ca_ref[...], wo_ref[:A_CH, :], preferred_element_type=F32)
                  + jnp.dot(ob_ref[...], wo_ref[A_CH:, :], preferred_element_type=F32))
    h = _rms(o_ref[...], g_ref[...]).astype(BF16)
    ff = D_FF // FF_SPLIT
    for c in range(FF_SPLIT):
        cols = slice(c * ff, (c + 1) * ff)
        gate = jnp.dot(h, wg_ref[:, cols], preferred_element_type=F32)
        up = jnp.dot(h, wu_ref[:, cols], preferred_element_type=F32)
        act = (gate * jax.nn.sigmoid(gate) * up).astype(BF16)
        o_ref[...] += jnp.dot(act, wd_ref[cols, :], preferred_element_type=F32)


def _outproj_ffn(x, ca, ob, wo, g, wg, wu, wd, tm, layer):
    n = x.shape[0]
    row = lambda w: pl.BlockSpec((tm, w), lambda i: (i, 0))
    hbm = pl.BlockSpec(memory_space=pl.ANY)
    return pl.pallas_call(
        functools.partial(_outproj_ffn_kernel, layer=layer),
        grid=(n // tm,),
        in_specs=[row(D_MODEL), row(A_CH), row(B_WIDTH), _const_spec((1, D_MODEL)), hbm, hbm, hbm, hbm],
        out_specs=row(D_MODEL),
        out_shape=jax.ShapeDtypeStruct((n, D_MODEL), F32),
        scratch_shapes=[pltpu.VMEM((D_MODEL, D_MODEL), BF16), pltpu.VMEM((D_MODEL, D_FF), BF16),
                        pltpu.VMEM((D_MODEL, D_FF), BF16), pltpu.VMEM((D_FF, D_MODEL), BF16)],
        compiler_params=_params(1),
        name="outproj_ffn",
    )(x, ca, ob, g, wo, wg, wu, wd)


def _gmlp_prompt_kernel(x_ref, g_ref, win_ref, lg_ref, lb_ref, ws_ref, bs_ref, wout_ref, o_ref, gate_ref):
    x = x_ref[...]
    h = _rms(x, g_ref[...]).astype(BF16)
    z = jax.nn.gelu(jnp.dot(h, win_ref[...], preferred_element_type=F32))
    u = z[:, :C_WIDTH]
    v = _layer_norm(z[:, C_WIDTH:], lg_ref[...], lb_ref[...]).astype(BF16)
    ri = lax.broadcasted_iota(jnp.int32, (C_CHUNK, C_CHUNK), 0)
    ci = lax.broadcasted_iota(jnp.int32, (C_CHUNK, C_CHUNK), 1)
    tm = x.shape[0]
    gd = C_WIDTH // C_GROUPS
    for grp in range(C_GROUPS):
        w = jnp.where(ri >= ci, ws_ref[grp], 0.0).astype(BF16)
        b = bs_ref[:, grp:grp + 1]
        lanes = slice(grp * gd, (grp + 1) * gd)
        for c in range(tm // C_CHUNK):
            rows = slice(c * C_CHUNK, (c + 1) * C_CHUNK)
            mixed = jnp.dot(w, v[rows, lanes], preferred_element_type=F32) + b
            gate_ref[rows, lanes] = (u[rows, lanes] * mixed).astype(BF16)
    o_ref[...] = x + jnp.dot(gate_ref[...], wout_ref[...], preferred_element_type=F32)


def _gmlp_prompt(x, g, win, lg, lb, ws, bs_t, wout, tm, layer):
    n = x.shape[0]
    row = lambda w: pl.BlockSpec((tm, w), lambda i: (i, 0))
    return pl.pallas_call(
        _gmlp_prompt_kernel,
        grid=(n // tm,),
        in_specs=[row(D_MODEL), _const_spec((1, D_MODEL)), _layer_spec((D_MODEL, 2 * C_WIDTH), layer),
                  _const_spec((1, C_WIDTH)), _const_spec((1, C_WIDTH)),
                  _layer_spec((C_GROUPS, C_CHUNK, C_CHUNK), layer), _const_spec((C_CHUNK, C_GROUPS)),
                  _layer_spec((C_WIDTH, D_MODEL), layer)],
        out_specs=row(D_MODEL),
        out_shape=jax.ShapeDtypeStruct((n, D_MODEL), F32),
        scratch_shapes=[pltpu.VMEM((tm, C_WIDTH), BF16)],
        compiler_params=_params(1),
        name="gmlp_prompt",
    )(x, g, win, lg, lb, ws, bs_t, wout)


def _gmlp_sample_kernel(x_ref, g_ref, win_ref, lg_ref, lb_ref, w0_ref, b0_ref, wout_ref, o_ref, v_ref):
    x = x_ref[...]
    h = _rms(x, g_ref[...]).astype(BF16)
    z = jax.nn.gelu(jnp.dot(h, win_ref[...], preferred_element_type=F32))
    v = _layer_norm(z[:, C_WIDTH:], lg_ref[...], lb_ref[...])
    v_ref[...] = v
    gated = z[:, :C_WIDTH] * (w0_ref[...] * v + b0_ref[...])
    o_ref[...] = x + jnp.dot(gated.astype(BF16), wout_ref[...], preferred_element_type=F32)


def _gmlp_sample(x, g, win, lg, lb, w0, b0, wout, layer):
    n = x.shape[0]
    return pl.pallas_call(
        _gmlp_sample_kernel,
        grid=(1,),
        in_specs=[_const_spec((n, D_MODEL)), _const_spec((1, D_MODEL)), _layer_spec((D_MODEL, 2 * C_WIDTH), layer),
                  _const_spec((1, C_WIDTH)), _const_spec((1, C_WIDTH)), _const_spec((1, C_WIDTH)),
                  _const_spec((1, C_WIDTH)), _layer_spec((C_WIDTH, D_MODEL), layer)],
        out_specs=[_const_spec((n, D_MODEL)), _const_spec((n, C_WIDTH))],
        out_shape=[jax.ShapeDtypeStruct((n, D_MODEL), F32), jax.ShapeDtypeStruct((n, C_WIDTH), F32)],
        compiler_params=_params(1),
        name="gmlp_sample",
    )(x, g, win, lg, lb, w0, b0, wout)


META_W = 8


def _split_bf16(a):
    hi = a.astype(BF16)
    return hi, (a - hi.astype(F32)).astype(BF16)


def _router_kernel(x_ref, g_ref, wr_ref, *rest, bt):
    hn_ref, cm_ref, lst_ref, cnt_ref = rest[-4:]
    h = _rms(x_ref[...], g_ref[...])
    tm = h.shape[0]
    h_hi, h_lo = _split_bf16(h)
    w_hi, w_lo = _split_bf16(wr_ref[...])
    hn_ref[...] = h_hi
    both = jnp.dot(h_hi, jnp.concatenate([w_hi, w_lo], axis=1), preferred_element_type=F32)
    logits = both[:, :128] + jnp.dot(h_lo, w_hi, preferred_element_type=F32) + both[:, 128:]
    lane = lax.broadcasted_iota(jnp.int32, (bt, 128), 1).astype(F32)
    neg = -jnp.inf
    ri = lax.broadcasted_iota(jnp.int32, (bt, bt), 0)
    ci = lax.broadcasted_iota(jnp.int32, (bt, bt), 1)
    earlier = jnp.where(ri > ci, 1.0, 0.0).astype(BF16)
    ei = lax.broadcasted_iota(jnp.int32, (128, 128), 0)
    ej = lax.broadcasted_iota(jnp.int32, (128, 128), 1)
    lower_experts = jnp.where(ei < ej, 1.0, 0.0).astype(BF16)
    for j in range(tm // bt):
        rows = slice(j * bt, (j + 1) * bt)
        lg = jnp.where(lane < N_EXPERTS, logits[rows], neg)
        m1 = jnp.max(lg, axis=-1, keepdims=True)
        i1 = jnp.min(jnp.where(lg == m1, lane, 128.0), axis=-1, keepdims=True)
        lg2 = jnp.where(lane == i1, neg, lg)
        m2 = jnp.max(lg2, axis=-1, keepdims=True)
        i2 = jnp.min(jnp.where(lg2 == m2, lane, 128.0), axis=-1, keepdims=True)
        e = jnp.exp(m2 - m1)
        g1 = 1.0 / (1.0 + e)
        g2 = e / (1.0 + e)
        sel1 = lane == i1
        sel2 = lane == i2
        oh = jnp.where(sel1, 1.0, jnp.where(sel2, 1.0, 0.0))
        before = jnp.dot(earlier, oh.astype(BF16), preferred_element_type=F32)
        cnt = jnp.sum(oh, axis=0, keepdims=True)
        padded = jnp.floor((cnt + (SEG_ALIGN - 1)) * (1.0 / SEG_ALIGN)) * SEG_ALIGN
        start = jnp.dot(jnp.broadcast_to(padded, (8, 128)).astype(BF16), lower_experts,
                        preferred_element_type=F32)[0:1]
        slot = before + start
        l1 = jnp.sum(jnp.where(sel1, slot, 0.0), axis=-1, keepdims=True)
        l2 = jnp.sum(jnp.where(sel2, slot, 0.0), axis=-1, keepdims=True)
        rec = jnp.where(lane == 0, l1, jnp.where(lane == 1, l2, jnp.where(lane == 2, g1, jnp.where(lane == 3, g2, 0.0))))
        cm_ref[rows, :] = rec[:, :META_W]
        lst_ref[j] = rec.T[0:2, :].astype(jnp.int32)
        cnt_ref[j] = cnt


def _router(x, g, wr_pad, tm, bt, row_blk0, prev=None):
    n = x.shape[0]
    nblk = tm // bt
    in_specs = [pl.BlockSpec((tm, D_MODEL), lambda i: (i, 0)), _const_spec((1, D_MODEL)),
                _const_spec((D_MODEL, 128))]
    args = [x, g, wr_pad]
    aliases = {}
    if prev is not None:
        in_specs += [pl.BlockSpec(memory_space=pl.ANY), pl.BlockSpec(memory_space=pl.ANY)]
        args += list(prev)
        aliases = {3: 0, 4: 1}
    return pl.pallas_call(
        functools.partial(_router_kernel, bt=bt),
        grid=(n // tm,),
        in_specs=in_specs,
        out_specs=[pl.BlockSpec((tm, D_MODEL), lambda i: (row_blk0 + i, 0)),
                   pl.BlockSpec((tm, META_W), lambda i: (row_blk0 + i, 0)),
                   pl.BlockSpec((nblk, 2, bt), lambda i: (i, 0, 0)),
                   pl.BlockSpec((nblk, 1, 128), lambda i: (i, 0, 0))],
        out_shape=[jax.ShapeDtypeStruct((N_TOK, D_MODEL), BF16), jax.ShapeDtypeStruct((N_TOK, META_W), F32),
                   jax.ShapeDtypeStruct((n // bt, 2, bt), jnp.int32), jax.ShapeDtypeStruct((n // bt, 1, 128), F32)],
        input_output_aliases=aliases,
        compiler_params=_params(1),
        name="moe_router",
    )(*args)


def _local_rows(bt):
    return 2 * bt + N_EXPERTS * SEG_ALIGN


def _segment_copies(seg_n, seg_src, seg_dst, blk, local_ref, slot_ref, sem, bt, to_slots, action):
    def per_expert(e, c):
        s = blk * N_EXPERTS + e
        n, src, dst = seg_n[s], seg_src[s], seg_dst[s]
        off = jnp.int32(0)
        p = bt
        while p >= SEG_ALIGN:
            hit = (n & p) != 0

            @pl.when(hit)
            def _(off=off, p=p):
                a = local_ref.at[pl.ds(pl.multiple_of(src + off, SEG_ALIGN), p)]
                b = slot_ref.at[pl.ds(pl.multiple_of(dst + off, SEG_ALIGN), p)]
                action(pltpu.make_async_copy(a, b, sem) if to_slots else pltpu.make_async_copy(b, a, sem))

            off = off + (n & p)
            p //= 2
        return c

    lax.fori_loop(0, N_EXPERTS, per_expert, 0)


def _dispatch_kernel(seg_n, seg_src, seg_dst, hn_ref, ls_ref, *rest, bt, blk0, nb):
    slots_ref, buf_ref, sem = rest[-3:]
    i = pl.program_id(0)
    par = i % 2
    start = lambda cp: cp.start()
    wait = lambda cp: cp.wait()
    copies = lambda blk, p, action: _segment_copies(seg_n, seg_src, seg_dst, blk, buf_ref.at[p], slots_ref,
                                                     sem.at[p], bt, True, action)

    @pl.when(i >= 2)
    def _():
        copies(blk0 + i - 2, par, wait)

    rows = lax.broadcasted_iota(jnp.int32, (_local_rows(bt), bt), 0)
    ls = ls_ref[0]
    onehot = jnp.where(rows == ls[0:1, :], 1.0, jnp.where(rows == ls[1:2, :], 1.0, 0.0)).astype(BF16)
    buf_ref[par] = jnp.dot(onehot, hn_ref[...], preferred_element_type=F32)
    copies(blk0 + i, par, start)

    @pl.when(i == nb - 1)
    def _():
        if nb >= 2:
            copies(blk0 + i - 1, 1 - par, wait)
        copies(blk0 + i, par, wait)


def _dispatch(seg, hn, ls_t, bt, blk0, tok0, prev=None):
    nb = ls_t.shape[0]
    in_specs = [pl.BlockSpec((bt, D_MODEL), lambda i, *_: (tok0 // bt + i, 0)),
                pl.BlockSpec((1, 2, bt), lambda i, *_: (i, 0, 0))]
    args = [hn, ls_t]
    aliases = {}
    if prev is not None:
        in_specs.append(pl.BlockSpec(memory_space=pl.ANY))
        args.append(prev)
        aliases = {5: 0}
    return pl.pallas_call(
        functools.partial(_dispatch_kernel, bt=bt, blk0=blk0, nb=nb),
        grid_spec=pltpu.PrefetchScalarGridSpec(
            num_scalar_prefetch=3,
            grid=(nb,),
            in_specs=in_specs,
            out_specs=pl.BlockSpec(memory_space=pl.ANY),
            scratch_shapes=[pltpu.VMEM((2, _local_rows(bt), D_MODEL), F32), pltpu.SemaphoreType.DMA((2,))],
        ),
        out_shape=jax.ShapeDtypeStruct((N_SLOTS, D_MODEL), F32),
        input_output_aliases=aliases,
        compiler_params=_params(1),
        name="moe_dispatch",
    )(*seg, *args)


def _experts_kernel(te_ref, tb_ref, nv_ref, x_ref, wg_ref, wu_ref, wd_ref, o_ref, xb_ref):
    t = pl.program_id(0)
    f = pl.program_id(1)
    nv = nv_ref[t]

    @pl.when(jnp.logical_and(nv > 0, f == 0))
    def _():
        rows = lax.broadcasted_iota(jnp.int32, (MOE_TILE, 1), 0)
        xb_ref[...] = jnp.where(rows < nv, x_ref[...], 0.0).astype(BF16)
        o_ref[...] = jnp.zeros_like(o_ref)

    def swiglu_rows(n_rows):
        rows = slice(0, n_rows)
        xb = xb_ref[rows, :]
        for c in range(MOE_FF_TILE // MOE_FF_SUB):
            cols = slice(c * MOE_FF_SUB, (c + 1) * MOE_FF_SUB)
            gate = jnp.dot(xb, wg_ref[:, cols].astype(BF16), preferred_element_type=F32)
            up = jnp.dot(xb, wu_ref[:, cols].astype(BF16), preferred_element_type=F32)
            act = (gate * jax.nn.sigmoid(gate) * up).astype(BF16)
            o_ref[rows, :] += jnp.dot(act, wd_ref[cols, :].astype(BF16), preferred_element_type=F32)

    lo = 0
    for n_rows in MOE_ROW_PATHS:
        @pl.when(jnp.logical_and(nv > lo, nv <= n_rows))
        def _(n_rows=n_rows):
            swiglu_rows(n_rows)
        lo = n_rows


def _experts(tile_expert, tile_blk, tile_rows, xs, wg, wu, wd, layer):
    def chunk(t, f, nv):
        return jnp.where(nv[t] > 0, f, MOE_NF - 1)
    up_spec = pl.BlockSpec((None, None, D_MODEL, MOE_FF_TILE),
                           lambda t, f, te, tb, nv: (layer, te[t], 0, chunk(t, f, nv)))
    down_spec = pl.BlockSpec((None, None, MOE_FF_TILE, D_MODEL),
                             lambda t, f, te, tb, nv: (layer, te[t], chunk(t, f, nv), 0))
    return pl.pallas_call(
        _experts_kernel,
        grid_spec=pltpu.PrefetchScalarGridSpec(
            num_scalar_prefetch=3,
            grid=(MOE_MAX_TILES, MOE_NF),
            in_specs=[pl.BlockSpec((MOE_TILE, D_MODEL), lambda t, f, te, tb, nv: (tb[t], 0)),
                      up_spec, up_spec, down_spec],
            out_specs=pl.BlockSpec((MOE_TILE, D_MODEL), lambda t, f, te, tb, nv: (tb[t], 0)),
            scratch_shapes=[pltpu.VMEM((MOE_TILE, D_MODEL), BF16)],
        ),
        out_shape=jax.ShapeDtypeStruct((N_SLOTS, D_MODEL), F32),
        compiler_params=_params(2),
        name="moe_experts",
    )(tile_expert, tile_blk, tile_rows, xs, wg, wu, wd)


def _combine_kernel(seg_n, seg_src, seg_dst, x_ref, cm_ref, eo_ref, y_ref, buf_ref, sem, *, bt, blk0, nb):
    i = pl.program_id(0)
    par = i % 2
    start = lambda cp: cp.start()
    wait = lambda cp: cp.wait()
    copies = lambda blk, p, action: _segment_copies(seg_n, seg_src, seg_dst, blk, buf_ref.at[p], eo_ref,
                                                     sem.at[p], bt, False, action)

    @pl.when(i == 0)
    def _():
        copies(blk0, 0, start)

    @pl.when(i + 1 < nb)
    def _():
        copies(blk0 + i + 1, 1 - par, start)

    copies(blk0 + i, par, wait)
    last = (blk0 + i) * N_EXPERTS + (N_EXPERTS - 1)
    used = seg_src[last] + seg_n[last]
    rows = lax.broadcasted_iota(jnp.int32, (_local_rows(bt), 1), 0)
    e_hi, e_lo = _split_bf16(jnp.where(rows < used, buf_ref[par], 0.0))
    cm = cm_ref[...]
    cols = lax.broadcasted_iota(jnp.int32, (bt, _local_rows(bt)), 1).astype(F32)
    comb = jnp.where(cols == cm[:, 0:1], cm[:, 2:3], jnp.where(cols == cm[:, 1:2], cm[:, 3:4], 0.0))
    c_hi, c_lo = _split_bf16(comb)
    y_ref[...] = x_ref[...] + jnp.dot(c_hi, e_hi, preferred_element_type=F32)
    y_ref[...] += jnp.dot(c_lo, e_hi, preferred_element_type=F32)
    y_ref[...] += jnp.dot(c_hi, e_lo, preferred_element_type=F32)


def _combine(seg, x, cm, eo, bt, blk0, tok0):
    n = x.shape[0]
    nb = n // bt
    return pl.pallas_call(
        functools.partial(_combine_kernel, bt=bt, blk0=blk0, nb=nb),
        grid_spec=pltpu.PrefetchScalarGridSpec(
            num_scalar_prefetch=3,
            grid=(nb,),
            in_specs=[pl.BlockSpec((bt, D_MODEL), lambda i, *_: (i, 0)),
                      pl.BlockSpec((bt, META_W), lambda i, *_: (tok0 // bt + i, 0)),
                      pl.BlockSpec(memory_space=pl.ANY)],
            out_specs=pl.BlockSpec((bt, D_MODEL), lambda i, *_: (i, 0)),
            scratch_shapes=[pltpu.VMEM((2, _local_rows(bt), D_MODEL), F32), pltpu.SemaphoreType.DMA((2,))],
        ),
        out_shape=jax.ShapeDtypeStruct((n, D_MODEL), F32),
        compiler_params=_params(1),
        name="moe_combine",
    )(*seg, x, cm, eo)


def _moe(xp, xs, g, w_router, wg, wu, wd, layer):
    g2 = g.reshape(1, D_MODEL)
    wr_pad = jnp.pad(w_router, ((0, 0), (0, 128 - N_EXPERTS)))
    hn, cm, ls_p, cnt_p = _router(xp, g2, wr_pad, ROW_TILE, MOE_BLOCK, 0)
    hn, cm, ls_s, cnt_s = _router(xs, g2, wr_pad, DEC_BATCH, DEC_BATCH, N_PROMPT // DEC_BATCH, prev=(hn, cm))

    nbp = N_PROMPT // MOE_BLOCK
    seg_cnt = jnp.concatenate([cnt_p[:, 0, :N_EXPERTS], cnt_s[:, 0, :N_EXPERTS]], axis=0).astype(jnp.int32)
    seg_n = (seg_cnt + (SEG_ALIGN - 1)) // SEG_ALIGN * SEG_ALIGN
    seg_src = jnp.cumsum(seg_n, axis=1) - seg_n
    slot0 = jnp.cumsum(seg_n, axis=0) - seg_n

    counts = seg_n.sum(axis=0)
    tiles_e = (counts + (MOE_TILE - 1)) // MOE_TILE
    tile_end = jnp.cumsum(tiles_e)
    tile_start = tile_end - tiles_e
    n_tiles = tile_end[-1]
    t = jnp.arange(MOE_MAX_TILES, dtype=jnp.int32)
    tc = jnp.minimum(t, n_tiles - 1)
    tile_expert = jnp.sum((tc[:, None] >= tile_end[None, :]).astype(jnp.int32), axis=1)
    rows_left = counts[tile_expert] - (tc - tile_start[tile_expert]) * MOE_TILE
    tile_rows = jnp.where(t < n_tiles, jnp.minimum(rows_left, MOE_TILE), 0).astype(jnp.int32)
    seg_dst = (tile_start * MOE_TILE)[None, :] + slot0

    cap = jnp.concatenate([jnp.full((nbp, 1), MOE_BLOCK, jnp.int32), jnp.full((1, 1), DEC_BATCH, jnp.int32)])
    seg_n = jnp.clip(seg_n, 0, cap)
    seg = (seg_n.reshape(-1), jnp.clip(seg_src, 0, 2 * cap + N_EXPERTS * SEG_ALIGN - seg_n).reshape(-1),
           jnp.clip(seg_dst, 0, N_SLOTS - MOE_BLOCK).reshape(-1))

    slots = _dispatch(seg, hn, ls_p, MOE_BLOCK, 0, 0)
    slots = _dispatch(seg, hn, ls_s, DEC_BATCH, nbp, N_PROMPT, prev=slots)
    eo = _experts(tile_expert.astype(jnp.int32), tc, tile_rows, slots, wg, wu, wd, layer)
    yp = _combine(seg, xp, cm, eo, MOE_BLOCK, 0, 0)
    ys = _combine(seg, xs, cm, eo, DEC_BATCH, nbp, N_PROMPT)
    return yp, ys


def kernel(x_prompt, x_sample, state_conv, cache_k, cache_v, norm_mix_g, norm_ffn_g, w_in_ab, conv_w, conv_b,
           conv_ln_g, conv_ln_b, q_norm_g, k_norm_g, attn_sinks, w_out_ab, w_gate_dense, w_up_dense, w_down_dense,
           w_in_c, c_ln_g, c_ln_b, w_spatial, b_spatial, w_out_c, w_router, w_gate_exp, w_up_exp, w_down_exp):
    xp = x_prompt.reshape(N_PROMPT, D_MODEL)
    xs = x_sample.reshape(DEC_BATCH, D_MODEL)
    head = jnp.arange(B_WIDTH) // HEAD_DIM
    seg = (head[:, None] == head[None, :]).astype(BF16)
    conv_p, conv_s, kp_out, vp_out, ks_out, vs_out, vc_out = [], [], [], [], [], [], []
    row = lambda a: a.reshape(1, -1)
    wo_all, wg_all = w_out_ab.astype(BF16), w_gate_dense.astype(BF16)
    wu_all, wd_all = w_up_dense.astype(BF16), w_down_dense.astype(BF16)
    w_in_all, win_c_all, wout_c_all = w_in_ab.astype(BF16), w_in_c.astype(BF16), w_out_c.astype(BF16)
    kc_all = cache_k.reshape(N_AB, DEC_BATCH, WINDOW, KV_WIDTH)
    vc_all = cache_v.reshape(N_AB, DEC_BATCH, WINDOW, KV_WIDTH)
    for layer in range(DEPTH):
        i = layer // 2
        g_mix = row(norm_mix_g[layer])
        g_ffn = row(norm_ffn_g[layer])
        if layer % 2 == 0:
            qg = row(jnp.tile(q_norm_g[i], N_HEADS))
            kg = row(jnp.tile(k_norm_g[i], N_KV_HEADS))
            a_p, q_p, k_p, v_p = _inproj(xp, g_mix, w_in_all, qg, kg, seg, ROW_TILE, i)
            a_s, q_s, k_s, v_s = _inproj(xs, g_mix, w_in_all, qg, kg, seg, DEC_BATCH, i)
            cb, lg, lb = row(conv_b[i]), row(conv_ln_g[i]), row(conv_ln_b[i])
            ca_p = _conv_prompt(a_p, conv_w[i], cb, lg, lb)
            ca_s = _conv_sample(a_s, state_conv, conv_w[i], cb, lg, lb, i)
            ob_p = _attn_prompt(q_p, k_p, v_p, attn_sinks[i])
            sinks_b = jnp.broadcast_to(attn_sinks[i].reshape(N_KV_HEADS, Q_GROUP, 1), (N_KV_HEADS, Q_GROUP, 128))
            ob_s = _attn_sample(q_s.reshape(DEC_BATCH, N_KV_HEADS, Q_GROUP, HEAD_DIM),
                                k_s.reshape(DEC_BATCH, N_KV_HEADS, HEAD_DIM),
                                v_s.reshape(DEC_BATCH, N_KV_HEADS, HEAD_DIM),
                                kc_all, vc_all, sinks_b, i)
            ob_s = ob_s.reshape(DEC_BATCH, B_WIDTH)
            conv_p.append(a_p.reshape(BATCH, SEQ, A_CH)[:, SEQ - (CONV_W - 1):])
            conv_s.append(a_s)
            last = lambda t: (t.reshape(BATCH, SEQ, KV_WIDTH)[:, SEQ - WINDOW:]
                              .reshape(BATCH, WINDOW, N_KV_HEADS, HEAD_DIM))
            kp_out.append(last(k_p))
            vp_out.append(last(v_p))
            ks_out.append(k_s.reshape(DEC_BATCH, 1, N_KV_HEADS, HEAD_DIM))
            vs_out.append(v_s.reshape(DEC_BATCH, 1, N_KV_HEADS, HEAD_DIM))
            xp = _outproj_ffn(xp, ca_p, ob_p, wo_all, g_ffn, wg_all, wu_all, wd_all, ROW_TILE, i)
            xs = _outproj_ffn(xs, ca_s, ob_s, wo_all, g_ffn, wg_all, wu_all, wd_all, DEC_BATCH, i)
        else:
            lg, lb = row(c_ln_g[i]), row(c_ln_b[i])
            xp = _gmlp_prompt(xp, g_mix, win_c_all, lg, lb, w_spatial, b_spatial[i].T, wout_c_all, ROW_TILE, i)
            gd = C_WIDTH // C_GROUPS
            w0 = row(jnp.repeat(w_spatial[i][:, 0, 0], gd))
            b0 = row(jnp.repeat(b_spatial[i][:, 0], gd))
            xs, v_s = _gmlp_sample(xs, g_mix, win_c_all, lg, lb, w0, b0, wout_c_all, i)
            vc_out.append(v_s.reshape(DEC_BATCH, 1, C_WIDTH))
            xp, xs = _moe(xp, xs, norm_ffn_g[layer], w_router[i], w_gate_exp, w_up_exp, w_down_exp, i)
    def shifted(old, new):
        crop = [(0, 0, 0)] * old.ndim
        crop[2] = (-1, 1, 0)
        moved = lax.pad(old, jnp.zeros((), old.dtype), crop)
        row = lax.broadcasted_iota(jnp.int32, old.shape, 2)
        return jnp.where(row == old.shape[2] - 1, new, moved)
    return (xp.reshape(BATCH, SEQ, D_MODEL), xs.reshape(DEC_BATCH, 1, D_MODEL),
            jnp.stack(conv_p), shifted(state_conv, jnp.stack(conv_s)[:, :, None, :]),
            jnp.stack(kp_out), jnp.stack(vp_out),
            shifted(cache_k, jnp.stack(ks_out)), shifted(cache_v, jnp.stack(vs_out)), jnp.stack(vc_out))
```
